```python
import jax, jax.numpy as jnp
from jax import lax
import numpy as np

D_MODEL = 2048
BATCH = 8
SEQ = 2048
DEPTH = 1

N_META = 16
D_RNN = D_MODEL
N_RNN_BLOCKS = 8
RNN_BLOCK = D_RNN // N_RNN_BLOCKS
CONV_WIDTH = 4
LRU_C = 8.0
LRU_MIN_RAD = 0.9
LRU_MAX_RAD = 0.999
HEAD_DIM = 64
N_Q_HEADS = D_MODEL // HEAD_DIM
N_KV_HEADS = N_Q_HEADS // 8
GROUP = N_Q_HEADS // N_KV_HEADS
D_ATTN = N_Q_HEADS * HEAD_DIM
D_KV = N_KV_HEADS * HEAD_DIM
WINDOW = 128
BLOCK = 128
ROPE_THETA = 10000.0
NEG_INF = -1e30
N_BRANCHES = 2
LN_EPS = 1e-5
DEEPNORM_ALPHA = (2.0 * DEPTH) ** 0.25
DEEPNORM_BETA = (8.0 * DEPTH) ** -0.25
OFF_GR = D_RNN
OFF_Q = 2 * D_RNN
OFF_K = OFF_Q + D_ATTN
OFF_V = OFF_K + D_KV
OFF_GA = OFF_V + D_KV
OFF_G = OFF_GA + D_ATTN
D_IN = OFF_G + N_BRANCHES * D_MODEL

kernel_name = "hybrid_rglru_swa_sink_gated_merge"


def layer_norm(x, g, b):
    xf = x.astype(jnp.float32)
    mu = xf.mean(-1, keepdims=True)
    var = jnp.square(xf - mu).mean(-1, keepdims=True)
    y = (xf - mu) * lax.rsqrt(var + LN_EPS)
    return (y * g.astype(jnp.float32) + b.astype(jnp.float32)).astype(x.dtype)


def rope(x, pos):
    half = HEAD_DIM // 2
    inv = ROPE_THETA ** (-jnp.arange(half, dtype=jnp.float32) / half)
    ang = pos.astype(jnp.float32)[:, None] * inv[None, :]
    cos = jnp.cos(ang)[None, :, None, :]
    sin = jnp.sin(ang)[None, :, None, :]
    xf = x.astype(jnp.float32)
    x1, x2 = xf[..., :half], xf[..., half:]
    return jnp.concatenate([x1 * cos - x2 * sin, x2 * cos + x1 * sin], axis=-1).astype(x.dtype)


def causal_depthwise_conv(x, w, b):
    T = x.shape[1]
    xp = jnp.pad(x, ((0, 0), (CONV_WIDTH - 1, 0), (0, 0)))
    y = b
    for k in range(CONV_WIDTH):
        s = CONV_WIDTH - 1 - k
        y = y + w[k] * xp[:, s:s + T]
    return y


def rg_lru(x, w_ra, b_ra, w_ri, b_ri, lam):
    B, T, _ = x.shape
    xb = x.reshape(B, T, N_RNN_BLOCKS, RNN_BLOCK)
    gate_r = jax.nn.sigmoid((jnp.einsum('btnc,ncd->btnd', xb, w_ra).reshape(B, T, D_RNN) + b_ra).astype(jnp.float32))
    gate_i = jax.nn.sigmoid((jnp.einsum('btnc,ncd->btnd', xb, w_ri).reshape(B, T, D_RNN) + b_ri).astype(jnp.float32))
    log_a = LRU_C * gate_r * jax.nn.log_sigmoid(lam.astype(jnp.float32))
    a = jnp.exp(log_a)
    mult = jnp.sqrt(-jnp.expm1(2.0 * log_a))
    mult = jnp.where((jnp.arange(T) == 0)[None, :, None], 1.0, mult)
    u = mult * gate_i * x.astype(jnp.float32)

    def combine(left, right):
        a1, b1 = left
        a2, b2 = right
        return a1 * a2, a2 * b1 + b2

    _, h = lax.associative_scan(combine, (a, u), axis=1)
    return h.astype(x.dtype)


def sliding_window_sink_attention(q, k, v, sinks):
    B, T = q.shape[:2]
    pad = BLOCK - N_META
    Lp = T + pad
    NB = Lp // BLOCK
    padf = lambda t: jnp.pad(t, ((0, 0), (pad, 0), (0, 0), (0, 0)))
    qb = padf(q).reshape(B, NB, BLOCK, N_KV_HEADS, GROUP, HEAD_DIM)
    kb = padf(k).reshape(B, NB, BLOCK, N_KV_HEADS, HEAD_DIM)
    vb = padf(v).reshape(B, NB, BLOCK, N_KV_HEADS, HEAD_DIM)

    def banded(t):
        prev = jnp.pad(t, ((0, 0), (1, 0), (0, 0), (0, 0), (0, 0)))[:, :NB]
        meta = jnp.broadcast_to(t[:, :1, pad:], (B, NB, N_META, N_KV_HEADS, HEAD_DIM))
        return jnp.concatenate([meta, prev, t], axis=2)

    kk, vv = banded(kb), banded(vb)

    qi = jnp.arange(NB)[:, None] * BLOCK + jnp.arange(BLOCK)[None, :]
    jb = (jnp.arange(NB)[:, None] - 1) * BLOCK + jnp.arange(2 * BLOCK)[None, :]
    jm = pad + jnp.arange(N_META)
    band_ok = ((jb[:, None, :] >= BLOCK) & (jb[:, None, :] <= qi[:, :, None])
               & (qi[:, :, None] - jb[:, None, :] < WINDOW))
    meta_ok = jnp.broadcast_to(jm[None, None, :] <= qi[:, :, None], (NB, BLOCK, N_META))
    mask = jnp.concatenate([meta_ok, band_ok], axis=-1)

    s = jnp.einsum('bnqkgd,bnskd->bnkgqs', qb, kk).astype(jnp.float32) * (HEAD_DIM ** -0.5)
    s = jnp.where(mask[None, :, None, None], s, NEG_INF)
    sink = sinks.astype(jnp.float32).reshape(N_KV_HEADS, GROUP)[None, None, :, :, None, None]
    m = jnp.maximum(s.max(-1, keepdims=True), sink)
    p = jnp.exp(s - m)
    denom = p.sum(-1, keepdims=True) + jnp.exp(sink - m)
    o = jnp.einsum('bnkgqs,bnskd->bnqkgd', (p / denom).astype(v.dtype), vv)
    return o.reshape(B, Lp, D_ATTN)[:, pad:]


def hybrid_layer(h, pos, w_in, b_in, conv_w, conv_b, w_ra, b_ra, w_ri, b_ri, lam, sinks,
                 w_rnn_out, w_attn_out, w_o, b_o, ln_g, ln_b):
    B, T, _ = h.shape
    z = h @ w_in + b_in
    xr, gr, q, k, v, ga, mg = jnp.split(z, [OFF_GR, OFF_Q, OFF_K, OFF_V, OFF_GA, OFF_G], axis=-1)

    hr = rg_lru(causal_depthwise_conv(xr, conv_w, conv_b), w_ra, b_ra, w_ri, b_ri, lam)
    y_a = (hr * jax.nn.silu(gr)) @ w_rnn_out

    q = rope(q.reshape(B, T, N_Q_HEADS, HEAD_DIM), pos)
    k = rope(k.reshape(B, T, N_KV_HEADS, HEAD_DIM), pos)
    v = v.reshape(B, T, N_KV_HEADS, HEAD_DIM)
    o = sliding_window_sink_attention(q, k, v, sinks)
    y_b = (o * jax.nn.silu(ga)) @ w_attn_out

    g = jax.nn.sigmoid(mg.astype(jnp.float32)).astype(h.dtype)
    mixed = g[..., :D_MODEL] * y_a + g[..., D_MODEL:] * y_b
    out = mixed @ w_o + b_o
    return layer_norm(DEEPNORM_ALPHA * h + out, ln_g, ln_b)


def setup_inputs(seed: int = 0) -> dict:
    key = jax.random.key(seed)
    ks = jax.random.split(key, 24)
    f32 = jnp.float32
    nrm = lambda k, shape, scale: jax.random.normal(k, shape, f32) * scale
    u = jax.random.uniform(ks[12], (DEPTH, D_RNN), f32, LRU_MIN_RAD, LRU_MAX_RAD)
    s_rad = u ** (1.0 / LRU_C)
    lru_lambda = jnp.log(s_rad) - jnp.log1p(-s_rad)
    return {
        "x": jax.random.normal(ks[0], (BATCH, SEQ, D_MODEL), f32),
        "meta_tokens": nrm(ks[1], (N_META, D_MODEL), 1.0),
        "ln_emb_g": 1.0 + nrm(ks[2], (D_MODEL,), 0.01),
        "ln_emb_b": nrm(ks[3], (D_MODEL,), 0.01),
        "w_in": nrm(ks[4], (DEPTH, D_MODEL, D_IN), D_MODEL ** -0.5),
        "b_in": nrm(ks[5], (DEPTH, D_IN), 0.01),
        "conv_w": nrm(ks[6], (DEPTH, CONV_WIDTH, D_RNN), CONV_WIDTH ** -0.5),
        "conv_b": nrm(ks[7], (DEPTH, D_RNN), 0.01),
        "w_ra": nrm(ks[8], (DEPTH, N_RNN_BLOCKS, RNN_BLOCK, RNN_BLOCK), RNN_BLOCK ** -0.5),
        "b_ra": nrm(ks[9], (DEPTH, D_RNN), 0.01),
        "w_ri": nrm(ks[10], (DEPTH, N_RNN_BLOCKS, RNN_BLOCK, RNN_BLOCK), RNN_BLOCK ** -0.5),
        "b_ri": nrm(ks[11], (DEPTH, D_RNN), 0.01),
        "lru_lambda": lru_lambda,
        "sinks": nrm(ks[13], (DEPTH, N_Q_HEADS), 0.5),
        "w_rnn_out": nrm(ks[14], (DEPTH, D_RNN, D_MODEL), D_RNN ** -0.5 * DEEPNORM_BETA),
        "w_attn_out": nrm(ks[15], (DEPTH, D_ATTN, D_MODEL), D_ATTN ** -0.5 * DEEPNORM_BETA),
        "w_o": nrm(ks[16], (DEPTH, D_MODEL, D_MODEL), D_MODEL ** -0.5 * DEEPNORM_BETA),
        "b_o": nrm(ks[17], (DEPTH, D_MODEL), 0.01),
        "ln_g": 1.0 + nrm(ks[18], (DEPTH, D_MODEL), 0.01),
        "ln_b": nrm(ks[19], (DEPTH, D_MODEL), 0.01),
    }


def reference(x, meta_tokens, ln_emb_g, ln_emb_b, w_in, b_in, conv_w, conv_b, w_ra, b_ra, w_ri, b_ri,
              lru_lambda, sinks, w_rnn_out, w_attn_out, w_o, b_o, ln_g, ln_b):
    B = x.shape[0]
    meta = jnp.broadcast_to(meta_tokens.astype(x.dtype)[None], (B, N_META, D_MODEL))
    h = jnp.concatenate([meta, x], axis=1)
    h = layer_norm(h, ln_emb_g, ln_emb_b)
    pos = jnp.arange(h.shape[1])
    for l in range(DEPTH):
        h = hybrid_layer(h, pos, w_in[l], b_in[l], conv_w[l], conv_b[l], w_ra[l], b_ra[l], w_ri[l], b_ri[l],
                         lru_lambda[l], sinks[l], w_rnn_out[l], w_attn_out[l], w_o[l], b_o[l], ln_g[l], ln_b[l])
    return h[:, N_META:]
```

```python
import functools

import jax
import jax.numpy as jnp
from jax import lax
from jax.experimental import pallas as pl
from jax.experimental.pallas import tpu as pltpu

D_MODEL = 2048
N_META = 16
N_RNN_BLOCKS = 8
RNN_BLOCK = 256
CONV_WIDTH = 4
LRU_C = 8.0
HEAD_DIM = 64
HALF = HEAD_DIM // 2
N_Q_HEADS = 32
N_KV_HEADS = 4
GROUP = 8
D_KV = N_KV_HEADS * HEAD_DIM
BLOCK = 128
ROPE_THETA = 10000.0
NEG_INF = -1e30
LN_EPS = 1e-5
DEPTH = 1
DEEPNORM_ALPHA = (2.0 * DEPTH) ** 0.25
OFF_GR = D_MODEL
OFF_Q = 2 * D_MODEL
OFF_K = OFF_Q + D_MODEL
OFF_V = OFF_K + D_KV
OFF_GA = OFF_V + D_KV
OFF_G = OFF_GA + D_MODEL

V7X_SUBLANES = 8
V7X_LANES = 128
V7X_VMEM_LIMIT = 56 * 1024 * 1024

KV_SLOT = V7X_LANES
HIST_ROWS = (CONV_WIDTH - 1) * V7X_SUBLANES

BF16 = jnp.bfloat16
F32 = jnp.float32


def _layer_norm(x, g, b):
    mu = jnp.mean(x, axis=-1, keepdims=True)
    xc = x - mu
    var = jnp.mean(xc * xc, axis=-1, keepdims=True)
    return xc * lax.rsqrt(var + LN_EPS) * g + b


def _sigmoid(x):
    return 1.0 / (1.0 + jnp.exp(-x))


def _dot(a, b):
    return jnp.dot(a, b, preferred_element_type=F32)


def _dot_nt(a, b):
    return lax.dot_general(a, b, (((1,), (1,)), ((), ())), preferred_element_type=F32)


def _const_spec(shape):
    nd = len(shape)
    return pl.BlockSpec(shape, lambda *_: (0,) * nd, pipeline_mode=pl.Buffered(1))


def _rnn_kernel(x_ref, lng_ref, lnb_ref, wx_ref, wg_ref, bx_ref, bg_ref, cw_ref, cb_ref,
                wra_ref, bra_ref, wri_ref, bri_ref, lam_ref, wout_ref, hist0_ref, state0_ref,
                ya_ref, hist_out_ref, state_out_ref, hist_s, state_s, *, tt, seq_start):
    rows = tt * V7X_SUBLANES
    step = pl.program_id(0)

    @pl.when(step == 0)
    def _():
        hist_s[...] = hist0_ref[...]
        state_s[...] = state0_ref[...]

    x = x_ref[...].reshape(rows, D_MODEL)
    h = _layer_norm(x, lng_ref[...], lnb_ref[...]).astype(BF16)

    ys = []
    for n in range(N_RNN_BLOCKS):
        xr = _dot(h, wx_ref[n]) + bx_ref[n]
        gr = _dot(h, wg_ref[n]) + bg_ref[n]
        xe = jnp.concatenate([hist_s[n], xr], axis=0)
        cw = cw_ref[n]
        conv = cb_ref[n] + cw[0:1] * xr
        for k in range(1, CONV_WIDTH):
            off = HIST_ROWS - k * V7X_SUBLANES
            conv = conv + cw[k:k + 1] * xe[off:off + rows]
        hist_s[n] = xe[rows:rows + HIST_ROWS]
        cb16 = conv.astype(BF16)
        gate_r = _sigmoid(_dot(cb16, wra_ref[n]) + bra_ref[n])
        gate_i = _sigmoid(_dot(cb16, wri_ref[n]) + bri_ref[n])
        lam = lam_ref[n]
        log_sig = jnp.minimum(lam, 0.0) - jnp.log(1.0 + jnp.exp(-jnp.abs(lam)))
        a = jnp.exp((LRU_C * gate_r) * log_sig)
        mult = jnp.sqrt(1.0 - a * a)
        if seq_start:
            first = lax.broadcasted_iota(jnp.int32, (rows, RNN_BLOCK), 0) < V7X_SUBLANES
            mult = jnp.where(first & (step == 0), 1.0, mult)
        u = mult * gate_i * conv
        hcur = state_s[n]
        hs = []
        for t in range(tt):
            sl = slice(t * V7X_SUBLANES, (t + 1) * V7X_SUBLANES)
            hcur = a[sl] * hcur + u[sl]
            hs.append(hcur)
        state_s[n] = hcur
        hr = jnp.concatenate(hs, axis=0)
        ys.append((hr * (gr * _sigmoid(gr))).astype(BF16))
    y = jnp.concatenate(ys, axis=1)
    ya = _dot(y, wout_ref[...])
    ya_ref[...] = ya.reshape(tt, V7X_SUBLANES, D_MODEL).astype(ya_ref.dtype)

    @pl.when(step == pl.num_programs(0) - 1)
    def _():
        hist_out_ref[...] = hist_s[...]
        state_out_ref[...] = state_s[...]


def _rnn_branch(xt, p, hist0, state0, *, tt, seq_start):
    t_len = xt.shape[0]
    nb, rb = N_RNN_BLOCKS, RNN_BLOCK
    kern = functools.partial(_rnn_kernel, tt=tt, seq_start=seq_start)
    in_specs = [
        pl.BlockSpec((tt, V7X_SUBLANES, D_MODEL), lambda i: (i, 0, 0)),
        _const_spec((1, D_MODEL)), _const_spec((1, D_MODEL)),
        _const_spec((nb, D_MODEL, rb)), _const_spec((nb, D_MODEL, rb)),
        _const_spec((nb, 1, rb)), _const_spec((nb, 1, rb)),
        _const_spec((nb, CONV_WIDTH, rb)), _const_spec((nb, 1, rb)),
        _const_spec((nb, rb, rb)), _const_spec((nb, 1, rb)),
        _const_spec((nb, rb, rb)), _const_spec((nb, 1, rb)),
        _const_spec((nb, 1, rb)),
        _const_spec((D_MODEL, D_MODEL)),
        _const_spec((nb, HIST_ROWS, rb)), _const_spec((nb, V7X_SUBLANES, rb)),
    ]
    out_shape = (
        jax.ShapeDtypeStruct((t_len, V7X_SUBLANES, D_MODEL), BF16),
        jax.ShapeDtypeStruct((nb, HIST_ROWS, rb), F32),
        jax.ShapeDtypeStruct((nb, V7X_SUBLANES, rb), F32),
    )
    out_specs = (
        pl.BlockSpec((tt, V7X_SUBLANES, D_MODEL), lambda i: (i, 0, 0)),
        pl.BlockSpec((nb, HIST_ROWS, rb), lambda i: (0, 0, 0)),
        pl.BlockSpec((nb, V7X_SUBLANES, rb), lambda i: (0, 0, 0)),
    )
    return pl.pallas_call(
        kern,
        grid=(t_len // tt,),
        in_specs=in_specs,
        out_specs=out_specs,
        out_shape=out_shape,
        scratch_shapes=[pltpu.VMEM((nb, HIST_ROWS, rb), F32), pltpu.VMEM((nb, V7X_SUBLANES, rb), F32)],
        compiler_params=pltpu.CompilerParams(dimension_semantics=("arbitrary",),
                                             vmem_limit_bytes=V7X_VMEM_LIMIT),
        name="rnn_branch_start" if seq_start else "rnn_branch",
    )(xt, p["ln_g"], p["ln_b"], p["wx"], p["wg"], p["bx"], p["bg"], p["cw"], p["cb"],
      p["wra"], p["bra"], p["wri"], p["bri"], p["lam"], p["w_rnn_out"], hist0, state0)


def _rope_rows(k, cos, sin_signed):
    lane = lax.broadcasted_iota(jnp.int32, k.shape, 1)
    partner = jnp.where((lane % HEAD_DIM) < HALF,
                        pltpu.roll(k, V7X_LANES - HALF, axis=1),
                        pltpu.roll(k, HALF, axis=1))
    return k * cos + partner * sin_signed


def _meta_kv_kernel(m_ref, lng_ref, lnb_ref, wk_ref, bk_ref, wv_ref, bv_ref, cos_ref, sin_ref,
                    k_ref, v_ref):
    h = _layer_norm(m_ref[...], lng_ref[...], lnb_ref[...]).astype(BF16)
    k = _dot(h, wk_ref[...]) + bk_ref[...]
    v = _dot(h, wv_ref[...]) + bv_ref[...]
    for g in range(N_KV_HEADS):
        sl = slice(g * KV_SLOT, (g + 1) * KV_SLOT)
        k_ref[:, sl] = _rope_rows(k[:, sl], cos_ref[...], sin_ref[...]).astype(BF16)
    v_ref[...] = v.astype(BF16)


def _meta_kv(meta, p, cos_rows, sin_rows):
    return pl.pallas_call(
        _meta_kv_kernel,
        out_shape=(jax.ShapeDtypeStruct((N_META, N_KV_HEADS * KV_SLOT), BF16),
                   jax.ShapeDtypeStruct((N_META, D_KV), BF16)),
        name="meta_kv",
    )(meta, p["ln_g"], p["ln_b"], p["wk"], p["bk"], p["wv"], p["bv"], cos_rows, sin_rows)


def _attn_kernel(x_ref, lng_ref, lnb_ref, wqt_ref, bqt_ref, wk_ref, bk_ref, wvt_ref, bvt_ref,
                 wgat_ref, bgat_ref, wout_ref, cost_ref, sint_ref, cosr_ref, sinr_ref,
                 kmeta_ref, vmetat_ref, sink_ref, yb_ref, kprev_s, vprev_s, ot_s, *, tq):
    nblk = tq // BLOCK
    i = pl.program_id(1)

    @pl.when(i == 0)
    def _():
        kprev_s[...] = jnp.zeros_like(kprev_s)
        vprev_s[...] = jnp.zeros_like(vprev_s)

    h = _layer_norm(x_ref[0], lng_ref[...], lnb_ref[...]).astype(BF16)

    def lanes_bias(b_ref):
        return jnp.concatenate([b_ref[...]] * (tq // V7X_LANES), axis=1)

    qt = _dot_nt(wqt_ref[...], h) + lanes_bias(bqt_ref)
    q4 = qt.reshape(N_Q_HEADS, 2, HALF, tq)
    scale = HEAD_DIM ** -0.5
    cos = cost_ref[...] * scale
    sin = sint_ref[...] * scale
    q1, q2 = q4[:, 0], q4[:, 1]
    qt = jnp.stack([q1 * cos - q2 * sin, q2 * cos + q1 * sin], axis=1)
    qt = qt.reshape(D_MODEL, tq).astype(BF16)

    k = _dot(h, wk_ref[...]) + bk_ref[...]
    k = jnp.concatenate(
        [_rope_rows(k[:, g * KV_SLOT:(g + 1) * KV_SLOT], cosr_ref[...], sinr_ref[...])
         for g in range(N_KV_HEADS)], axis=1).astype(BF16)
    vt = (_dot_nt(wvt_ref[...], h) + lanes_bias(bvt_ref)).astype(BF16)

    kmeta = kmeta_ref[...]
    vmetat = vmetat_ref[...]

    key_row = lax.broadcasted_iota(jnp.int32, (BLOCK, GROUP * BLOCK), 0)
    qry_col = lax.broadcasted_iota(jnp.int32, (BLOCK, GROUP * BLOCK), 1) % BLOCK
    cur_ok = key_row <= qry_col
    prev_in_window = key_row > qry_col

    for jb in range(nblk):
        tok = slice(jb * BLOCK, (jb + 1) * BLOCK)
        if jb == 0:
            k_prev, vt_prev = kprev_s[...], vprev_s[...]
            prev_ok = prev_in_window & (i > 0)
        else:
            ptok = slice((jb - 1) * BLOCK, jb * BLOCK)
            k_prev, vt_prev = k[ptok], vt[:, ptok]
            prev_ok = prev_in_window
        k_cur, vt_cur = k[tok], vt[:, tok]
        for g in range(N_KV_HEADS):
            ks = slice(g * KV_SLOT, g * KV_SLOT + HEAD_DIM)
            vs = slice(g * HEAD_DIM, (g + 1) * HEAD_DIM)
            q_g = jnp.concatenate(
                [qt[(g * GROUP + hh) * HEAD_DIM:(g * GROUP + hh + 1) * HEAD_DIM, tok]
                 for hh in range(GROUP)], axis=1)
            s_prev = jnp.where(prev_ok, _dot(k_prev[:, ks], q_g), NEG_INF)
            s_cur = jnp.where(cur_ok, _dot(k_cur[:, ks], q_g), NEG_INF)
            s_meta = _dot(kmeta[:, ks], q_g)
            sink = sink_ref[g]
            m = jnp.maximum(
                jnp.maximum(jnp.max(s_prev, axis=0, keepdims=True), jnp.max(s_cur, axis=0, keepdims=True)),
                jnp.maximum(jnp.max(s_meta, axis=0, keepdims=True), sink))
            p_prev = jnp.exp(s_prev - m)
            p_cur = jnp.exp(s_cur - m)
            p_meta = jnp.exp(s_meta - m)
            denom = (jnp.sum(p_prev, axis=0, keepdims=True) + jnp.sum(p_cur, axis=0, keepdims=True)
                     + jnp.sum(p_meta, axis=0, keepdims=True) + jnp.exp(sink - m))
            o = (_dot(vt_prev[vs], p_prev.astype(BF16)) + _dot(vt_cur[vs], p_cur.astype(BF16))
                 + _dot(vmetat[vs], p_meta.astype(BF16)))
            o = o / denom
            for hh in range(GROUP):
                r0 = (g * GROUP + hh) * HEAD_DIM
                ot_s[r0:r0 + HEAD_DIM, tok] = o[:, hh * BLOCK:(hh + 1) * BLOCK]

    kprev_s[...] = k[tq - BLOCK:]
    vprev_s[...] = vt[:, tq - BLOCK:]

    gat = _dot_nt(wgat_ref[...], h) + lanes_bias(bgat_ref)
    act = (ot_s[...] * (gat * _sigmoid(gat))).astype(BF16)
    yb = lax.dot_general(act, wout_ref[...], (((0,), (0,)), ((), ())), preferred_element_type=F32)
    yb_ref[0] = yb.astype(yb_ref.dtype)


def _attn_branch(x, p, tabs, kmeta, vmetat, *, tq):
    b, t_len, _ = x.shape
    kern = functools.partial(_attn_kernel, tq=tq)
    in_specs = [
        pl.BlockSpec((1, tq, D_MODEL), lambda bi, i: (bi, i, 0)),
        _const_spec((1, D_MODEL)), _const_spec((1, D_MODEL)),
        _const_spec((D_MODEL, D_MODEL)), _const_spec((D_MODEL, V7X_LANES)),
        _const_spec((D_MODEL, N_KV_HEADS * KV_SLOT)), _const_spec((1, N_KV_HEADS * KV_SLOT)),
        _const_spec((D_KV, D_MODEL)), _const_spec((D_KV, V7X_LANES)),
        _const_spec((D_MODEL, D_MODEL)), _const_spec((D_MODEL, V7X_LANES)),
        _const_spec((D_MODEL, D_MODEL)),
        pl.BlockSpec((HALF, tq), lambda bi, i: (0, i)), pl.BlockSpec((HALF, tq), lambda bi, i: (0, i)),
        pl.BlockSpec((tq, KV_SLOT), lambda bi, i: (i, 0)), pl.BlockSpec((tq, KV_SLOT), lambda bi, i: (i, 0)),
        _const_spec((N_META, N_KV_HEADS * KV_SLOT)), _const_spec((D_KV, N_META)),
        _const_spec((N_KV_HEADS, 1, GROUP * BLOCK)),
    ]
    return pl.pallas_call(
        kern,
        grid=(b, t_len // tq),
        in_specs=in_specs,
        out_specs=pl.BlockSpec((1, tq, D_MODEL), lambda bi, i: (bi, i, 0)),
        out_shape=jax.ShapeDtypeStruct((b, t_len, D_MODEL), BF16),
        scratch_shapes=[pltpu.VMEM((BLOCK, N_KV_HEADS * KV_SLOT), BF16),
                        pltpu.VMEM((D_KV, BLOCK), BF16),
                        pltpu.VMEM((D_MODEL, tq), F32)],
        compiler_params=pltpu.CompilerParams(dimension_semantics=("arbitrary", "arbitrary"),
                                             vmem_limit_bytes=V7X_VMEM_LIMIT),
        name="attn_branch",
    )(x, p["ln_g"], p["ln_b"], p["wqt"], p["bqt"], p["wk"], p["bk"], p["wvt"], p["bvt"],
      p["wgat"], p["bgat"], p["w_attn_out"], tabs["cos_t"], tabs["sin_t"], tabs["cos_r"], tabs["sin_r"],
      kmeta, vmetat, p["sink_rows"])


def _merge_kernel(x_ref, ya_ref, yb_ref, lng_ref, lnb_ref, wmg_ref, bmg_ref, wo_ref, bo_ref,
                  g2_ref, b2_ref, out_ref):
    h32 = _layer_norm(x_ref[0], lng_ref[...], lnb_ref[...])
    h = h32.astype(BF16)
    ga = _sigmoid(_dot(h, wmg_ref[:, :D_MODEL]) + bmg_ref[:, :D_MODEL])
    gb = _sigmoid(_dot(h, wmg_ref[:, D_MODEL:]) + bmg_ref[:, D_MODEL:])
    mixed = ga * ya_ref[0].astype(F32) + gb * yb_ref[0].astype(F32)
    out = _dot(mixed.astype(BF16), wo_ref[...]) + bo_ref[...]
    out_ref[0] = _layer_norm(DEEPNORM_ALPHA * h32 + out, g2_ref[...], b2_ref[...])


def _merge(x, ya, yb, p, *, tm):
    b, t_len, _ = x.shape
    tile = pl.BlockSpec((1, tm, D_MODEL), lambda bi, i: (bi, i, 0))
    in_specs = [tile, tile, tile,
                _const_spec((1, D_MODEL)), _const_spec((1, D_MODEL)),
                _const_spec((D_MODEL, 2 * D_MODEL)), _const_spec((1, 2 * D_MODEL)),
                _const_spec((D_MODEL, D_MODEL)), _const_spec((1, D_MODEL)),
                _const_spec((1, D_MODEL)), _const_spec((1, D_MODEL))]
    return pl.pallas_call(
        _merge_kernel,
        grid=(b, t_len // tm),
        in_specs=in_specs,
        out_specs=tile,
        out_shape=jax.ShapeDtypeStruct((b, t_len, D_MODEL), F32),
        compiler_params=pltpu.CompilerParams(dimension_semantics=("arbitrary", "arbitrary"),
                                             vmem_limit_bytes=V7X_VMEM_LIMIT),
        name="merge",
    )(x, ya, yb, p["ln_g"], p["ln_b"], p["wmg"], p["bmg"], p["w_o"], p["b_o"], p["ln2_g"], p["ln2_b"])


def _rope_tables(t_total):
    inv = ROPE_THETA ** (-jnp.arange(HALF, dtype=F32) / HALF)
    ang = jnp.arange(t_total, dtype=F32)[:, None] * inv[None, :]
    cos, sin = jnp.cos(ang), jnp.sin(ang)
    zeros = jnp.zeros_like(cos)
    cos_r = jnp.concatenate([cos, cos, zeros, zeros], axis=1)
    sin_r = jnp.concatenate([-sin, sin, zeros, zeros], axis=1)
    return cos, sin, cos_r, sin_r


def _slotted(w):
    lead = w.shape[:-1]
    w4 = w.reshape(lead + (N_KV_HEADS, HEAD_DIM))
    w4 = jnp.concatenate([w4, jnp.zeros_like(w4)], axis=-1)
    return w4.reshape(lead + (N_KV_HEADS * KV_SLOT,))


def kernel(x, meta_tokens, ln_emb_g, ln_emb_b, w_in, b_in, conv_w, conv_b, w_ra, b_ra, w_ri, b_ri,
           lru_lambda, sinks, w_rnn_out, w_attn_out, w_o, b_o, ln_g, ln_b):
    b, seq, _ = x.shape
    assert b == V7X_SUBLANES and w_in.shape[0] == DEPTH
    nb, rb = N_RNN_BLOCKS, RNN_BLOCK
    w = w_in[0]
    bi = b_in[0]
    row = lambda v: v.reshape(1, -1)
    blocks = lambda v: v.reshape(nb, 1, rb)
    lanes = lambda v: jnp.broadcast_to(v[:, None], (v.shape[0], V7X_LANES))
    col_blocks = lambda m: m.reshape(D_MODEL, nb, rb).transpose(1, 0, 2).astype(BF16)

    common = {"ln_g": row(ln_emb_g), "ln_b": row(ln_emb_b)}
    p_rnn = dict(common,
                 wx=col_blocks(w[:, :OFF_GR]), wg=col_blocks(w[:, OFF_GR:OFF_Q]),
                 bx=blocks(bi[:OFF_GR]), bg=blocks(bi[OFF_GR:OFF_Q]),
                 cw=conv_w[0].reshape(CONV_WIDTH, nb, rb).transpose(1, 0, 2), cb=blocks(conv_b[0]),
                 wra=w_ra[0].astype(BF16), bra=blocks(b_ra[0]),
                 wri=w_ri[0].astype(BF16), bri=blocks(b_ri[0]),
                 lam=blocks(lru_lambda[0]), w_rnn_out=w_rnn_out[0].astype(BF16))
    p_attn = dict(common,
                  wqt=w[:, OFF_Q:OFF_K].T.astype(BF16), bqt=lanes(bi[OFF_Q:OFF_K]),
                  wk=_slotted(w[:, OFF_K:OFF_V]).astype(BF16), bk=row(_slotted(bi[OFF_K:OFF_V])),
                  wvt=w[:, OFF_V:OFF_GA].T.astype(BF16), bvt=lanes(bi[OFF_V:OFF_GA]),
                  wgat=w[:, OFF_GA:OFF_G].T.astype(BF16), bgat=lanes(bi[OFF_GA:OFF_G]),
                  wv=w[:, OFF_V:OFF_GA].astype(BF16), bv=row(bi[OFF_V:OFF_GA]),
                  w_attn_out=w_attn_out[0].astype(BF16),
                  sink_rows=jnp.repeat(sinks[0].astype(F32), BLOCK).reshape(N_KV_HEADS, 1, GROUP * BLOCK))
    p_merge = dict(common, wmg=w[:, OFF_G:].astype(BF16), bmg=row(bi[OFF_G:]),
                   w_o=w_o[0].astype(BF16), b_o=row(b_o[0]), ln2_g=row(ln_g[0]), ln2_b=row(ln_b[0]))

    cos, sin, cos_r, sin_r = _rope_tables(N_META + seq)
    tabs = {"cos_t": cos[N_META:].T, "sin_t": sin[N_META:].T, "cos_r": cos_r[N_META:], "sin_r": sin_r[N_META:]}

    meta = meta_tokens.astype(x.dtype)
    meta_t = jnp.broadcast_to(meta[:, None, :], (N_META, V7X_SUBLANES, D_MODEL))
    zero_hist = jnp.zeros((nb, HIST_ROWS, rb), F32)
    zero_state = jnp.zeros((nb, V7X_SUBLANES, rb), F32)
    _, hist0, state0 = _rnn_branch(meta_t, p_rnn, zero_hist, zero_state, tt=N_META, seq_start=True)
    kmeta, vmeta = _meta_kv(meta, p_attn, cos_r[:N_META], sin_r[:N_META])

    xt = jnp.transpose(x, (1, 0, 2))
    ya_t, _, _ = _rnn_branch(xt, p_rnn, hist0, state0, tt=32, seq_start=False)
    ya = jnp.transpose(ya_t, (1, 0, 2))
    yb = _attn_branch(x, p_attn, tabs, kmeta, vmeta.T, tq=256)
    return _merge(x, ya, yb, p_merge, tm=256)
```

```python
import functools

import jax
import jax.numpy as jnp
from jax import lax
from jax.experimental import pallas as pl
from jax.experimental.pallas import tpu as pltpu

D_MODEL = 2048
N_META = 16
N_RNN_BLOCKS = 8
RNN_BLOCK = 256
CONV_WIDTH = 4
LRU_C = 8.0
HEAD_DIM = 64
HALF = HEAD_DIM // 2
N_Q_HEADS = 32
N_KV_HEADS = 4
GROUP = 8
D_KV = N_KV_HEADS * HEAD_DIM
BLOCK = 128
ROPE_THETA = 10000.0
NEG_INF = -1e30
LN_EPS = 1e-5
DEPTH = 1
DEEPNORM_ALPHA = (2.0 * DEPTH) ** 0.25
OFF_GR = D_MODEL
OFF_Q = 2 * D_MODEL
OFF_K = OFF_Q + D_MODEL
OFF_V = OFF_K + D_KV
OFF_GA = OFF_V + D_KV
OFF_G = OFF_GA + D_MODEL

V7X_SUBLANES = 8
V7X_LANES = 128
V7X_VMEM_LIMIT = 56 * 1024 * 1024

KV_SLOT = V7X_LANES
HIST_ROWS = (CONV_WIDTH - 1) * V7X_SUBLANES

BF16 = jnp.bfloat16
F32 = jnp.float32


def _layer_norm(x, g, b):
    mu = jnp.mean(x, axis=-1, keepdims=True)
    xc = x - mu
    var = jnp.mean(xc * xc, axis=-1, keepdims=True)
    return xc * lax.rsqrt(var + LN_EPS) * g + b


def _sigmoid(x):
    return 1.0 / (1.0 + jnp.exp(-x))


def _dot(a, b):
    return jnp.dot(a, b, preferred_element_type=F32)


def _dot_nt(a, b):
    return lax.dot_general(a, b, (((1,), (1,)), ((), ())), preferred_element_type=F32)


def _const_spec(shape):
    nd = len(shape)
    return pl.BlockSpec(shape, lambda *_: (0,) * nd, pipeline_mode=pl.Buffered(1))


def _rnn_kernel(x_ref, lng_ref, lnb_ref, wx_ref, wg_ref, bx_ref, bg_ref, cw_ref, cb_ref,
                wra_ref, bra_ref, wri_ref, bri_ref, lam_ref, wout_ref, hist0_ref, state0_ref,
                ya_ref, hist_out_ref, state_out_ref, xg_s, hist_s, state_s, *, tt, seq_start):
    rows = tt * V7X_SUBLANES
    nb = N_RNN_BLOCKS
    j = pl.program_id(0)
    w_slot = j % 2
    r_slot = 1 - w_slot

    @pl.when(j == 0)
    def _():
        xg_s[...] = jnp.zeros_like(xg_s)
        hist_s[...] = hist0_ref[...]
        state_s[...] = state0_ref[...]

    x = x_ref[...].reshape(rows, D_MODEL)
    h = _layer_norm(x, lng_ref[...], lnb_ref[...]).astype(BF16)
    for n in range(nb):
        xg_s[w_slot, n] = _dot(h, wx_ref[n]) + bx_ref[n]
        xg_s[w_slot, nb + n] = _dot(h, wg_ref[n]) + bg_ref[n]

    ys = []
    for n in range(nb):
        xr = xg_s[r_slot, n]
        gr = xg_s[r_slot, nb + n]
        xe = jnp.concatenate([hist_s[n], xr], axis=0)
        cw = cw_ref[n]
        conv = cb_ref[n] + cw[0:1] * xr
        for k in range(1, CONV_WIDTH):
            off = HIST_ROWS - k * V7X_SUBLANES
            conv = conv + cw[k:k + 1] * xe[off:off + rows]
        hist_s[n] = xe[rows:rows + HIST_ROWS]
        cb16 = conv.astype(BF16)
        gate_r = _sigmoid(_dot(cb16, wra_ref[n]) + bra_ref[n])
        gate_i = _sigmoid(_dot(cb16, wri_ref[n]) + bri_ref[n])
        lam = lam_ref[n]
        log_sig = jnp.minimum(lam, 0.0) - jnp.log1p(jnp.exp(-jnp.abs(lam)))
        a = jnp.exp((LRU_C * gate_r) * log_sig)
        mult = jnp.sqrt(1.0 - a * a)
        if seq_start:
            first = lax.broadcasted_iota(jnp.int32, (rows, RNN_BLOCK), 0) < V7X_SUBLANES
            mult = jnp.where(first & (j == 1), 1.0, mult)
        u = mult * gate_i * conv
        hcur = state_s[n]
        hs = []
        for t in range(tt):
            sl = slice(t * V7X_SUBLANES, (t + 1) * V7X_SUBLANES)
            hcur = a[sl] * hcur + u[sl]
            hs.append(hcur)
        state_s[n] = hcur
        hr = jnp.concatenate(hs, axis=0)
        ys.append((hr * (gr * _sigmoid(gr))).astype(BF16))
    y = jnp.concatenate(ys, axis=1)
    ya = _dot(y, wout_ref[...])
    ya_ref[...] = ya.reshape(tt, V7X_SUBLANES, D_MODEL).astype(ya_ref.dtype)

    @pl.when(j == 0)
    def _():
        hist_s[...] = hist0_ref[...]
        state_s[...] = state0_ref[...]

    @pl.when(j == pl.num_programs(0) - 1)
    def _():
        hist_out_ref[...] = hist_s[...]
        state_out_ref[...] = state_s[...]


def _rnn_branch(xt, p, hist0, state0, *, tt, seq_start):
    t_len = xt.shape[0]
    nb, rb = N_RNN_BLOCKS, RNN_BLOCK
    kern = functools.partial(_rnn_kernel, tt=tt, seq_start=seq_start)
    n_tiles = t_len // tt
    in_specs = [
        pl.BlockSpec((tt, V7X_SUBLANES, D_MODEL), lambda j: (jnp.minimum(j, n_tiles - 1), 0, 0)),
        _const_spec((1, D_MODEL)), _const_spec((1, D_MODEL)),
        _const_spec((nb, D_MODEL, rb)), _const_spec((nb, D_MODEL, rb)),
        _const_spec((nb, 1, rb)), _const_spec((nb, 1, rb)),
        _const_spec((nb, CONV_WIDTH, rb)), _const_spec((nb, 1, rb)),
        _const_spec((nb, rb, rb)), _const_spec((nb, 1, rb)),
        _const_spec((nb, rb, rb)), _const_spec((nb, 1, rb)),
        _const_spec((nb, 1, rb)),
        _const_spec((D_MODEL, D_MODEL)),
        _const_spec((nb, HIST_ROWS, rb)), _const_spec((nb, V7X_SUBLANES, rb)),
    ]
    out_shape = (
        jax.ShapeDtypeStruct((t_len, V7X_SUBLANES, D_MODEL), BF16),
        jax.ShapeDtypeStruct((nb, HIST_ROWS, rb), F32),
        jax.ShapeDtypeStruct((nb, V7X_SUBLANES, rb), F32),
    )
    out_specs = (
        pl.BlockSpec((tt, V7X_SUBLANES, D_MODEL), lambda j: (jnp.maximum(j - 1, 0), 0, 0)),
        pl.BlockSpec((nb, HIST_ROWS, rb), lambda j: (0, 0, 0)),
        pl.BlockSpec((nb, V7X_SUBLANES, rb), lambda j: (0, 0, 0)),
    )
    return pl.pallas_call(
        kern,
        grid=(n_tiles + 1,),
        in_specs=in_specs,
        out_specs=out_specs,
        out_shape=out_shape,
        scratch_shapes=[pltpu.VMEM((2, 2 * nb, tt * V7X_SUBLANES, rb), F32),
                        pltpu.VMEM((nb, HIST_ROWS, rb), F32), pltpu.VMEM((nb, V7X_SUBLANES, rb), F32)],
        compiler_params=pltpu.CompilerParams(dimension_semantics=("arbitrary",),
                                             vmem_limit_bytes=V7X_VMEM_LIMIT),
        name="rnn_branch_start" if seq_start else "rnn_branch",
    )(xt, p["ln_g"], p["ln_b"], p["wx"], p["wg"], p["bx"], p["bg"], p["cw"], p["cb"],
      p["wra"], p["bra"], p["wri"], p["bri"], p["lam"], p["w_rnn_out"], hist0, state0)


def _rope_rows(k, cos, sin_signed):
    lane = lax.broadcasted_iota(jnp.int32, k.shape, 1)
    partner = jnp.where((lane % HEAD_DIM) < HALF,
                        pltpu.roll(k, V7X_LANES - HALF, axis=1),
                        pltpu.roll(k, HALF, axis=1))
    return k * cos + partner * sin_signed


def _meta_kv_kernel(m_ref, lng_ref, lnb_ref, wk_ref, bk_ref, wv_ref, bv_ref, cos_ref, sin_ref,
                    k_ref, v_ref):
    h = _layer_norm(m_ref[...], lng_ref[...], lnb_ref[...]).astype(BF16)
    k = _dot(h, wk_ref[...]) + bk_ref[...]
    v = _dot(h, wv_ref[...]) + bv_ref[...]
    for g in range(N_KV_HEADS):
        sl = slice(g * KV_SLOT, (g + 1) * KV_SLOT)
        k_ref[:, sl] = _rope_rows(k[:, sl], cos_ref[...], sin_ref[...]).astype(BF16)
    v_ref[...] = v.astype(BF16)


def _meta_kv(meta, p, cos_rows, sin_rows):
    return pl.pallas_call(
        _meta_kv_kernel,
        out_shape=(jax.ShapeDtypeStruct((N_META, N_KV_HEADS * KV_SLOT), BF16),
                   jax.ShapeDtypeStruct((N_META, D_KV), BF16)),
        name="meta_kv",
    )(meta, p["ln_g"], p["ln_b"], p["wk"], p["bk"], p["wv"], p["bv"], cos_rows, sin_rows)


def _attn_kernel(x_ref, lng_ref, lnb_ref, wqt_ref, bqt_ref, wk_ref, bk_ref, wvt_ref, bvt_ref,
                 wgat_ref, bgat_ref, wout_ref, cost_ref, sint_ref, cosr_ref, sinr_ref,
                 kmeta_ref, vmetat_ref, sink_ref, yb_ref,
                 qt_s, k_s, vt_s, gat_s, kprev_s, vprev_s, ot_s, *, tq, n_t):
    nblk = tq // BLOCK
    j = pl.program_id(0)
    w_slot = j % 2
    r_slot = 1 - w_slot

    @pl.when(j == 0)
    def _():
        for ref in (qt_s, k_s, vt_s, gat_s, kprev_s, vprev_s):
            ref[...] = jnp.zeros_like(ref)

    h = _layer_norm(x_ref[0], lng_ref[...], lnb_ref[...]).astype(BF16)

    def lanes_bias(b_ref):
        return jnp.concatenate([b_ref[...]] * (tq // V7X_LANES), axis=1)

    qt = _dot_nt(wqt_ref[...], h) + lanes_bias(bqt_ref)
    scale = HEAD_DIM ** -0.5
    cos = cost_ref[...] * scale
    sin = sint_ref[...] * scale
    for hh in range(N_Q_HEADS):
        r0 = hh * HEAD_DIM
        q1, q2 = qt[r0:r0 + HALF], qt[r0 + HALF:r0 + HEAD_DIM]
        qt_s[w_slot, r0:r0 + HALF, :] = q1 * cos - q2 * sin
        qt_s[w_slot, r0 + HALF:r0 + HEAD_DIM, :] = q2 * cos + q1 * sin

    k = _dot(h, wk_ref[...]) + bk_ref[...]
    for g in range(N_KV_HEADS):
        sl = slice(g * KV_SLOT, (g + 1) * KV_SLOT)
        k_s[w_slot, :, sl] = _rope_rows(k[:, sl], cosr_ref[...], sinr_ref[...])
    vt_s[w_slot] = _dot_nt(wvt_ref[...], h) + lanes_bias(bvt_ref)
    gat = _dot_nt(wgat_ref[...], h) + lanes_bias(bgat_ref)
    gat_s[w_slot] = gat * _sigmoid(gat)

    i = jnp.maximum(j - 1, 0) % n_t
    k = k_s[r_slot].astype(BF16)
    vt = vt_s[r_slot].astype(BF16)
    kmeta = kmeta_ref[...]
    vmetat = vmetat_ref[...]

    key_row = lax.broadcasted_iota(jnp.int32, (BLOCK, GROUP * BLOCK), 0)
    qry_col = lax.broadcasted_iota(jnp.int32, (BLOCK, GROUP * BLOCK), 1) % BLOCK
    cur_ok = key_row <= qry_col
    prev_in_window = key_row > qry_col

    for jb in range(nblk):
        tok = slice(jb * BLOCK, (jb + 1) * BLOCK)
        if jb == 0:
            k_prev, vt_prev = kprev_s[...].astype(BF16), vprev_s[...].astype(BF16)
            prev_ok = prev_in_window & (i > 0)
        else:
            ptok = slice((jb - 1) * BLOCK, jb * BLOCK)
            k_prev, vt_prev = k[ptok], vt[:, ptok]
            prev_ok = prev_in_window
        k_cur, vt_cur = k[tok], vt[:, tok]
        for g in range(N_KV_HEADS):
            ks = slice(g * KV_SLOT, g * KV_SLOT + HEAD_DIM)
            vs = slice(g * HEAD_DIM, (g + 1) * HEAD_DIM)
            q_g = jnp.concatenate(
                [qt_s[r_slot, (g * GROUP + hh) * HEAD_DIM:(g * GROUP + hh + 1) * HEAD_DIM, tok].astype(BF16)
                 for hh in range(GROUP)], axis=1)
            s_prev = jnp.where(prev_ok, _dot(k_prev[:, ks], q_g), NEG_INF)
            s_cur = jnp.where(cur_ok, _dot(k_cur[:, ks], q_g), NEG_INF)
            s_meta = _dot(kmeta[:, ks], q_g)
            sink = sink_ref[g]
            m = jnp.maximum(
                jnp.maximum(jnp.max(s_prev, axis=0, keepdims=True), jnp.max(s_cur, axis=0, keepdims=True)),
                jnp.maximum(jnp.max(s_meta, axis=0, keepdims=True), sink))
            p_prev = jnp.exp(s_prev - m)
            p_cur = jnp.exp(s_cur - m)
            p_meta = jnp.exp(s_meta - m)
            denom = (jnp.sum(p_prev, axis=0, keepdims=True) + jnp.sum(p_cur, axis=0, keepdims=True)
                     + jnp.sum(p_meta, axis=0, keepdims=True) + jnp.exp(sink - m))
            o = (_dot(vt_prev[vs], p_prev.astype(BF16)) + _dot(vt_cur[vs], p_cur.astype(BF16))
                 + _dot(vmetat[vs], p_meta.astype(BF16)))
            o = o / denom
            for hh in range(GROUP):
                r0 = (g * GROUP + hh) * HEAD_DIM
                ot_s[r0:r0 + HEAD_DIM, tok] = o[:, hh * BLOCK:(hh + 1) * BLOCK]

    kprev_s[...] = k_s[r_slot, tq - BLOCK:, :]
    vprev_s[...] = vt_s[r_slot, :, tq - BLOCK:]

    act = (ot_s[...] * gat_s[r_slot]).astype(BF16)
    yb = lax.dot_general(act, wout_ref[...], (((0,), (0,)), ((), ())), preferred_element_type=F32)
    yb_ref[0] = yb.astype(yb_ref.dtype)


def _attn_branch(x, p, tabs, kmeta, vmetat, *, tq):
    b, t_len, _ = x.shape
    n_t = t_len // tq
    n_tiles = b * n_t
    kern = functools.partial(_attn_kernel, tq=tq, n_t=n_t)
    proj_tile = lambda j: jnp.minimum(j, n_tiles - 1)
    attn_tile = lambda j: jnp.maximum(j - 1, 0)
    in_specs = [
        pl.BlockSpec((1, tq, D_MODEL), lambda j: (proj_tile(j) // n_t, proj_tile(j) % n_t, 0)),
        _const_spec((1, D_MODEL)), _const_spec((1, D_MODEL)),
        _const_spec((D_MODEL, D_MODEL)), _const_spec((D_MODEL, V7X_LANES)),
        _const_spec((D_MODEL, N_KV_HEADS * KV_SLOT)), _const_spec((1, N_KV_HEADS * KV_SLOT)),
        _const_spec((D_KV, D_MODEL)), _const_spec((D_KV, V7X_LANES)),
        _const_spec((D_MODEL, D_MODEL)), _const_spec((D_MODEL, V7X_LANES)),
        _const_spec((D_MODEL, D_MODEL)),
        pl.BlockSpec((HALF, tq), lambda j: (0, proj_tile(j) % n_t)),
        pl.BlockSpec((HALF, tq), lambda j: (0, proj_tile(j) % n_t)),
        pl.BlockSpec((tq, KV_SLOT), lambda j: (proj_tile(j) % n_t, 0)),
        pl.BlockSpec((tq, KV_SLOT), lambda j: (proj_tile(j) % n_t, 0)),
        _const_spec((N_META, N_KV_HEADS * KV_SLOT)), _const_spec((D_KV, N_META)),
        _const_spec((N_KV_HEADS, 1, GROUP * BLOCK)),
    ]
    return pl.pallas_call(
        kern,
        grid=(n_tiles + 1,),
        in_specs=in_specs,
        out_specs=pl.BlockSpec((1, tq, D_MODEL), lambda j: (attn_tile(j) // n_t, attn_tile(j) % n_t, 0)),
        out_shape=jax.ShapeDtypeStruct((b, t_len, D_MODEL), BF16),
        scratch_shapes=[pltpu.VMEM((2, D_MODEL, tq), F32),
                        pltpu.VMEM((2, tq, N_KV_HEADS * KV_SLOT), F32),
                        pltpu.VMEM((2, D_KV, tq), F32),
                        pltpu.VMEM((2, D_MODEL, tq), F32),
                        pltpu.VMEM((BLOCK, N_KV_HEADS * KV_SLOT), F32),
                        pltpu.VMEM((D_KV, BLOCK), F32),
                        pltpu.VMEM((D_MODEL, tq), F32)],
        compiler_params=pltpu.CompilerParams(dimension_semantics=("arbitrary",),
                                             vmem_limit_bytes=V7X_VMEM_LIMIT),
        name="attn_branch",
    )(x, p["ln_g"], p["ln_b"], p["wqt"], p["bqt"], p["wk"], p["bk"], p["wvt"], p["bvt"],
      p["wgat"], p["bgat"], p["w_attn_out"], tabs["cos_t"], tabs["sin_t"], tabs["cos_r"], tabs["sin_r"],
      kmeta, vmetat, p["sink_rows"])


def _merge_kernel(x_ref, ya_ref, yb_ref, lng_ref, lnb_ref, wmg_ref, bmg_ref, wo_ref, bo_ref,
                  g2_ref, b2_ref, out_ref, h32_s, mixed_s):
    j = pl.program_id(0)
    w_slot = j % 2
    r_slot = 1 - w_slot

    @pl.when(j == 0)
    def _():
        h32_s[...] = jnp.zeros_like(h32_s)
        mixed_s[...] = jnp.zeros_like(mixed_s)

    out = _dot(mixed_s[r_slot].astype(BF16), wo_ref[...]) + bo_ref[...]

    h32 = _layer_norm(x_ref[0], lng_ref[...], lnb_ref[...])
    h = h32.astype(BF16)
    ga = _dot(h, wmg_ref[:, :D_MODEL]) + bmg_ref[:, :D_MODEL]
    gb = _dot(h, wmg_ref[:, D_MODEL:]) + bmg_ref[:, D_MODEL:]

    out_ref[0] = _layer_norm(DEEPNORM_ALPHA * h32_s[r_slot] + out, g2_ref[...], b2_ref[...])

    h32_s[w_slot] = h32
    mixed_s[w_slot] = _sigmoid(ga) * ya_ref[0].astype(F32) + _sigmoid(gb) * yb_ref[0].astype(F32)


def _merge(x, ya, yb, p, *, tm):
    b, t_len, _ = x.shape
    n_t = t_len // tm
    n_tiles = b * n_t
    tile_of = lambda t: (t // n_t, t % n_t, 0)
    tile = pl.BlockSpec((1, tm, D_MODEL), lambda j: tile_of(jnp.minimum(j, n_tiles - 1)))
    in_specs = [tile, tile, tile,
                _const_spec((1, D_MODEL)), _const_spec((1, D_MODEL)),
                _const_spec((D_MODEL, 2 * D_MODEL)), _const_spec((1, 2 * D_MODEL)),
                _const_spec((D_MODEL, D_MODEL)), _const_spec((1, D_MODEL)),
                _const_spec((1, D_MODEL)), _const_spec((1, D_MODEL))]
    return pl.pallas_call(
        _merge_kernel,
        grid=(n_tiles + 1,),
        in_specs=in_specs,
        out_specs=pl.BlockSpec((1, tm, D_MODEL), lambda j: tile_of(jnp.maximum(j - 1, 0))),
        out_shape=jax.ShapeDtypeStruct((b, t_len, D_MODEL), F32),
        scratch_shapes=[pltpu.VMEM((2, tm, D_MODEL), F32), pltpu.VMEM((2, tm, D_MODEL), F32)],
        compiler_params=pltpu.CompilerParams(dimension_semantics=("arbitrary",),
                                             vmem_limit_bytes=V7X_VMEM_LIMIT),
        name="merge",
    )(x, ya, yb, p["ln_g"], p["ln_b"], p["wmg"], p["bmg"], p["w_o"], p["b_o"], p["ln2_g"], p["ln2_b"])


def _rope_tables(t_total):
    inv = ROPE_THETA ** (-jnp.arange(HALF, dtype=F32) / HALF)
    ang = jnp.arange(t_total, dtype=F32)[:, None] * inv[None, :]
    cos, sin = jnp.cos(ang), jnp.sin(ang)
    zeros = jnp.zeros_like(cos)
    cos_r = jnp.concatenate([cos, cos, zeros, zeros], axis=1)
    sin_r = jnp.concatenate([-sin, sin, zeros, zeros], axis=1)
    return cos, sin, cos_r, sin_r


def _slotted(w):
    lead = w.shape[:-1]
    w4 = w.reshape(lead + (N_KV_HEADS, HEAD_DIM))
    w4 = jnp.concatenate([w4, jnp.zeros_like(w4)], axis=-1)
    return w4.reshape(lead + (N_KV_HEADS * KV_SLOT,))


def kernel(x, meta_tokens, ln_emb_g, ln_emb_b, w_in, b_in, conv_w, conv_b, w_ra, b_ra, w_ri, b_ri,
           lru_lambda, sinks, w_rnn_out, w_attn_out, w_o, b_o, ln_g, ln_b):
    b, seq, _ = x.shape
    assert b == V7X_SUBLANES and w_in.shape[0] == DEPTH
    nb, rb = N_RNN_BLOCKS, RNN_BLOCK
    w = w_in[0]
    bi = b_in[0]
    row = lambda v: v.reshape(1, -1)
    blocks = lambda v: v.reshape(nb, 1, rb)
    lanes = lambda v: jnp.broadcast_to(v[:, None], (v.shape[0], V7X_LANES))
    col_blocks = lambda m: m.reshape(D_MODEL, nb, rb).transpose(1, 0, 2).astype(BF16)

    common = {"ln_g": row(ln_emb_g), "ln_b": row(ln_emb_b)}
    p_rnn = dict(common,
                 wx=col_blocks(w[:, :OFF_GR]), wg=col_blocks(w[:, OFF_GR:OFF_Q]),
                 bx=blocks(bi[:OFF_GR]), bg=blocks(bi[OFF_GR:OFF_Q]),
                 cw=conv_w[0].reshape(CONV_WIDTH, nb, rb).transpose(1, 0, 2), cb=blocks(conv_b[0]),
                 wra=w_ra[0].astype(BF16), bra=blocks(b_ra[0]),
                 wri=w_ri[0].astype(BF16), bri=blocks(b_ri[0]),
                 lam=blocks(lru_lambda[0]), w_rnn_out=w_rnn_out[0].astype(BF16))
    p_attn = dict(common,
                  wqt=w[:, OFF_Q:OFF_K].T.astype(BF16), bqt=lanes(bi[OFF_Q:OFF_K]),
                  wk=_slotted(w[:, OFF_K:OFF_V]).astype(BF16), bk=row(_slotted(bi[OFF_K:OFF_V])),
                  wvt=w[:, OFF_V:OFF_GA].T.astype(BF16), bvt=lanes(bi[OFF_V:OFF_GA]),
                  wgat=w[:, OFF_GA:OFF_G].T.astype(BF16), bgat=lanes(bi[OFF_GA:OFF_G]),
                  wv=w[:, OFF_V:OFF_GA].astype(BF16), bv=row(bi[OFF_V:OFF_GA]),
                  w_attn_out=w_attn_out[0].astype(BF16),
                  sink_rows=jnp.repeat(sinks[0].astype(F32), BLOCK).reshape(N_KV_HEADS, 1, GROUP * BLOCK))
    p_merge = dict(common, wmg=w[:, OFF_G:].astype(BF16), bmg=row(bi[OFF_G:]),
                   w_o=w_o[0].astype(BF16), b_o=row(b_o[0]), ln2_g=row(ln_g[0]), ln2_b=row(ln_b[0]))

    cos, sin, cos_r, sin_r = _rope_tables(N_META + seq)
    tabs = {"cos_t": cos[N_META:].T, "sin_t": sin[N_META:].T, "cos_r": cos_r[N_META:], "sin_r": sin_r[N_META:]}

    meta = meta_tokens.astype(x.dtype)
    meta_t = jnp.broadcast_to(meta[:, None, :], (N_META, V7X_SUBLANES, D_MODEL))
    zero_hist = jnp.zeros((nb, HIST_ROWS, rb), F32)
    zero_state = jnp.zeros((nb, V7X_SUBLANES, rb), F32)
    _, hist0, state0 = _rnn_branch(meta_t, p_rnn, zero_hist, zero_state, tt=N_META, seq_start=True)
    kmeta, vmeta = _meta_kv(meta, p_attn, cos_r[:N_META], sin_r[:N_META])

    xt = jnp.transpose(x, (1, 0, 2))
    ya_t, _, _ = _rnn_branch(xt, p_rnn, hist0, state0, tt=32, seq_start=False)
    ya = jnp.transpose(ya_t, (1, 0, 2))
    yb = _attn_branch(x, p_attn, tabs, kmeta, vmeta.T, tq=256)
    return _merge(x, ya, yb, p_merge, tm=256)
```

```python
import functools

import jax
import jax.numpy as jnp
from jax import lax
from jax.experimental import pallas as pl
from jax.experimental.pallas import tpu as pltpu

D_MODEL = 2048
N_META = 16
N_RNN_BLOCKS = 8
RNN_BLOCK = 256
CONV_WIDTH = 4
LRU_C = 8.0
HEAD_DIM = 64
HALF = HEAD_DIM // 2
N_Q_HEADS = 32
N_KV_HEADS = 4
GROUP = 8
D_KV = N_KV_HEADS * HEAD_DIM
BLOCK = 128
ROPE_THETA = 10000.0
NEG_INF = -1e30
LN_EPS = 1e-5
DEPTH = 1
DEEPNORM_ALPHA = (2.0 * DEPTH) ** 0.25
OFF_GR = D_MODEL
OFF_Q = 2 * D_MODEL
OFF_K = OFF_Q + D_MODEL
OFF_V = OFF_K + D_KV
OFF_GA = OFF_V + D_KV
OFF_G = OFF_GA + D_MODEL

V7X_SUBLANES = 8
V7X_LANES = 128
V7X_VMEM_LIMIT = 56 * 1024 * 1024

KV_SLOT = V7X_LANES
HIST_ROWS = (CONV_WIDTH - 1) * V7X_SUBLANES

BF16 = jnp.bfloat16
F32 = jnp.float32


def _layer_norm(x, g, b):
    mu = jnp.mean(x, axis=-1, keepdims=True)
    xc = x - mu
    var = jnp.mean(xc * xc, axis=-1, keepdims=True)
    return xc * lax.rsqrt(var + LN_EPS) * g + b


def _sigmoid(x):
    return 1.0 / (1.0 + jnp.exp(-x))


def _dot(a, b):
    return jnp.dot(a, b, preferred_element_type=F32)


def _dot_nt(a, b):
    return lax.dot_general(a, b, (((1,), (1,)), ((), ())), preferred_element_type=F32)


def _const_spec(shape):
    nd = len(shape)
    return pl.BlockSpec(shape, lambda *_: (0,) * nd, pipeline_mode=pl.Buffered(1))


def _rnn_kernel(x_ref, lng_ref, lnb_ref, wx_ref, wg_ref, bx_ref, bg_ref, cw_ref, cb_ref,
                wra_ref, bra_ref, wri_ref, bri_ref, lam_ref, wout_ref, perm_ref, permt_ref,
                hist0_ref, state0_ref, ya_ref, hist_out_ref, state_out_ref, y_s, hist_s, state_s,
                *, tt, seq_start):
    rows = tt * V7X_SUBLANES
    nb = N_RNN_BLOCKS
    j = pl.program_id(0)

    @pl.when(j == 0)
    def _():
        y_s[...] = jnp.zeros_like(y_s)
        hist_s[...] = hist0_ref[...]
        state_s[...] = state0_ref[...]

    x = x_ref[...].reshape(rows, D_MODEL)
    h = _layer_norm(x, lng_ref[...], lnb_ref[...]).astype(BF16)
    ya = _dot(y_s[...].astype(BF16), wout_ref[...])
    ya_ref[...] = ya.reshape(V7X_SUBLANES, tt, D_MODEL).astype(ya_ref.dtype)
    h = _dot(perm_ref[...], h).astype(BF16)

    def project(n):
        return _dot(h, wx_ref[n]) + bx_ref[n], _dot(h, wg_ref[n]) + bg_ref[n]

    ys = []
    nxt = project(0)
    for n in range(nb):
        xr, gr = nxt
        if n + 1 < nb:
            nxt = project(n + 1)
        xe = jnp.concatenate([hist_s[n], xr], axis=0)
        cw = cw_ref[n]
        conv = cb_ref[n] + cw[0:1] * xr
        for k in range(1, CONV_WIDTH):
            off = HIST_ROWS - k * V7X_SUBLANES
            conv = conv + cw[k:k + 1] * xe[off:off + rows]
        hist_s[n] = xe[rows:rows + HIST_ROWS]
        cb16 = conv.astype(BF16)
        gate_r = _sigmoid(_dot(cb16, wra_ref[n]) + bra_ref[n])
        gate_i = _sigmoid(_dot(cb16, wri_ref[n]) + bri_ref[n])
        lam = lam_ref[n]
        log_sig = jnp.minimum(lam, 0.0) - jnp.log1p(jnp.exp(-jnp.abs(lam)))
        a = jnp.exp((LRU_C * gate_r) * log_sig)
        mult = jnp.sqrt(1.0 - a * a)
        if seq_start:
            first = lax.broadcasted_iota(jnp.int32, (rows, RNN_BLOCK), 0) < V7X_SUBLANES
            mult = jnp.where(first & (j == 0), 1.0, mult)
        u = mult * gate_i * conv
        hcur = state_s[n]
        hs = []
        for t in range(tt):
            sl = slice(t * V7X_SUBLANES, (t + 1) * V7X_SUBLANES)
            hcur = a[sl] * hcur + u[sl]
            hs.append(hcur)
        state_s[n] = hcur
        hr = jnp.concatenate(hs, axis=0)
        ys.append((hr * (gr * _sigmoid(gr))).astype(BF16))
    y = jnp.concatenate(ys, axis=1)
    y_s[...] = _dot(permt_ref[...], y)

    @pl.when(j == pl.num_programs(0) - 2)
    def _():
        hist_out_ref[...] = hist_s[...]
        state_out_ref[...] = state_s[...]


def _rnn_branch(x, p, hist0, state0, *, tt, seq_start):
    t_len = x.shape[1]
    nb, rb = N_RNN_BLOCKS, RNN_BLOCK
    rows = tt * V7X_SUBLANES
    r = jnp.arange(rows)
    perm = jax.nn.one_hot((r % V7X_SUBLANES) * tt + r // V7X_SUBLANES, rows, dtype=BF16)
    kern = functools.partial(_rnn_kernel, tt=tt, seq_start=seq_start)
    n_tiles = t_len // tt
    in_specs = [
        pl.BlockSpec((V7X_SUBLANES, tt, D_MODEL), lambda j: (0, jnp.minimum(j, n_tiles - 1), 0)),
        _const_spec((1, D_MODEL)), _const_spec((1, D_MODEL)),
        _const_spec((nb, D_MODEL, rb)), _const_spec((nb, D_MODEL, rb)),
        _const_spec((nb, 1, rb)), _const_spec((nb, 1, rb)),
        _const_spec((nb, CONV_WIDTH, rb)), _const_spec((nb, 1, rb)),
        _const_spec((nb, rb, rb)), _const_spec((nb, 1, rb)),
        _const_spec((nb, rb, rb)), _const_spec((nb, 1, rb)),
        _const_spec((nb, 1, rb)),
        _const_spec((D_MODEL, D_MODEL)),
        _const_spec((rows, rows)), _const_spec((rows, rows)),
        _const_spec((nb, HIST_ROWS, rb)), _const_spec((nb, V7X_SUBLANES, rb)),
    ]
    out_shape = (
        jax.ShapeDtypeStruct((V7X_SUBLANES, t_len, D_MODEL), BF16),
        jax.ShapeDtypeStruct((nb, HIST_ROWS, rb), F32),
        jax.ShapeDtypeStruct((nb, V7X_SUBLANES, rb), F32),
    )
    out_specs = (
        pl.BlockSpec((V7X_SUBLANES, tt, D_MODEL), lambda j: (0, jnp.maximum(j - 1, 0), 0)),
        pl.BlockSpec((nb, HIST_ROWS, rb), lambda j: (0, 0, 0)),
        pl.BlockSpec((nb, V7X_SUBLANES, rb), lambda j: (0, 0, 0)),
    )
    return pl.pallas_call(
        kern,
        grid=(n_tiles + 1,),
        in_specs=in_specs,
        out_specs=out_specs,
        out_shape=out_shape,
        scratch_shapes=[pltpu.VMEM((rows, D_MODEL), F32),
                        pltpu.VMEM((nb, HIST_ROWS, rb), F32), pltpu.VMEM((nb, V7X_SUBLANES, rb), F32)],
        compiler_params=pltpu.CompilerParams(dimension_semantics=("arbitrary",),
                                             vmem_limit_bytes=V7X_VMEM_LIMIT),
        name="rnn_branch_start" if seq_start else "rnn_branch",
    )(x, p["ln_g"], p["ln_b"], p["wx"], p["wg"], p["bx"], p["bg"], p["cw"], p["cb"],
      p["wra"], p["bra"], p["wri"], p["bri"], p["lam"], p["w_rnn_out"], perm, perm.T, hist0, state0)


def _rope_rows(k, cos, sin_signed):
    lane = lax.broadcasted_iota(jnp.int32, k.shape, 1)
    partner = jnp.where((lane % HEAD_DIM) < HALF,
                        pltpu.roll(k, V7X_LANES - HALF, axis=1),
                        pltpu.roll(k, HALF, axis=1))
    return k * cos + partner * sin_signed


def _meta_kv_kernel(m_ref, lng_ref, lnb_ref, wk_ref, bk_ref, wv_ref, bv_ref, cos_ref, sin_ref,
                    k_ref, v_ref):
    h = _layer_norm(m_ref[...], lng_ref[...], lnb_ref[...]).astype(BF16)
    k = _dot(h, wk_ref[...]) + bk_ref[...]
    v = _dot(h, wv_ref[...]) + bv_ref[...]
    for g in range(N_KV_HEADS):
        sl = slice(g * KV_SLOT, (g + 1) * KV_SLOT)
        k_ref[:, sl] = _rope_rows(k[:, sl], cos_ref[...], sin_ref[...]).astype(BF16)
    v_ref[...] = v.astype(BF16)


def _meta_kv(meta, p, cos_rows, sin_rows):
    return pl.pallas_call(
        _meta_kv_kernel,
        out_shape=(jax.ShapeDtypeStruct((N_META, N_KV_HEADS * KV_SLOT), BF16),
                   jax.ShapeDtypeStruct((N_META, D_KV), BF16)),
        name="meta_kv",
    )(meta, p["ln_g"], p["ln_b"], p["wk"], p["bk"], p["wv"], p["bv"], cos_rows, sin_rows)


def _attn_kernel(x_ref, lng_ref, lnb_ref, wqt_ref, bqt_ref, wk_ref, bk_ref, wvt_ref, bvt_ref,
                 wgat_ref, bgat_ref, wout_ref, cost_ref, sint_ref, cosr_ref, sinr_ref,
                 kmeta_ref, vmetat_ref, sink_ref, yb_ref,
                 qt_s, k_s, vt_s, gat_s, kprev_s, vprev_s, ot_s, *, tq, n_t):
    nblk = tq // BLOCK
    j = pl.program_id(0)
    w_slot = j % 2
    r_slot = 1 - w_slot

    @pl.when(j == 0)
    def _():
        for ref in (qt_s, k_s, vt_s, gat_s, kprev_s, vprev_s):
            ref[...] = jnp.zeros_like(ref)

    h = _layer_norm(x_ref[0], lng_ref[...], lnb_ref[...]).astype(BF16)

    def lanes_bias(b_ref):
        return jnp.concatenate([b_ref[...]] * (tq // V7X_LANES), axis=1)

    qt = _dot_nt(wqt_ref[...], h) + lanes_bias(bqt_ref)
    scale = HEAD_DIM ** -0.5
    cos = cost_ref[...] * scale
    sin = sint_ref[...] * scale
    for hh in range(N_Q_HEADS):
        r0 = hh * HEAD_DIM
        q1, q2 = qt[r0:r0 + HALF], qt[r0 + HALF:r0 + HEAD_DIM]
        qt_s[w_slot, r0:r0 + HALF, :] = q1 * cos - q2 * sin
        qt_s[w_slot, r0 + HALF:r0 + HEAD_DIM, :] = q2 * cos + q1 * sin

    k = _dot(h, wk_ref[...]) + bk_ref[...]
    for g in range(N_KV_HEADS):
        sl = slice(g * KV_SLOT, (g + 1) * KV_SLOT)
        k_s[w_slot, :, sl] = _rope_rows(k[:, sl], cosr_ref[...], sinr_ref[...])
    vt_s[w_slot] = _dot_nt(wvt_ref[...], h) + lanes_bias(bvt_ref)
    gat = _dot_nt(wgat_ref[...], h) + lanes_bias(bgat_ref)
    gat_s[w_slot] = gat * _sigmoid(gat)

    i = jnp.maximum(j - 1, 0) % n_t
    k = k_s[r_slot].astype(BF16)
    vt = vt_s[r_slot].astype(BF16)
    kmeta = kmeta_ref[...]
    vmetat = vmetat_ref[...]

    key_row = lax.broadcasted_iota(jnp.int32, (BLOCK, GROUP * BLOCK), 0)
    qry_col = lax.broadcasted_iota(jnp.int32, (BLOCK, GROUP * BLOCK), 1) % BLOCK
    cur_ok = key_row <= qry_col
    prev_in_window = key_row > qry_col

    for jb in range(nblk):
        tok = slice(jb * BLOCK, (jb + 1) * BLOCK)
        if jb == 0:
            k_prev, vt_prev = kprev_s[...].astype(BF16), vprev_s[...].astype(BF16)
            prev_ok = prev_in_window & (i > 0)
        else:
            ptok = slice((jb - 1) * BLOCK, jb * BLOCK)
            k_prev, vt_prev = k[ptok], vt[:, ptok]
            prev_ok = prev_in_window
        k_cur, vt_cur = k[tok], vt[:, tok]
        for g in range(N_KV_HEADS):
            ks = slice(g * KV_SLOT, g * KV_SLOT + HEAD_DIM)
            vs = slice(g * HEAD_DIM, (g + 1) * HEAD_DIM)
            q_g = jnp.concatenate(
                [qt_s[r_slot, (g * GROUP + hh) * HEAD_DIM:(g * GROUP + hh + 1) * HEAD_DIM, tok].astype(BF16)
                 for hh in range(GROUP)], axis=1)
            s_prev = jnp.where(prev_ok, _dot(k_prev[:, ks], q_g), NEG_INF)
            s_cur = jnp.where(cur_ok, _dot(k_cur[:, ks], q_g), NEG_INF)
            s_meta = _dot(kmeta[:, ks], q_g)
            sink = sink_ref[g]
            m = jnp.maximum(
                jnp.maximum(jnp.max(s_prev, axis=0, keepdims=True), jnp.max(s_cur, axis=0, keepdims=True)),
                jnp.maximum(jnp.max(s_meta, axis=0, keepdims=True), sink))
            p_prev = jnp.exp(s_prev - m)
            p_cur = jnp.exp(s_cur - m)
            p_meta = jnp.exp(s_meta - m)
            denom = (jnp.sum(p_prev, axis=0, keepdims=True) + jnp.sum(p_cur, axis=0, keepdims=True)
                     + jnp.sum(p_meta, axis=0, keepdims=True) + jnp.exp(sink - m))
            o = (_dot(vt_prev[vs], p_prev.astype(BF16)) + _dot(vt_cur[vs], p_cur.astype(BF16))
                 + _dot(vmetat[vs], p_meta.astype(BF16)))
            o = o / denom
            for hh in range(GROUP):
                r0 = (g * GROUP + hh) * HEAD_DIM
                ot_s[r0:r0 + HEAD_DIM, tok] = o[:, hh * BLOCK:(hh + 1) * BLOCK]

    kprev_s[...] = k_s[r_slot, tq - BLOCK:, :]
    vprev_s[...] = vt_s[r_slot, :, tq - BLOCK:]

    act = (ot_s[...] * gat_s[r_slot]).astype(BF16)
    yb = lax.dot_general(act, wout_ref[...], (((0,), (0,)), ((), ())), preferred_element_type=F32)
    yb_ref[0] = yb.astype(yb_ref.dtype)


def _attn_branch(x, p, tabs, kmeta, vmetat, *, tq):
    b, t_len, _ = x.shape
    n_t = t_len // tq
    n_tiles = b * n_t
    kern = functools.partial(_attn_kernel, tq=tq, n_t=n_t)
    proj_tile = lambda j: jnp.minimum(j, n_tiles - 1)
    attn_tile = lambda j: jnp.maximum(j - 1, 0)
    in_specs = [
        pl.BlockSpec((1, tq, D_MODEL), lambda j: (proj_tile(j) // n_t, proj_tile(j) % n_t, 0)),
        _const_spec((1, D_MODEL)), _const_spec((1, D_MODEL)),
        _const_spec((D_MODEL, D_MODEL)), _const_spec((D_MODEL, V7X_LANES)),
        _const_spec((D_MODEL, N_KV_HEADS * KV_SLOT)), _const_spec((1, N_KV_HEADS * KV_SLOT)),
        _const_spec((D_KV, D_MODEL)), _const_spec((D_KV, V7X_LANES)),
        _const_spec((D_MODEL, D_MODEL)), _const_spec((D_MODEL, V7X_LANES)),
        _const_spec((D_MODEL, D_MODEL)),
        pl.BlockSpec((HALF, tq), lambda j: (0, proj_tile(j) % n_t)),
        pl.BlockSpec((HALF, tq), lambda j: (0, proj_tile(j) % n_t)),
        pl.BlockSpec((tq, KV_SLOT), lambda j: (proj_tile(j) % n_t, 0)),
        pl.BlockSpec((tq, KV_SLOT), lambda j: (proj_tile(j) % n_t, 0)),
        _const_spec((N_META, N_KV_HEADS * KV_SLOT)), _const_spec((D_KV, N_META)),
        _const_spec((N_KV_HEADS, 1, GROUP * BLOCK)),
    ]
    return pl.pallas_call(
        kern,
        grid=(n_tiles + 1,),
        in_specs=in_specs,
        out_specs=pl.BlockSpec((1, tq, D_MODEL), lambda j: (attn_tile(j) // n_t, attn_tile(j) % n_t, 0)),
        out_shape=jax.ShapeDtypeStruct((b, t_len, D_MODEL), BF16),
        scratch_shapes=[pltpu.VMEM((2, D_MODEL, tq), F32),
                        pltpu.VMEM((2, tq, N_KV_HEADS * KV_SLOT), F32),
                        pltpu.VMEM((2, D_KV, tq), F32),
                        pltpu.VMEM((2, D_MODEL, tq), F32),
                        pltpu.VMEM((BLOCK, N_KV_HEADS * KV_SLOT), F32),
                        pltpu.VMEM((D_KV, BLOCK), F32),
                        pltpu.VMEM((D_MODEL, tq), F32)],
        compiler_params=pltpu.CompilerParams(dimension_semantics=("arbitrary",),
                                             vmem_limit_bytes=V7X_VMEM_LIMIT),
        name="attn_branch",
    )(x, p["ln_g"], p["ln_b"], p["wqt"], p["bqt"], p["wk"], p["bk"], p["wvt"], p["bvt"],
      p["wgat"], p["bgat"], p["w_attn_out"], tabs["cos_t"], tabs["sin_t"], tabs["cos_r"], tabs["sin_r"],
      kmeta, vmetat, p["sink_rows"])


def _merge_kernel(x_ref, ya_ref, yb_ref, lng_ref, lnb_ref, wmg_ref, bmg_ref, wo_ref, bo_ref,
                  g2_ref, b2_ref, out_ref, h32_s, mixed_s):
    j = pl.program_id(0)

    @pl.when(j == 0)
    def _():
        h32_s[...] = jnp.zeros_like(h32_s)
        mixed_s[...] = jnp.zeros_like(mixed_s)

    h32 = _layer_norm(x_ref[0], lng_ref[...], lnb_ref[...])
    h = h32.astype(BF16)
    out = _dot(mixed_s[...].astype(BF16), wo_ref[...]) + bo_ref[...]
    h32_prev = h32_s[...]
    ga = _dot(h, wmg_ref[:, :D_MODEL]) + bmg_ref[:, :D_MODEL]
    out_ref[0] = _layer_norm(DEEPNORM_ALPHA * h32_prev + out, g2_ref[...], b2_ref[...])
    gb = _dot(h, wmg_ref[:, D_MODEL:]) + bmg_ref[:, D_MODEL:]
    h32_s[...] = h32
    mixed_s[...] = _sigmoid(ga) * ya_ref[0].astype(F32) + _sigmoid(gb) * yb_ref[0].astype(F32)


def _merge(x, ya, yb, p, *, tm):
    b, t_len, _ = x.shape
    n_t = t_len // tm
    n_tiles = b * n_t
    tile_of = lambda t: (t // n_t, t % n_t, 0)
    tile = pl.BlockSpec((1, tm, D_MODEL), lambda j: tile_of(jnp.minimum(j, n_tiles - 1)))
    in_specs = [tile, tile, tile,
                _const_spec((1, D_MODEL)), _const_spec((1, D_MODEL)),
                _const_spec((D_MODEL, 2 * D_MODEL)), _const_spec((1, 2 * D_MODEL)),
                _const_spec((D_MODEL, D_MODEL)), _const_spec((1, D_MODEL)),
                _const_spec((1, D_MODEL)), _const_spec((1, D_MODEL))]
    return pl.pallas_call(
        _merge_kernel,
        grid=(n_tiles + 1,),
        in_specs=in_specs,
        out_specs=pl.BlockSpec((1, tm, D_MODEL), lambda j: tile_of(jnp.maximum(j - 1, 0))),
        out_shape=jax.ShapeDtypeStruct((b, t_len, D_MODEL), F32),
        scratch_shapes=[pltpu.VMEM((tm, D_MODEL), F32), pltpu.VMEM((tm, D_MODEL), F32)],
        compiler_params=pltpu.CompilerParams(dimension_semantics=("arbitrary",),
                                             vmem_limit_bytes=V7X_VMEM_LIMIT),
        name="merge",
    )(x, ya, yb, p["ln_g"], p["ln_b"], p["wmg"], p["bmg"], p["w_o"], p["b_o"], p["ln2_g"], p["ln2_b"])


def _rope_tables(t_total):
    inv = ROPE_THETA ** (-jnp.arange(HALF, dtype=F32) / HALF)
    ang = jnp.arange(t_total, dtype=F32)[:, None] * inv[None, :]
    cos, sin = jnp.cos(ang), jnp.sin(ang)
    zeros = jnp.zeros_like(cos)
    cos_r = jnp.concatenate([cos, cos, zeros, zeros], axis=1)
    sin_r = jnp.concatenate([-sin, sin, zeros, zeros], axis=1)
    return cos, sin, cos_r, sin_r


def _slotted(w):
    lead = w.shape[:-1]
    w4 = w.reshape(lead + (N_KV_HEADS, HEAD_DIM))
    w4 = jnp.concatenate([w4, jnp.zeros_like(w4)], axis=-1)
    return w4.reshape(lead + (N_KV_HEADS * KV_SLOT,))


def kernel(x, meta_tokens, ln_emb_g, ln_emb_b, w_in, b_in, conv_w, conv_b, w_ra, b_ra, w_ri, b_ri,
           lru_lambda, sinks, w_rnn_out, w_attn_out, w_o, b_o, ln_g, ln_b):
    b, seq, _ = x.shape
    assert b == V7X_SUBLANES and w_in.shape[0] == DEPTH
    nb, rb = N_RNN_BLOCKS, RNN_BLOCK
    w = w_in[0]
    bi = b_in[0]
    row = lambda v: v.reshape(1, -1)
    blocks = lambda v: v.reshape(nb, 1, rb)
    lanes = lambda v: jnp.broadcast_to(v[:, None], (v.shape[0], V7X_LANES))
    col_blocks = lambda m: m.reshape(D_MODEL, nb, rb).transpose(1, 0, 2).astype(BF16)

    common = {"ln_g": row(ln_emb_g), "ln_b": row(ln_emb_b)}
    p_rnn = dict(common,
                 wx=col_blocks(w[:, :OFF_GR]), wg=col_blocks(w[:, OFF_GR:OFF_Q]),
                 bx=blocks(bi[:OFF_GR]), bg=blocks(bi[OFF_GR:OFF_Q]),
                 cw=conv_w[0].reshape(CONV_WIDTH, nb, rb).transpose(1, 0, 2), cb=blocks(conv_b[0]),
                 wra=w_ra[0].astype(BF16), bra=blocks(b_ra[0]),
                 wri=w_ri[0].astype(BF16), bri=blocks(b_ri[0]),
                 lam=blocks(lru_lambda[0]), w_rnn_out=w_rnn_out[0].astype(BF16))
    p_attn = dict(common,
                  wqt=w[:, OFF_Q:OFF_K].T.astype(BF16), bqt=lanes(bi[OFF_Q:OFF_K]),
                  wk=_slotted(w[:, OFF_K:OFF_V]).astype(BF16), bk=row(_slotted(bi[OFF_K:OFF_V])),
                  wvt=w[:, OFF_V:OFF_GA].T.astype(BF16), bvt=lanes(bi[OFF_V:OFF_GA]),
                  wgat=w[:, OFF_GA:OFF_G].T.astype(BF16), bgat=lanes(bi[OFF_GA:OFF_G]),
                  wv=w[:, OFF_V:OFF_GA].astype(BF16), bv=row(bi[OFF_V:OFF_GA]),
                  w_attn_out=w_attn_out[0].astype(BF16),
                  sink_rows=jnp.repeat(sinks[0].astype(F32), BLOCK).reshape(N_KV_HEADS, 1, GROUP * BLOCK))
    p_merge = dict(common, wmg=w[:, OFF_G:].astype(BF16), bmg=row(bi[OFF_G:]),
                   w_o=w_o[0].astype(BF16), b_o=row(b_o[0]), ln2_g=row(ln_g[0]), ln2_b=row(ln_b[0]))

    cos, sin, cos_r, sin_r = _rope_tables(N_META + seq)
    tabs = {"cos_t": cos[N_META:].T, "sin_t": sin[N_META:].T, "cos_r": cos_r[N_META:], "sin_r": sin_r[N_META:]}

    meta = meta_tokens.astype(x.dtype)
    meta_b = jnp.broadcast_to(meta[None], (V7X_SUBLANES, N_META, D_MODEL))
    zero_hist = jnp.zeros((nb, HIST_ROWS, rb), F32)
    zero_state = jnp.zeros((nb, V7X_SUBLANES, rb), F32)
    _, hist0, state0 = _rnn_branch(meta_b, p_rnn, zero_hist, zero_state, tt=N_META, seq_start=True)
    kmeta, vmeta = _meta_kv(meta, p_attn, cos_r[:N_META], sin_r[:N_META])

    ya, _, _ = _rnn_branch(x, p_rnn, hist0, state0, tt=32, seq_start=False)
    yb = _attn_branch(x, p_attn, tabs, kmeta, vmeta.T, tq=256)
    return _merge(x, ya, yb, p_merge, tm=256)
```

```python
import functools

import jax
import jax.numpy as jnp
from jax import lax
from jax.experimental import pallas as pl
from jax.experimental.pallas import tpu as pltpu

D_MODEL = 2048
N_META = 16
N_RNN_BLOCKS = 8
RNN_BLOCK = 256
CONV_WIDTH = 4
LRU_C = 8.0
HEAD_DIM = 64
HALF = HEAD_DIM // 2
N_Q_HEADS = 32
N_KV_HEADS = 4
GROUP = 8
D_KV = N_KV_HEADS * HEAD_DIM
BLOCK = 128
ROPE_THETA = 10000.0
NEG_INF = -1e30
LN_EPS = 1e-5
DEPTH = 1
DEEPNORM_ALPHA = (2.0 * DEPTH) ** 0.25
OFF_GR = D_MODEL
OFF_Q = 2 * D_MODEL
OFF_K = OFF_Q + D_MODEL
OFF_V = OFF_K + D_KV
OFF_GA = OFF_V + D_KV
OFF_G = OFF_GA + D_MODEL

V7X_SUBLANES = 8
V7X_LANES = 128
V7X_VMEM_LIMIT = 56 * 1024 * 1024

KV_SLOT = V7X_LANES
HIST_ROWS = (CONV_WIDTH - 1) * V7X_SUBLANES

BF16 = jnp.bfloat16
F32 = jnp.float32


def _layer_norm(x, g, b):
    mu = jnp.mean(x, axis=-1, keepdims=True)
    xc = x - mu
    var = jnp.mean(xc * xc, axis=-1, keepdims=True)
    return xc * lax.rsqrt(var + LN_EPS) * g + b


def _sigmoid(x):
    return 1.0 / (1.0 + jnp.exp(-x))


def _dot(a, b):
    return jnp.dot(a, b, preferred_element_type=F32)


def _dot_nt(a, b):
    return lax.dot_general(a, b, (((1,), (1,)), ((), ())), preferred_element_type=F32)


def _const_spec(shape):
    nd = len(shape)
    return pl.BlockSpec(shape, lambda *_: (0,) * nd, pipeline_mode=pl.Buffered(1))


def _rnn_kernel(x_ref, lng_ref, lnb_ref, wx_ref, wg_ref, bx_ref, bg_ref, cw_ref, cb_ref,
                wra_ref, bra_ref, wri_ref, bri_ref, lam_ref, wout_ref, perm_ref, permt_ref,
                hist0_ref, state0_ref, ya_ref, hist_out_ref, state_out_ref, y_s, hist_s, state_s,
                *, tt, seq_start):
    rows = tt * V7X_SUBLANES
    nb = N_RNN_BLOCKS
    j = pl.program_id(0)

    @pl.when(j == 0)
    def _():
        y_s[...] = jnp.zeros_like(y_s)
        hist_s[...] = hist0_ref[...]
        state_s[...] = state0_ref[...]

    x = x_ref[...].reshape(rows, D_MODEL)
    h = _layer_norm(x, lng_ref[...], lnb_ref[...]).astype(BF16)
    ya = _dot(y_s[...].astype(BF16), wout_ref[...])
    ya_ref[...] = ya.reshape(V7X_SUBLANES, tt, D_MODEL).astype(ya_ref.dtype)
    h = _dot(perm_ref[...], h).astype(BF16)

    def project(n):
        return _dot(h, wx_ref[n]) + bx_ref[n], _dot(h, wg_ref[n]) + bg_ref[n]

    ys = []
    nxt = project(0)
    for n in range(nb):
        xr, gr = nxt
        if n + 1 < nb:
            nxt = project(n + 1)
        xe = jnp.concatenate([hist_s[n], xr], axis=0)
        cw = cw_ref[n]
        conv = cb_ref[n] + cw[0:1] * xr
        for k in range(1, CONV_WIDTH):
            off = HIST_ROWS - k * V7X_SUBLANES
            conv = conv + cw[k:k + 1] * xe[off:off + rows]
        hist_s[n] = xe[rows:rows + HIST_ROWS]
        cb16 = conv.astype(BF16)
        gate_r = _sigmoid(_dot(cb16, wra_ref[n]) + bra_ref[n])
        gate_i = _sigmoid(_dot(cb16, wri_ref[n]) + bri_ref[n])
        lam = lam_ref[n]
        log_sig = jnp.minimum(lam, 0.0) - jnp.log1p(jnp.exp(-jnp.abs(lam)))
        a = jnp.exp((LRU_C * gate_r) * log_sig)
        mult = jnp.sqrt(1.0 - a * a)
        if seq_start:
            first = lax.broadcasted_iota(jnp.int32, (rows, RNN_BLOCK), 0) < V7X_SUBLANES
            mult = jnp.where(first & (j == 0), 1.0, mult)
        u = mult * gate_i * conv
        hcur = state_s[n]
        hs = []
        for t in range(tt):
            sl = slice(t * V7X_SUBLANES, (t + 1) * V7X_SUBLANES)
            hcur = a[sl] * hcur + u[sl]
            hs.append(hcur)
        state_s[n] = hcur
        hr = jnp.concatenate(hs, axis=0)
        ys.append((hr * (gr * _sigmoid(gr))).astype(BF16))
    y = jnp.concatenate(ys, axis=1)
    y_s[...] = _dot(permt_ref[...], y)

    @pl.when(j == pl.num_programs(0) - 2)
    def _():
        hist_out_ref[...] = hist_s[...]
        state_out_ref[...] = state_s[...]


def _rnn_branch(x, p, hist0, state0, *, tt, seq_start):
    t_len = x.shape[1]
    nb, rb = N_RNN_BLOCKS, RNN_BLOCK
    rows = tt * V7X_SUBLANES
    r = jnp.arange(rows)
    perm = jax.nn.one_hot((r % V7X_SUBLANES) * tt + r // V7X_SUBLANES, rows, dtype=BF16)
    kern = functools.partial(_rnn_kernel, tt=tt, seq_start=seq_start)
    n_tiles = t_len // tt
    in_specs = [
        pl.BlockSpec((V7X_SUBLANES, tt, D_MODEL), lambda j: (0, jnp.minimum(j, n_tiles - 1), 0)),
        _const_spec((1, D_MODEL)), _const_spec((1, D_MODEL)),
        _const_spec((nb, D_MODEL, rb)), _const_spec((nb, D_MODEL, rb)),
        _const_spec((nb, 1, rb)), _const_spec((nb, 1, rb)),
        _const_spec((nb, CONV_WIDTH, rb)), _const_spec((nb, 1, rb)),
        _const_spec((nb, rb, rb)), _const_spec((nb, 1, rb)),
        _const_spec((nb, rb, rb)), _const_spec((nb, 1, rb)),
        _const_spec((nb, 1, rb)),
        _const_spec((D_MODEL, D_MODEL)),
        _const_spec((rows, rows)), _const_spec((rows, rows)),
        _const_spec((nb, HIST_ROWS, rb)), _const_spec((nb, V7X_SUBLANES, rb)),
    ]
    out_shape = (
        jax.ShapeDtypeStruct((V7X_SUBLANES, t_len, D_MODEL), BF16),
        jax.ShapeDtypeStruct((nb, HIST_ROWS, rb), F32),
        jax.ShapeDtypeStruct((nb, V7X_SUBLANES, rb), F32),
    )
    out_specs = (
        pl.BlockSpec((V7X_SUBLANES, tt, D_MODEL), lambda j: (0, jnp.maximum(j - 1, 0), 0)),
        pl.BlockSpec((nb, HIST_ROWS, rb), lambda j: (0, 0, 0)),
        pl.BlockSpec((nb, V7X_SUBLANES, rb), lambda j: (0, 0, 0)),
    )
    return pl.pallas_call(
        kern,
        grid=(n_tiles + 1,),
        in_specs=in_specs,
        out_specs=out_specs,
        out_shape=out_shape,
        scratch_shapes=[pltpu.VMEM((rows, D_MODEL), F32),
                        pltpu.VMEM((nb, HIST_ROWS, rb), F32), pltpu.VMEM((nb, V7X_SUBLANES, rb), F32)],
        compiler_params=pltpu.CompilerParams(dimension_semantics=("arbitrary",),
                                             vmem_limit_bytes=V7X_VMEM_LIMIT),
        name="rnn_branch_start" if seq_start else "rnn_branch",
    )(x, p["ln_g"], p["ln_b"], p["wx"], p["wg"], p["bx"], p["bg"], p["cw"], p["cb"],
      p["wra"], p["bra"], p["wri"], p["bri"], p["lam"], p["w_rnn_out"], perm, perm.T, hist0, state0)


def _rope_rows(k, cos, sin_signed):
    lane = lax.broadcasted_iota(jnp.int32, k.shape, 1)
    partner = jnp.where((lane % HEAD_DIM) < HALF,
                        pltpu.roll(k, V7X_LANES - HALF, axis=1),
                        pltpu.roll(k, HALF, axis=1))
    return k * cos + partner * sin_signed


def _meta_kv_kernel(m_ref, lng_ref, lnb_ref, wk_ref, bk_ref, wv_ref, bv_ref, cos_ref, sin_ref,
                    k_ref, v_ref):
    h = _layer_norm(m_ref[...], lng_ref[...], lnb_ref[...]).astype(BF16)
    k = _dot(h, wk_ref[...]) + bk_ref[...]
    v = _dot(h, wv_ref[...]) + bv_ref[...]
    for g in range(N_KV_HEADS):
        sl = slice(g * KV_SLOT, (g + 1) * KV_SLOT)
        k_ref[:, sl] = _rope_rows(k[:, sl], cos_ref[...], sin_ref[...]).astype(BF16)
    v_ref[...] = v.astype(BF16)


def _meta_kv(meta, p, cos_rows, sin_rows):
    return pl.pallas_call(
        _meta_kv_kernel,
        out_shape=(jax.ShapeDtypeStruct((N_META, N_KV_HEADS * KV_SLOT), BF16),
                   jax.ShapeDtypeStruct((N_META, D_KV), BF16)),
        name="meta_kv",
    )(meta, p["ln_g"], p["ln_b"], p["wk"], p["bk"], p["wv"], p["bv"], cos_rows, sin_rows)


def _attn_kernel(x_ref, lng_ref, lnb_ref, wqt_ref, bqt_ref, wk_ref, bk_ref, wvt_ref, bvt_ref,
                 wgat_ref, bgat_ref, wout_ref, cost_ref, sint_ref, cosr_ref, sinr_ref,
                 kmeta_ref, vmetat_ref, sink_ref, yb_ref,
                 qt_s, k_s, vt_s, gat_s, kprev_s, vprev_s, ot_s, act_s, *, tq, n_t):
    nblk = tq // BLOCK
    rows_g = GROUP * HEAD_DIM
    j = pl.program_id(0)

    @pl.when(j == 0)
    def _():
        for ref in (qt_s, k_s, vt_s, gat_s, kprev_s, vprev_s, act_s):
            ref[...] = jnp.zeros_like(ref)

    h = _layer_norm(x_ref[0], lng_ref[...], lnb_ref[...]).astype(BF16)

    act_prev = act_s[...].astype(BF16)
    out_cols = D_MODEL // (N_KV_HEADS * nblk)

    def project_out(c):
        cs = slice(c * out_cols, (c + 1) * out_cols)
        yb_ref[0, :, cs] = _dot(act_prev, wout_ref[:, cs]).astype(yb_ref.dtype)

    def lanes_bias(b):
        return jnp.concatenate([b] * (tq // V7X_LANES), axis=1)

    scale = HEAD_DIM ** -0.5
    cos = cost_ref[...] * scale
    sin = sint_ref[...] * scale

    def project_q(g):
        rs = slice(g * rows_g, (g + 1) * rows_g)
        qt = _dot_nt(wqt_ref[rs, :], h) + lanes_bias(bqt_ref[rs, :])
        out = []
        for hh in range(GROUP):
            r0 = hh * HEAD_DIM
            q1, q2 = qt[r0:r0 + HALF], qt[r0 + HALF:r0 + HEAD_DIM]
            out.append((q1 * cos - q2 * sin, q2 * cos + q1 * sin))
        return out

    def store_q(g, roped):
        for hh, (lo, hi) in enumerate(roped):
            r0 = g * rows_g + hh * HEAD_DIM
            qt_s[r0:r0 + HALF, :] = lo
            qt_s[r0 + HALF:r0 + HEAD_DIM, :] = hi

    def project_gate(g):
        rs = slice(g * rows_g, (g + 1) * rows_g)
        gat = _dot_nt(wgat_ref[rs, :], h) + lanes_bias(bgat_ref[rs, :])
        return gat * _sigmoid(gat)

    i = jnp.maximum(j - 1, 0) % n_t
    k = k_s[...].astype(BF16)
    vt = vt_s[...].astype(BF16)
    k_carry, vt_carry = kprev_s[...].astype(BF16), vprev_s[...].astype(BF16)
    kmeta = kmeta_ref[...]
    vmetat = vmetat_ref[...]

    key_row = lax.broadcasted_iota(jnp.int32, (BLOCK, GROUP * BLOCK), 0)
    qry_col = lax.broadcasted_iota(jnp.int32, (BLOCK, GROUP * BLOCK), 1) % BLOCK
    cur_ok = key_row <= qry_col
    prev_in_window = key_row > qry_col
    first_prev_ok = prev_in_window & (i > 0)

    def scores(g, jb):
        tok = slice(jb * BLOCK, (jb + 1) * BLOCK)
        ks = slice(g * KV_SLOT, g * KV_SLOT + HEAD_DIM)
        if jb == 0:
            k_prev, prev_ok = k_carry, first_prev_ok
        else:
            k_prev, prev_ok = k[(jb - 1) * BLOCK:jb * BLOCK], prev_in_window
        q_g = jnp.concatenate(
            [qt_s[(g * GROUP + hh) * HEAD_DIM:(g * GROUP + hh + 1) * HEAD_DIM, tok].astype(BF16)
             for hh in range(GROUP)], axis=1)
        s_prev = jnp.where(prev_ok, _dot(k_prev[:, ks], q_g), NEG_INF)
        s_cur = jnp.where(cur_ok, _dot(k[tok, ks], q_g), NEG_INF)
        s_meta = _dot(kmeta[:, ks], q_g)
        return s_prev, s_cur, s_meta

    def finish(g, jb, s):
        s_prev, s_cur, s_meta = s
        tok = slice(jb * BLOCK, (jb + 1) * BLOCK)
        vs = slice(g * HEAD_DIM, (g + 1) * HEAD_DIM)
        vt_prev = vt_carry if jb == 0 else vt[:, (jb - 1) * BLOCK:jb * BLOCK]
        sink = sink_ref[g]
        m = jnp.maximum(
            jnp.maximum(jnp.max(s_prev, axis=0, keepdims=True), jnp.max(s_cur, axis=0, keepdims=True)),
            jnp.maximum(jnp.max(s_meta, axis=0, keepdims=True), sink))
        p_prev = jnp.exp(s_prev - m)
        p_cur = jnp.exp(s_cur - m)
        p_meta = jnp.exp(s_meta - m)
        denom = (jnp.sum(p_prev, axis=0, keepdims=True) + jnp.sum(p_cur, axis=0, keepdims=True)
                 + jnp.sum(p_meta, axis=0, keepdims=True) + jnp.exp(sink - m))
        o = (_dot(vt_prev[vs], p_prev.astype(BF16)) + _dot(vt[vs, tok], p_cur.astype(BF16))
             + _dot(vmetat[vs], p_meta.astype(BF16)))
        o = o / denom
        for hh in range(GROUP):
            r0 = (g * GROUP + hh) * HEAD_DIM
            ot_s[r0:r0 + HEAD_DIM, tok] = o[:, hh * BLOCK:(hh + 1) * BLOCK]

    gates = [None] * N_KV_HEADS
    held_q = None
    for g in range(N_KV_HEADS):
        pg = (g - 1) % N_KV_HEADS
        for jb in range(nblk):
            s = scores(g, jb)
            if jb == 0:
                roped = project_q(pg)
                if pg < g:
                    store_q(pg, roped)
                else:
                    held_q = roped
            elif jb == 1:
                gates[pg] = project_gate(pg)
            project_out(g * nblk + jb)
            finish(g, jb, s)
    store_q(N_KV_HEADS - 1, held_q)

    k_new = _dot(h, wk_ref[...]) + bk_ref[...]
    vt_new = _dot_nt(wvt_ref[...], h) + lanes_bias(bvt_ref[...])

    act_s[...] = (ot_s[...] * gat_s[...]).T

    kprev_s[...] = k_s[tq - BLOCK:, :]
    vprev_s[...] = vt_s[:, tq - BLOCK:]
    for g in range(N_KV_HEADS):
        sl = slice(g * KV_SLOT, (g + 1) * KV_SLOT)
        k_s[:, sl] = _rope_rows(k_new[:, sl], cosr_ref[...], sinr_ref[...])
        gat_s[g * rows_g:(g + 1) * rows_g, :] = gates[g]
    vt_s[...] = vt_new


def _attn_branch(x, p, tabs, kmeta, vmetat, *, tq):
    b, t_len, _ = x.shape
    n_t = t_len // tq
    n_tiles = b * n_t
    kern = functools.partial(_attn_kernel, tq=tq, n_t=n_t)
    proj_tile = lambda j: jnp.minimum(j, n_tiles - 1)
    out_tile = lambda j: jnp.maximum(j - 2, 0)
    in_specs = [
        pl.BlockSpec((1, tq, D_MODEL), lambda j: (proj_tile(j) // n_t, proj_tile(j) % n_t, 0)),
        _const_spec((1, D_MODEL)), _const_spec((1, D_MODEL)),
        _const_spec((D_MODEL, D_MODEL)), _const_spec((D_MODEL, V7X_LANES)),
        _const_spec((D_MODEL, N_KV_HEADS * KV_SLOT)), _const_spec((1, N_KV_HEADS * KV_SLOT)),
        _const_spec((D_KV, D_MODEL)), _const_spec((D_KV, V7X_LANES)),
        _const_spec((D_MODEL, D_MODEL)), _const_spec((D_MODEL, V7X_LANES)),
        _const_spec((D_MODEL, D_MODEL)),
        pl.BlockSpec((HALF, tq), lambda j: (0, proj_tile(j) % n_t)),
        pl.BlockSpec((HALF, tq), lambda j: (0, proj_tile(j) % n_t)),
        pl.BlockSpec((tq, KV_SLOT), lambda j: (proj_tile(j) % n_t, 0)),
        pl.BlockSpec((tq, KV_SLOT), lambda j: (proj_tile(j) % n_t, 0)),
        _const_spec((N_META, N_KV_HEADS * KV_SLOT)), _const_spec((D_KV, N_META)),
        _const_spec((N_KV_HEADS, 1, GROUP * BLOCK)),
    ]
    return pl.pallas_call(
        kern,
        grid=(n_tiles + 2,),
        in_specs=in_specs,
        out_specs=pl.BlockSpec((1, tq, D_MODEL), lambda j: (out_tile(j) // n_t, out_tile(j) % n_t, 0)),
        out_shape=jax.ShapeDtypeStruct((b, t_len, D_MODEL), BF16),
        scratch_shapes=[pltpu.VMEM((D_MODEL, tq), F32),
                        pltpu.VMEM((tq, N_KV_HEADS * KV_SLOT), F32),
                        pltpu.VMEM((D_KV, tq), F32),
                        pltpu.VMEM((D_MODEL, tq), F32),
                        pltpu.VMEM((BLOCK, N_KV_HEADS * KV_SLOT), F32),
                        pltpu.VMEM((D_KV, BLOCK), F32),
                        pltpu.VMEM((D_MODEL, tq), F32),
                        pltpu.VMEM((tq, D_MODEL), F32)],
        compiler_params=pltpu.CompilerParams(dimension_semantics=("arbitrary",),
                                             vmem_limit_bytes=V7X_VMEM_LIMIT),
        name="attn_branch",
    )(x, p["ln_g"], p["ln_b"], p["wqt"], p["bqt"], p["wk"], p["bk"], p["wvt"], p["bvt"],
      p["wgat"], p["bgat"], p["w_attn_out"], tabs["cos_t"], tabs["sin_t"], tabs["cos_r"], tabs["sin_r"],
      kmeta, vmetat, p["sink_rows"])


def _merge_kernel(x_ref, ya_ref, yb_ref, lng_ref, lnb_ref, wmg_ref, bmg_ref, wo_ref, bo_ref,
                  g2_ref, b2_ref, out_ref, h32_s, mixed_s):
    j = pl.program_id(0)

    @pl.when(j == 0)
    def _():
        h32_s[...] = jnp.zeros_like(h32_s)
        mixed_s[...] = jnp.zeros_like(mixed_s)

    h32 = _layer_norm(x_ref[0], lng_ref[...], lnb_ref[...])
    h = h32.astype(BF16)
    out = _dot(mixed_s[...].astype(BF16), wo_ref[...]) + bo_ref[...]
    h32_prev = h32_s[...]
    ga = _dot(h, wmg_ref[:, :D_MODEL]) + bmg_ref[:, :D_MODEL]
    out_ref[0] = _layer_norm(DEEPNORM_ALPHA * h32_prev + out, g2_ref[...], b2_ref[...])
    gb = _dot(h, wmg_ref[:, D_MODEL:]) + bmg_ref[:, D_MODEL:]
    h32_s[...] = h32
    mixed_s[...] = _sigmoid(ga) * ya_ref[0].astype(F32) + _sigmoid(gb) * yb_ref[0].astype(F32)


def _merge(x, ya, yb, p, *, tm):
    b, t_len, _ = x.shape
    n_t = t_len // tm
    n_tiles = b * n_t
    tile_of = lambda t: (t // n_t, t % n_t, 0)
    tile = pl.BlockSpec((1, tm, D_MODEL), lambda j: tile_of(jnp.minimum(j, n_tiles - 1)))
    in_specs = [tile, tile, tile,
                _const_spec((1, D_MODEL)), _const_spec((1, D_MODEL)),
                _const_spec((D_MODEL, 2 * D_MODEL)), _const_spec((1, 2 * D_MODEL)),
                _const_spec((D_MODEL, D_MODEL)), _const_spec((1, D_MODEL)),
                _const_spec((1, D_MODEL)), _const_spec((1, D_MODEL))]
    return pl.pallas_call(
        _merge_kernel,
        grid=(n_tiles + 1,),
        in_specs=in_specs,
        out_specs=pl.BlockSpec((1, tm, D_MODEL), lambda j: tile_of(jnp.maximum(j - 1, 0))),
        out_shape=jax.ShapeDtypeStruct((b, t_len, D_MODEL), F32),
        scratch_shapes=[pltpu.VMEM((tm, D_MODEL), F32), pltpu.VMEM((tm, D_MODEL), F32)],
        compiler_params=pltpu.CompilerParams(dimension_semantics=("arbitrary",),
                                             vmem_limit_bytes=V7X_VMEM_LIMIT),
        name="merge",
    )(x, ya, yb, p["ln_g"], p["ln_b"], p["wmg"], p["bmg"], p["w_o"], p["b_o"], p["ln2_g"], p["ln2_b"])


def _rope_tables(t_total):
    inv = ROPE_THETA ** (-jnp.arange(HALF, dtype=F32) / HALF)
    ang = jnp.arange(t_total, dtype=F32)[:, None] * inv[None, :]
    cos, sin = jnp.cos(ang), jnp.sin(ang)
    zeros = jnp.zeros_like(cos)
    cos_r = jnp.concatenate([cos, cos, zeros, zeros], axis=1)
    sin_r = jnp.concatenate([-sin, sin, zeros, zeros], axis=1)
    return cos, sin, cos_r, sin_r


def _slotted(w):
    lead = w.shape[:-1]
    w4 = w.reshape(lead + (N_KV_HEADS, HEAD_DIM))
    w4 = jnp.concatenate([w4, jnp.zeros_like(w4)], axis=-1)
    return w4.reshape(lead + (N_KV_HEADS * KV_SLOT,))


def kernel(x, meta_tokens, ln_emb_g, ln_emb_b, w_in, b_in, conv_w, conv_b, w_ra, b_ra, w_ri, b_ri,
           lru_lambda, sinks, w_rnn_out, w_attn_out, w_o, b_o, ln_g, ln_b):
    b, seq, _ = x.shape
    assert b == V7X_SUBLANES and w_in.shape[0] == DEPTH
    nb, rb = N_RNN_BLOCKS, RNN_BLOCK
    w = w_in[0]
    bi = b_in[0]
    row = lambda v: v.reshape(1, -1)
    blocks = lambda v: v.reshape(nb, 1, rb)
    lanes = lambda v: jnp.broadcast_to(v[:, None], (v.shape[0], V7X_LANES))
    col_blocks = lambda m: m.reshape(D_MODEL, nb, rb).transpose(1, 0, 2).astype(BF16)

    common = {"ln_g": row(ln_emb_g), "ln_b": row(ln_emb_b)}
    p_rnn = dict(common,
                 wx=col_blocks(w[:, :OFF_GR]), wg=col_blocks(w[:, OFF_GR:OFF_Q]),
                 bx=blocks(bi[:OFF_GR]), bg=blocks(bi[OFF_GR:OFF_Q]),
                 cw=conv_w[0].reshape(CONV_WIDTH, nb, rb).transpose(1, 0, 2), cb=blocks(conv_b[0]),
                 wra=w_ra[0].astype(BF16), bra=blocks(b_ra[0]),
                 wri=w_ri[0].astype(BF16), bri=blocks(b_ri[0]),
                 lam=blocks(lru_lambda[0]), w_rnn_out=w_rnn_out[0].astype(BF16))
    p_attn = dict(common,
                  wqt=w[:, OFF_Q:OFF_K].T.astype(BF16), bqt=lanes(bi[OFF_Q:OFF_K]),
                  wk=_slotted(w[:, OFF_K:OFF_V]).astype(BF16), bk=row(_slotted(bi[OFF_K:OFF_V])),
                  wvt=w[:, OFF_V:OFF_GA].T.astype(BF16), bvt=lanes(bi[OFF_V:OFF_GA]),
                  wgat=w[:, OFF_GA:OFF_G].T.astype(BF16), bgat=lanes(bi[OFF_GA:OFF_G]),
                  wv=w[:, OFF_V:OFF_GA].astype(BF16), bv=row(bi[OFF_V:OFF_GA]),
                  w_attn_out=w_attn_out[0].astype(BF16),
                  sink_rows=jnp.repeat(sinks[0].astype(F32), BLOCK).reshape(N_KV_HEADS, 1, GROUP * BLOCK))
    p_merge = dict(common, wmg=w[:, OFF_G:].astype(BF16), bmg=row(bi[OFF_G:]),
                   w_o=w_o[0].astype(BF16), b_o=row(b_o[0]), ln2_g=row(ln_g[0]), ln2_b=row(ln_b[0]))

    cos, sin, cos_r, sin_r = _rope_tables(N_META + seq)
    tabs = {"cos_t": cos[N_META:].T, "sin_t": sin[N_META:].T, "cos_r": cos_r[N_META:], "sin_r": sin_r[N_META:]}

    meta = meta_tokens.astype(x.dtype)
    meta_b = jnp.broadcast_to(meta[None], (V7X_SUBLANES, N_META, D_MODEL))
    zero_hist = jnp.zeros((nb, HIST_ROWS, rb), F32)
    zero_state = jnp.zeros((nb, V7X_SUBLANES, rb), F32)
    _, hist0, state0 = _rnn_branch(meta_b, p_rnn, zero_hist, zero_state, tt=N_META, seq_start=True)
    kmeta, vmeta = _meta_kv(meta, p_attn, cos_r[:N_META], sin_r[:N_META])

    ya, _, _ = _rnn_branch(x, p_rnn, hist0, state0, tt=32, seq_start=False)
    yb = _attn_branch(x, p_attn, tabs, kmeta, vmeta.T, tq=256)
    return _merge(x, ya, yb, p_merge, tm=256)
```

```python
import functools

import jax
import jax.numpy as jnp
from jax import lax
from jax.experimental import pallas as pl
from jax.experimental.pallas import tpu as pltpu

D_MODEL = 2048
N_META = 16
N_RNN_BLOCKS = 8
RNN_BLOCK = 256
CONV_WIDTH = 4
LRU_C = 8.0
HEAD_DIM = 64
HALF = HEAD_DIM // 2
N_Q_HEADS = 32
N_KV_HEADS = 4
GROUP = 8
D_KV = N_KV_HEADS * HEAD_DIM
BLOCK = 128
ROPE_THETA = 10000.0
NEG_INF = -1e30
LN_EPS = 1e-5
DEPTH = 1
DEEPNORM_ALPHA = (2.0 * DEPTH) ** 0.25
OFF_XR = 0
OFF_GR = D_MODEL
OFF_Q = 2 * D_MODEL
OFF_K = OFF_Q + D_MODEL
OFF_V = OFF_K + D_KV
OFF_GA = OFF_V + D_KV
OFF_G = OFF_GA + D_MODEL

V7X_SUBLANES = 8
V7X_LANES = 128
V7X_VMEM_LIMIT = 56 * 1024 * 1024

KV_SLOT = V7X_LANES
HIST_ROWS = (CONV_WIDTH - 1) * V7X_SUBLANES

BF16 = jnp.bfloat16
F32 = jnp.float32


def _layer_norm(x, g, b):
    mu = jnp.mean(x, axis=-1, keepdims=True)
    xc = x - mu
    var = jnp.mean(xc * xc, axis=-1, keepdims=True)
    return xc * lax.rsqrt(var + LN_EPS) * g + b


def _sigmoid(x):
    return 1.0 / (1.0 + jnp.exp(-x))


def _dot(a, b):
    return jnp.dot(a, b, preferred_element_type=F32)


def _dot_nt(a, b):
    return lax.dot_general(a, b, (((1,), (1,)), ((), ())), preferred_element_type=F32)


def _const_spec(shape):
    nd = len(shape)
    return pl.BlockSpec(shape, lambda *_: (0,) * nd, pipeline_mode=pl.Buffered(1))


def _rnn_kernel(x_ref, lng_ref, lnb_ref, wx_ref, wg_ref, bx_ref, bg_ref, cw_ref, cb_ref,
                wra_ref, bra_ref, wri_ref, bri_ref, lam_ref, wout_ref, perm_ref, permt_ref,
                hist0_ref, state0_ref, ya_ref, hist_out_ref, state_out_ref, y_s, hist_s, state_s,
                *, tt, seq_start):
    rows = tt * V7X_SUBLANES
    nb = N_RNN_BLOCKS
    j = pl.program_id(0)

    @pl.when(j == 0)
    def _():
        y_s[...] = jnp.zeros_like(y_s)
        hist_s[...] = hist0_ref[...]
        state_s[...] = state0_ref[...]

    x = x_ref[...].reshape(rows, D_MODEL)
    h = _layer_norm(x, lng_ref[...], lnb_ref[...]).astype(BF16)
    ya = _dot(y_s[...].astype(BF16), wout_ref[...])
    ya_ref[...] = ya.reshape(V7X_SUBLANES, tt, D_MODEL).astype(ya_ref.dtype)
    h = _dot(perm_ref[...], h).astype(BF16)

    def project(n):
        cs = slice(n * RNN_BLOCK, (n + 1) * RNN_BLOCK)
        return _dot(h, wx_ref[:, cs]) + bx_ref[n], _dot(h, wg_ref[:, cs]) + bg_ref[n]

    ys = []
    nxt = project(0)
    for n in range(nb):
        xr, gr = nxt
        if n + 1 < nb:
            nxt = project(n + 1)
        xe = jnp.concatenate([hist_s[n], xr], axis=0)
        cw = cw_ref[n]
        conv = cb_ref[n] + cw[0:1] * xr
        for k in range(1, CONV_WIDTH):
            off = HIST_ROWS - k * V7X_SUBLANES
            conv = conv + cw[k:k + 1] * xe[off:off + rows]
        hist_s[n] = xe[rows:rows + HIST_ROWS]
        cb16 = conv.astype(BF16)
        gate_r = _sigmoid(_dot(cb16, wra_ref[n]) + bra_ref[n])
        gate_i = _sigmoid(_dot(cb16, wri_ref[n]) + bri_ref[n])
        lam = lam_ref[n]
        log_sig = jnp.minimum(lam, 0.0) - jnp.log1p(jnp.exp(-jnp.abs(lam)))
        a = jnp.exp((LRU_C * gate_r) * log_sig)
        mult = jnp.sqrt(1.0 - a * a)
        if seq_start:
            first = lax.broadcasted_iota(jnp.int32, (rows, RNN_BLOCK), 0) < V7X_SUBLANES
            mult = jnp.where(first & (j == 0), 1.0, mult)
        u = mult * gate_i * conv
        hcur = state_s[n]
        hs = []
        for t in range(tt):
            sl = slice(t * V7X_SUBLANES, (t + 1) * V7X_SUBLANES)
            hcur = a[sl] * hcur + u[sl]
            hs.append(hcur)
        state_s[n] = hcur
        hr = jnp.concatenate(hs, axis=0)
        ys.append((hr * (gr * _sigmoid(gr))).astype(BF16))
    y = jnp.concatenate(ys, axis=1)
    y_s[...] = _dot(permt_ref[...], y)

    @pl.when(j == pl.num_programs(0) - 2)
    def _():
        hist_out_ref[...] = hist_s[...]
        state_out_ref[...] = state_s[...]


def _rnn_branch(x, p, hist0, state0, *, tt, seq_start):
    t_len = x.shape[1]
    nb, rb = N_RNN_BLOCKS, RNN_BLOCK
    rows = tt * V7X_SUBLANES
    r = jnp.arange(rows)
    perm = jax.nn.one_hot((r % V7X_SUBLANES) * tt + r // V7X_SUBLANES, rows, dtype=BF16)
    kern = functools.partial(_rnn_kernel, tt=tt, seq_start=seq_start)
    n_tiles = t_len // tt
    in_specs = [
        pl.BlockSpec((V7X_SUBLANES, tt, D_MODEL), lambda j: (0, jnp.minimum(j, n_tiles - 1), 0)),
        _const_spec((1, D_MODEL)), _const_spec((1, D_MODEL)),
        pl.BlockSpec((D_MODEL, D_MODEL), lambda j: (0, OFF_XR // D_MODEL), pipeline_mode=pl.Buffered(1)),
        pl.BlockSpec((D_MODEL, D_MODEL), lambda j: (0, OFF_GR // D_MODEL), pipeline_mode=pl.Buffered(1)),
        _const_spec((nb, 1, rb)), _const_spec((nb, 1, rb)),
        _const_spec((nb, CONV_WIDTH, rb)), _const_spec((nb, 1, rb)),
        _const_spec((nb, rb, rb)), _const_spec((nb, 1, rb)),
        _const_spec((nb, rb, rb)), _const_spec((nb, 1, rb)),
        _const_spec((nb, 1, rb)),
        _const_spec((D_MODEL, D_MODEL)),
        _const_spec((rows, rows)), _const_spec((rows, rows)),
        _const_spec((nb, HIST_ROWS, rb)), _const_spec((nb, V7X_SUBLANES, rb)),
    ]
    out_shape = (
        jax.ShapeDtypeStruct((V7X_SUBLANES, t_len, D_MODEL), BF16),
        jax.ShapeDtypeStruct((nb, HIST_ROWS, rb), F32),
        jax.ShapeDtypeStruct((nb, V7X_SUBLANES, rb), F32),
    )
    out_specs = (
        pl.BlockSpec((V7X_SUBLANES, tt, D_MODEL), lambda j: (0, jnp.maximum(j - 1, 0), 0)),
        pl.BlockSpec((nb, HIST_ROWS, rb), lambda j: (0, 0, 0)),
        pl.BlockSpec((nb, V7X_SUBLANES, rb), lambda j: (0, 0, 0)),
    )
    return pl.pallas_call(
        kern,
        grid=(n_tiles + 1,),
        in_specs=in_specs,
        out_specs=out_specs,
        out_shape=out_shape,
        scratch_shapes=[pltpu.VMEM((rows, D_MODEL), F32),
                        pltpu.VMEM((nb, HIST_ROWS, rb), F32), pltpu.VMEM((nb, V7X_SUBLANES, rb), F32)],
        compiler_params=pltpu.CompilerParams(dimension_semantics=("arbitrary",),
                                             vmem_limit_bytes=V7X_VMEM_LIMIT),
        name="rnn_branch_start" if seq_start else "rnn_branch",
    )(x, p["ln_g"], p["ln_b"], p["w_in"], p["w_in"], p["bx"], p["bg"], p["cw"], p["cb"],
      p["wra"], p["bra"], p["wri"], p["bri"], p["lam"], p["w_rnn_out"], perm, perm.T, hist0, state0)


def _rope_rows(k, cos, sin_signed):
    lane = lax.broadcasted_iota(jnp.int32, k.shape, 1)
    partner = jnp.where((lane % HEAD_DIM) < HALF,
                        pltpu.roll(k, V7X_LANES - HALF, axis=1),
                        pltpu.roll(k, HALF, axis=1))
    return k * cos + partner * sin_signed


def _meta_kv_kernel(m_ref, lng_ref, lnb_ref, wk_ref, bk_ref, wv_ref, bv_ref, cos_ref, sin_ref,
                    k_ref, v_ref):
    h = _layer_norm(m_ref[...], lng_ref[...], lnb_ref[...]).astype(BF16)
    k = _dot(h, wk_ref[...]) + bk_ref[...]
    v = _dot(h, wv_ref[...]) + bv_ref[...]
    for g in range(N_KV_HEADS):
        sl = slice(g * KV_SLOT, (g + 1) * KV_SLOT)
        k_ref[:, sl] = _rope_rows(k[:, sl], cos_ref[...], sin_ref[...]).astype(BF16)
    v_ref[...] = v.astype(BF16)


def _meta_kv(meta, p, cos_rows, sin_rows):
    return pl.pallas_call(
        _meta_kv_kernel,
        out_shape=(jax.ShapeDtypeStruct((N_META, N_KV_HEADS * KV_SLOT), BF16),
                   jax.ShapeDtypeStruct((N_META, D_KV), BF16)),
        name="meta_kv",
    )(meta, p["ln_g"], p["ln_b"], p["wk"], p["bk"], p["wv"], p["bv"], cos_rows, sin_rows)


def _attn_kernel(x_ref, lng_ref, lnb_ref, wqt_ref, bqt_ref, wk_ref, bk_ref, wvt_ref, bvt_ref,
                 wgat_ref, bgat_ref, wout_ref, cost_ref, sint_ref, cosr_ref, sinr_ref,
                 kmeta_ref, vmetat_ref, sink_ref, yb_ref,
                 qt_s, k_s, vt_s, gat_s, kprev_s, vprev_s, ot_s, act_s, *, tq, n_t):
    nblk = tq // BLOCK
    rows_g = GROUP * HEAD_DIM
    j = pl.program_id(0)

    @pl.when(j == 0)
    def _():
        for ref in (qt_s, k_s, vt_s, gat_s, kprev_s, vprev_s, act_s):
            ref[...] = jnp.zeros_like(ref)

    h = _layer_norm(x_ref[0], lng_ref[...], lnb_ref[...]).astype(BF16)

    act_prev = act_s[...].astype(BF16)
    out_cols = D_MODEL // (N_KV_HEADS * nblk)

    def project_out(c):
        cs = slice(c * out_cols, (c + 1) * out_cols)
        yb_ref[0, :, cs] = _dot(act_prev, wout_ref[:, cs]).astype(yb_ref.dtype)

    def lanes_bias(b):
        return jnp.concatenate([b] * (tq // V7X_LANES), axis=1)

    scale = HEAD_DIM ** -0.5
    cos = cost_ref[...] * scale
    sin = sint_ref[...] * scale

    def project_q(g):
        rs = slice(g * rows_g, (g + 1) * rows_g)
        qt = _dot_nt(wqt_ref[rs, :], h) + lanes_bias(bqt_ref[rs, :])
        out = []
        for hh in range(GROUP):
            r0 = hh * HEAD_DIM
            q1, q2 = qt[r0:r0 + HALF], qt[r0 + HALF:r0 + HEAD_DIM]
            out.append((q1 * cos - q2 * sin, q2 * cos + q1 * sin))
        return out

    def store_q(g, roped):
        for hh, (lo, hi) in enumerate(roped):
            r0 = g * rows_g + hh * HEAD_DIM
            qt_s[r0:r0 + HALF, :] = lo
            qt_s[r0 + HALF:r0 + HEAD_DIM, :] = hi

    def project_gate(g):
        rs = slice(g * rows_g, (g + 1) * rows_g)
        gat = _dot_nt(wgat_ref[rs, :], h) + lanes_bias(bgat_ref[rs, :])
        return gat * _sigmoid(gat)

    i = jnp.maximum(j - 1, 0) % n_t
    k = k_s[...].astype(BF16)
    vt = vt_s[...].astype(BF16)
    k_carry, vt_carry = kprev_s[...].astype(BF16), vprev_s[...].astype(BF16)
    kmeta = kmeta_ref[...]
    vmetat = vmetat_ref[...]

    key_row = lax.broadcasted_iota(jnp.int32, (BLOCK, GROUP * BLOCK), 0)
    qry_col = lax.broadcasted_iota(jnp.int32, (BLOCK, GROUP * BLOCK), 1) % BLOCK
    cur_ok = key_row <= qry_col
    prev_in_window = key_row > qry_col
    first_prev_ok = prev_in_window & (i > 0)

    def scores(g, jb):
        tok = slice(jb * BLOCK, (jb + 1) * BLOCK)
        ks = slice(g * KV_SLOT, g * KV_SLOT + HEAD_DIM)
        if jb == 0:
            k_prev, prev_ok = k_carry, first_prev_ok
        else:
            k_prev, prev_ok = k[(jb - 1) * BLOCK:jb * BLOCK], prev_in_window
        q_g = jnp.concatenate(
            [qt_s[(g * GROUP + hh) * HEAD_DIM:(g * GROUP + hh + 1) * HEAD_DIM, tok].astype(BF16)
             for hh in range(GROUP)], axis=1)
        s_prev = jnp.where(prev_ok, _dot(k_prev[:, ks], q_g), NEG_INF)
        s_cm = _dot(jnp.concatenate([k[tok, ks], kmeta[:, ks]], axis=0), q_g)
        s_cur = jnp.where(cur_ok, s_cm[:BLOCK], NEG_INF)
        return s_prev, s_cur, s_cm[BLOCK:]

    def finish(g, jb, s):
        s_prev, s_cur, s_meta = s
        tok = slice(jb * BLOCK, (jb + 1) * BLOCK)
        vs = slice(g * HEAD_DIM, (g + 1) * HEAD_DIM)
        vt_prev = vt_carry if jb == 0 else vt[:, (jb - 1) * BLOCK:jb * BLOCK]
        sink = sink_ref[g]
        m = jnp.maximum(
            jnp.maximum(jnp.max(s_prev, axis=0, keepdims=True), jnp.max(s_cur, axis=0, keepdims=True)),
            jnp.maximum(jnp.max(s_meta, axis=0, keepdims=True), sink))
        p_prev = jnp.exp(s_prev - m)
        p_cur = jnp.exp(s_cur - m)
        p_meta = jnp.exp(s_meta - m)
        denom = (jnp.sum(p_prev, axis=0, keepdims=True) + jnp.sum(p_cur, axis=0, keepdims=True)
                 + jnp.sum(p_meta, axis=0, keepdims=True) + jnp.exp(sink - m))
        p_cm = jnp.concatenate([p_cur, p_meta], axis=0).astype(BF16)
        v_cm = jnp.concatenate([vt[vs, tok], vmetat[vs]], axis=1)
        o = _dot(vt_prev[vs], p_prev.astype(BF16)) + _dot(v_cm, p_cm)
        o = o * (1.0 / denom)
        for hh in range(GROUP):
            r0 = (g * GROUP + hh) * HEAD_DIM
            ot_s[r0:r0 + HEAD_DIM, tok] = o[:, hh * BLOCK:(hh + 1) * BLOCK]

    gates = [None] * N_KV_HEADS
    held_q = None
    for g in range(N_KV_HEADS):
        pg = (g - 1) % N_KV_HEADS
        for jb in range(nblk):
            s = scores(g, jb)
            if jb == 0:
                roped = project_q(pg)
                if pg < g:
                    store_q(pg, roped)
                else:
                    held_q = roped
            elif jb == 1:
                gates[pg] = project_gate(pg)
            project_out(g * nblk + jb)
            finish(g, jb, s)
    store_q(N_KV_HEADS - 1, held_q)

    k_new = _dot(h, wk_ref[...]) + bk_ref[...]
    vt_new = _dot_nt(wvt_ref[...], h) + lanes_bias(bvt_ref[...])

    act_s[...] = (ot_s[...] * gat_s[...]).T

    kprev_s[...] = k_s[tq - BLOCK:, :]
    vprev_s[...] = vt_s[:, tq - BLOCK:]
    for g in range(N_KV_HEADS):
        sl = slice(g * KV_SLOT, (g + 1) * KV_SLOT)
        k_s[:, sl] = _rope_rows(k_new[:, sl], cosr_ref[...], sinr_ref[...])
        gat_s[g * rows_g:(g + 1) * rows_g, :] = gates[g]
    vt_s[...] = vt_new


def _attn_branch(x, p, tabs, kmeta, vmetat, *, tq):
    b, t_len, _ = x.shape
    n_t = t_len // tq
    n_tiles = b * n_t
    kern = functools.partial(_attn_kernel, tq=tq, n_t=n_t)
    proj_tile = lambda j: jnp.minimum(j, n_tiles - 1)
    out_tile = lambda j: jnp.maximum(j - 2, 0)
    in_specs = [
        pl.BlockSpec((1, tq, D_MODEL), lambda j: (proj_tile(j) // n_t, proj_tile(j) % n_t, 0)),
        _const_spec((1, D_MODEL)), _const_spec((1, D_MODEL)),
        _const_spec((D_MODEL, D_MODEL)), _const_spec((D_MODEL, V7X_LANES)),
        _const_spec((D_MODEL, N_KV_HEADS * KV_SLOT)), _const_spec((1, N_KV_HEADS * KV_SLOT)),
        _const_spec((D_KV, D_MODEL)), _const_spec((D_KV, V7X_LANES)),
        _const_spec((D_MODEL, D_MODEL)), _const_spec((D_MODEL, V7X_LANES)),
        _const_spec((D_MODEL, D_MODEL)),
        pl.BlockSpec((HALF, tq), lambda j: (0, proj_tile(j) % n_t)),
        pl.BlockSpec((HALF, tq), lambda j: (0, proj_tile(j) % n_t)),
        pl.BlockSpec((tq, KV_SLOT), lambda j: (proj_tile(j) % n_t, 0)),
        pl.BlockSpec((tq, KV_SLOT), lambda j: (proj_tile(j) % n_t, 0)),
        _const_spec((N_META, N_KV_HEADS * KV_SLOT)), _const_spec((D_KV, N_META)),
        _const_spec((N_KV_HEADS, 1, GROUP * BLOCK)),
    ]
    return pl.pallas_call(
        kern,
        grid=(n_tiles + 2,),
        in_specs=in_specs,
        out_specs=pl.BlockSpec((1, tq, D_MODEL), lambda j: (out_tile(j) // n_t, out_tile(j) % n_t, 0)),
        out_shape=jax.ShapeDtypeStruct((b, t_len, D_MODEL), BF16),
        scratch_shapes=[pltpu.VMEM((D_MODEL, tq), F32),
                        pltpu.VMEM((tq, N_KV_HEADS * KV_SLOT), F32),
                        pltpu.VMEM((D_KV, tq), F32),
                        pltpu.VMEM((D_MODEL, tq), F32),
                        pltpu.VMEM((BLOCK, N_KV_HEADS * KV_SLOT), F32),
                        pltpu.VMEM((D_KV, BLOCK), F32),
                        pltpu.VMEM((D_MODEL, tq), F32),
                        pltpu.VMEM((tq, D_MODEL), F32)],
        compiler_params=pltpu.CompilerParams(dimension_semantics=("arbitrary",),
                                             vmem_limit_bytes=V7X_VMEM_LIMIT),
        name="attn_branch",
    )(x, p["ln_g"], p["ln_b"], p["wqt"], p["bqt"], p["wk"], p["bk"], p["wvt"], p["bvt"],
      p["wgat"], p["bgat"], p["w_attn_out"], tabs["cos_t"], tabs["sin_t"], tabs["cos_r"], tabs["sin_r"],
      kmeta, vmetat, p["sink_rows"])


def _merge_kernel(x_ref, ya_ref, yb_ref, lng_ref, lnb_ref, wmg_ref, bmg_ref, wo_ref, bo_ref,
                  g2_ref, b2_ref, out_ref, h32_s, mixed_s):
    j = pl.program_id(0)

    @pl.when(j == 0)
    def _():
        h32_s[...] = jnp.zeros_like(h32_s)
        mixed_s[...] = jnp.zeros_like(mixed_s)

    h32 = _layer_norm(x_ref[0], lng_ref[...], lnb_ref[...])
    h = h32.astype(BF16)
    out = _dot(mixed_s[...].astype(BF16), wo_ref[...]) + bo_ref[...]
    h32_prev = h32_s[...]
    ga = _dot(h, wmg_ref[:, :D_MODEL]) + bmg_ref[:, :D_MODEL]
    out_ref[0] = _layer_norm(DEEPNORM_ALPHA * h32_prev + out, g2_ref[...], b2_ref[...])
    gb = _dot(h, wmg_ref[:, D_MODEL:]) + bmg_ref[:, D_MODEL:]
    h32_s[...] = h32
    mixed_s[...] = _sigmoid(ga) * ya_ref[0].astype(F32) + _sigmoid(gb) * yb_ref[0].astype(F32)


def _merge(x, ya, yb, p, *, tm):
    b, t_len, _ = x.shape
    n_t = t_len // tm
    n_tiles = b * n_t
    tile_of = lambda t: (t // n_t, t % n_t, 0)
    tile = pl.BlockSpec((1, tm, D_MODEL), lambda j: tile_of(jnp.minimum(j, n_tiles - 1)))
    in_specs = [tile, tile, tile,
                _const_spec((1, D_MODEL)), _const_spec((1, D_MODEL)),
                _const_spec((D_MODEL, 2 * D_MODEL)), _const_spec((1, 2 * D_MODEL)),
                _const_spec((D_MODEL, D_MODEL)), _const_spec((1, D_MODEL)),
                _const_spec((1, D_MODEL)), _const_spec((1, D_MODEL))]
    return pl.pallas_call(
        _merge_kernel,
        grid=(n_tiles + 1,),
        in_specs=in_specs,
        out_specs=pl.BlockSpec((1, tm, D_MODEL), lambda j: tile_of(jnp.maximum(j - 1, 0))),
        out_shape=jax.ShapeDtypeStruct((b, t_len, D_MODEL), F32),
        scratch_shapes=[pltpu.VMEM((tm, D_MODEL), F32), pltpu.VMEM((tm, D_MODEL), F32)],
        compiler_params=pltpu.CompilerParams(dimension_semantics=("arbitrary",),
                                             vmem_limit_bytes=V7X_VMEM_LIMIT),
        name="merge",
    )(x, ya, yb, p["ln_g"], p["ln_b"], p["wmg"], p["bmg"], p["w_o"], p["b_o"], p["ln2_g"], p["ln2_b"])


def _rope_tables(t_total):
    inv = ROPE_THETA ** (-jnp.arange(HALF, dtype=F32) / HALF)
    ang = jnp.arange(t_total, dtype=F32)[:, None] * inv[None, :]
    cos, sin = jnp.cos(ang), jnp.sin(ang)
    zeros = jnp.zeros_like(cos)
    cos_r = jnp.concatenate([cos, cos, zeros, zeros], axis=1)
    sin_r = jnp.concatenate([-sin, sin, zeros, zeros], axis=1)
    return cos, sin, cos_r, sin_r


def _slotted(w):
    lead = w.shape[:-1]
    w4 = w.reshape(lead + (N_KV_HEADS, HEAD_DIM))
    w4 = jnp.concatenate([w4, jnp.zeros_like(w4)], axis=-1)
    return w4.reshape(lead + (N_KV_HEADS * KV_SLOT,))


def kernel(x, meta_tokens, ln_emb_g, ln_emb_b, w_in, b_in, conv_w, conv_b, w_ra, b_ra, w_ri, b_ri,
           lru_lambda, sinks, w_rnn_out, w_attn_out, w_o, b_o, ln_g, ln_b):
    b, seq, _ = x.shape
    assert b == V7X_SUBLANES and w_in.shape[0] == DEPTH
    nb, rb = N_RNN_BLOCKS, RNN_BLOCK
    w = w_in[0].astype(BF16)
    bi = b_in[0]
    row = lambda v: v.reshape(1, -1)
    blocks = lambda v: v.reshape(nb, 1, rb)
    lanes = lambda v: jnp.broadcast_to(v[:, None], (v.shape[0], V7X_LANES))

    common = {"ln_g": row(ln_emb_g), "ln_b": row(ln_emb_b)}
    p_rnn = dict(common, w_in=w,
                 bx=blocks(bi[:OFF_GR]), bg=blocks(bi[OFF_GR:OFF_Q]),
                 cw=conv_w[0].reshape(CONV_WIDTH, nb, rb).transpose(1, 0, 2), cb=blocks(conv_b[0]),
                 wra=w_ra[0].astype(BF16), bra=blocks(b_ra[0]),
                 wri=w_ri[0].astype(BF16), bri=blocks(b_ri[0]),
                 lam=blocks(lru_lambda[0]), w_rnn_out=w_rnn_out[0].astype(BF16))
    p_attn = dict(common,
                  wqt=w[:, OFF_Q:OFF_K].T, bqt=lanes(bi[OFF_Q:OFF_K]),
                  wk=_slotted(w[:, OFF_K:OFF_V]), bk=row(_slotted(bi[OFF_K:OFF_V])),
                  wvt=w[:, OFF_V:OFF_GA].T, bvt=lanes(bi[OFF_V:OFF_GA]),
                  wgat=w[:, OFF_GA:OFF_G].T, bgat=lanes(bi[OFF_GA:OFF_G]),
                  wv=w[:, OFF_V:OFF_GA], bv=row(bi[OFF_V:OFF_GA]),
                  w_attn_out=w_attn_out[0].astype(BF16),
                  sink_rows=jnp.repeat(sinks[0].astype(F32), BLOCK).reshape(N_KV_HEADS, 1, GROUP * BLOCK))
    p_merge = dict(common, wmg=w[:, OFF_G:], bmg=row(bi[OFF_G:]),
                   w_o=w_o[0].astype(BF16), b_o=row(b_o[0]), ln2_g=row(ln_g[0]), ln2_b=row(ln_b[0]))

    cos, sin, cos_r, sin_r = _rope_tables(N_META + seq)
    tabs = {"cos_t": cos[N_META:].T, "sin_t": sin[N_META:].T, "cos_r": cos_r[N_META:], "sin_r": sin_r[N_META:]}

    meta = meta_tokens.astype(x.dtype)
    meta_b = jnp.broadcast_to(meta[None], (V7X_SUBLANES, N_META, D_MODEL))
    zero_hist = jnp.zeros((nb, HIST_ROWS, rb), F32)
    zero_state = jnp.zeros((nb, V7X_SUBLANES, rb), F32)
    _, hist0, state0 = _rnn_branch(meta_b, p_rnn, zero_hist, zero_state, tt=N_META, seq_start=True)
    kmeta, vmeta = _meta_kv(meta, p_attn, cos_r[:N_META], sin_r[:N_META])

    ya, _, _ = _rnn_branch(x, p_rnn, hist0, state0, tt=32, seq_start=False)
    yb = _attn_branch(x, p_attn, tabs, kmeta, vmeta.T, tq=256)
    return _merge(x, ya, yb, p_merge, tm=256)
```

```python
import functools

import jax
import jax.numpy as jnp
from jax import lax
from jax.experimental import pallas as pl
from jax.experimental.pallas import tpu as pltpu

D_MODEL = 2048
N_META = 16
N_RNN_BLOCKS = 8
RNN_BLOCK = 256
CONV_WIDTH = 4
LRU_C = 8.0
HEAD_DIM = 64
HALF = HEAD_DIM // 2
N_Q_HEADS = 32
N_KV_HEADS = 4
GROUP = 8
D_KV = N_KV_HEADS * HEAD_DIM
BLOCK = 128
ROPE_THETA = 10000.0
NEG_INF = -1e30
LN_EPS = 1e-5
DEPTH = 1
DEEPNORM_ALPHA = (2.0 * DEPTH) ** 0.25
OFF_XR = 0
OFF_GR = D_MODEL
OFF_Q = 2 * D_MODEL
OFF_K = OFF_Q + D_MODEL
OFF_V = OFF_K + D_KV
OFF_GA = OFF_V + D_KV
OFF_G = OFF_GA + D_MODEL

V7X_SUBLANES = 8
V7X_LANES = 128
V7X_VMEM_LIMIT = 56 * 1024 * 1024

KV_SLOT = V7X_LANES
HIST_ROWS = (CONV_WIDTH - 1) * V7X_SUBLANES

BF16 = jnp.bfloat16
F32 = jnp.float32


def _layer_norm(x, g, b):
    mu = jnp.mean(x, axis=-1, keepdims=True)
    xc = x - mu
    var = jnp.mean(xc * xc, axis=-1, keepdims=True)
    return xc * lax.rsqrt(var + LN_EPS) * g + b


def _sigmoid(x):
    return 1.0 / (1.0 + jnp.exp(-x))


def _dot(a, b):
    return jnp.dot(a, b, preferred_element_type=F32)


def _dot_nt(a, b):
    return lax.dot_general(a, b, (((1,), (1,)), ((), ())), preferred_element_type=F32)


def _const_spec(shape):
    nd = len(shape)
    return pl.BlockSpec(shape, lambda *_: (0,) * nd, pipeline_mode=pl.Buffered(1))


def _rnn_kernel(x_ref, lng_ref, lnb_ref, *refs, tt, seq_start):
    wx_refs, wg_refs = refs[:N_RNN_BLOCKS], refs[N_RNN_BLOCKS:2 * N_RNN_BLOCKS]
    (bx_ref, bg_ref, cw_ref, cb_ref, wra_ref, bra_ref, wri_ref, bri_ref, lam_ref, wout_ref, perm_ref,
     permt_ref, hist0_ref, state0_ref, ya_ref, hist_out_ref, state_out_ref,
     y_s, hist_s, state_s) = refs[2 * N_RNN_BLOCKS:]
    rows = tt * V7X_SUBLANES
    nb = N_RNN_BLOCKS
    j = pl.program_id(0)

    @pl.when(j == 0)
    def _():
        y_s[...] = jnp.zeros_like(y_s)
        hist_s[...] = hist0_ref[...]
        state_s[...] = state0_ref[...]

    x = x_ref[...].reshape(rows, D_MODEL)
    h = _layer_norm(x, lng_ref[...], lnb_ref[...]).astype(BF16)
    ya = _dot(y_s[...].astype(BF16), wout_ref[...])
    ya_ref[...] = ya.reshape(V7X_SUBLANES, tt, D_MODEL).astype(ya_ref.dtype)
    h = _dot(perm_ref[...], h).astype(BF16)

    def project(n):
        return _dot(h, wx_refs[n][...]) + bx_ref[n], _dot(h, wg_refs[n][...]) + bg_ref[n]

    ys = []
    nxt = project(0)
    for n in range(nb):
        xr, gr = nxt
        if n + 1 < nb:
            nxt = project(n + 1)
        xe = jnp.concatenate([hist_s[n], xr], axis=0)
        cw = cw_ref[n]
        conv = cb_ref[n] + cw[0:1] * xr
        for k in range(1, CONV_WIDTH):
            off = HIST_ROWS - k * V7X_SUBLANES
            conv = conv + cw[k:k + 1] * xe[off:off + rows]
        hist_s[n] = xe[rows:rows + HIST_ROWS]
        cb16 = conv.astype(BF16)
        gate_r = _sigmoid(_dot(cb16, wra_ref[n]) + bra_ref[n])
        gate_i = _sigmoid(_dot(cb16, wri_ref[n]) + bri_ref[n])
        lam = lam_ref[n]
        log_sig = jnp.minimum(lam, 0.0) - jnp.log1p(jnp.exp(-jnp.abs(lam)))
        a = jnp.exp((LRU_C * gate_r) * log_sig)
        mult = jnp.sqrt(1.0 - a * a)
        if seq_start:
            first = lax.broadcasted_iota(jnp.int32, (rows, RNN_BLOCK), 0) < V7X_SUBLANES
            mult = jnp.where(first & (j == 0), 1.0, mult)
        u = mult * gate_i * conv
        hcur = state_s[n]
        hs = []
        for t in range(tt):
            sl = slice(t * V7X_SUBLANES, (t + 1) * V7X_SUBLANES)
            hcur = a[sl] * hcur + u[sl]
            hs.append(hcur)
        state_s[n] = hcur
        hr = jnp.concatenate(hs, axis=0)
        ys.append((hr * (gr * _sigmoid(gr))).astype(BF16))
    y = jnp.concatenate(ys, axis=1)
    y_s[...] = _dot(permt_ref[...], y)

    @pl.when(j == pl.num_programs(0) - 2)
    def _():
        hist_out_ref[...] = hist_s[...]
        state_out_ref[...] = state_s[...]


def _rnn_branch(x, p, hist0, state0, *, tt, seq_start):
    t_len = x.shape[1]
    nb, rb = N_RNN_BLOCKS, RNN_BLOCK
    rows = tt * V7X_SUBLANES
    r = jnp.arange(rows)
    perm = jax.nn.one_hot((r % V7X_SUBLANES) * tt + r // V7X_SUBLANES, rows, dtype=BF16)
    kern = functools.partial(_rnn_kernel, tt=tt, seq_start=seq_start)
    n_tiles = t_len // tt
    w_in_block = lambda c: pl.BlockSpec((D_MODEL, rb), lambda j: (0, c), pipeline_mode=pl.Buffered(1))
    in_specs = [
        pl.BlockSpec((V7X_SUBLANES, tt, D_MODEL), lambda j: (0, jnp.minimum(j, n_tiles - 1), 0)),
        _const_spec((1, D_MODEL)), _const_spec((1, D_MODEL)),
        *[w_in_block(OFF_XR // rb + n) for n in range(nb)],
        *[w_in_block(OFF_GR // rb + n) for n in range(nb)],
        _const_spec((nb, 1, rb)), _const_spec((nb, 1, rb)),
        _const_spec((nb, CONV_WIDTH, rb)), _const_spec((nb, 1, rb)),
        _const_spec((nb, rb, rb)), _const_spec((nb, 1, rb)),
        _const_spec((nb, rb, rb)), _const_spec((nb, 1, rb)),
        _const_spec((nb, 1, rb)),
        _const_spec((D_MODEL, D_MODEL)),
        _const_spec((rows, rows)), _const_spec((rows, rows)),
        _const_spec((nb, HIST_ROWS, rb)), _const_spec((nb, V7X_SUBLANES, rb)),
    ]
    out_shape = (
        jax.ShapeDtypeStruct((V7X_SUBLANES, t_len, D_MODEL), BF16),
        jax.ShapeDtypeStruct((nb, HIST_ROWS, rb), F32),
        jax.ShapeDtypeStruct((nb, V7X_SUBLANES, rb), F32),
    )
    out_specs = (
        pl.BlockSpec((V7X_SUBLANES, tt, D_MODEL), lambda j: (0, jnp.maximum(j - 1, 0), 0)),
        pl.BlockSpec((nb, HIST_ROWS, rb), lambda j: (0, 0, 0)),
        pl.BlockSpec((nb, V7X_SUBLANES, rb), lambda j: (0, 0, 0)),
    )
    return pl.pallas_call(
        kern,
        grid=(n_tiles + 1,),
        in_specs=in_specs,
        out_specs=out_specs,
        out_shape=out_shape,
        scratch_shapes=[pltpu.VMEM((rows, D_MODEL), F32),
                        pltpu.VMEM((nb, HIST_ROWS, rb), F32), pltpu.VMEM((nb, V7X_SUBLANES, rb), F32)],
        compiler_params=pltpu.CompilerParams(dimension_semantics=("arbitrary",),
                                             vmem_limit_bytes=V7X_VMEM_LIMIT),
        name="rnn_branch_start" if seq_start else "rnn_branch",
    )(x, p["ln_g"], p["ln_b"], *([p["w_in"]] * (2 * nb)), p["bx"], p["bg"], p["cw"], p["cb"],
      p["wra"], p["bra"], p["wri"], p["bri"], p["lam"], p["w_rnn_out"], perm, perm.T, hist0, state0)


def _rope_rows(k, cos, sin_signed):
    lane = lax.broadcasted_iota(jnp.int32, k.shape, 1)
    partner = jnp.where((lane % HEAD_DIM) < HALF,
                        pltpu.roll(k, V7X_LANES - HALF, axis=1),
                        pltpu.roll(k, HALF, axis=1))
    return k * cos + partner * sin_signed


def _meta_kv_kernel(m_ref, lng_ref, lnb_ref, wk_ref, bk_ref, wv_ref, bv_ref, cos_ref, sin_ref,
                    k_ref, v_ref):
    h = _layer_norm(m_ref[...], lng_ref[...], lnb_ref[...]).astype(BF16)
    k = _dot(h, wk_ref[...]) + bk_ref[...]
    v = _dot(h, wv_ref[...]) + bv_ref[...]
    for g in range(N_KV_HEADS):
        sl = slice(g * KV_SLOT, (g + 1) * KV_SLOT)
        k_ref[:, sl] = _rope_rows(k[:, sl], cos_ref[...], sin_ref[...]).astype(BF16)
    v_ref[...] = v.astype(BF16)


def _meta_kv(meta, p, cos_rows, sin_rows):
    return pl.pallas_call(
        _meta_kv_kernel,
        out_shape=(jax.ShapeDtypeStruct((N_META, N_KV_HEADS * KV_SLOT), BF16),
                   jax.ShapeDtypeStruct((N_META, D_KV), BF16)),
        name="meta_kv",
    )(meta, p["ln_g"], p["ln_b"], p["wk"], p["bk"], p["wv"], p["bv"], cos_rows, sin_rows)


def _attn_kernel(x_ref, lng_ref, lnb_ref, wqt_ref, bqt_ref, wk_ref, bk_ref, wvt_ref, bvt_ref,
                 wgat_ref, bgat_ref, wout_ref, cost_ref, sint_ref, cosr_ref, sinr_ref,
                 kmeta_ref, vmetat_ref, sink_ref, yb_ref,
                 qt_s, k_s, vt_s, gat_s, kprev_s, vprev_s, ot_s, act_s, *, tq, n_t):
    nblk = tq // BLOCK
    rows_g = GROUP * HEAD_DIM
    j = pl.program_id(0)

    @pl.when(j == 0)
    def _():
        for ref in (qt_s, k_s, vt_s, gat_s, kprev_s, vprev_s, act_s):
            ref[...] = jnp.zeros_like(ref)

    h = _layer_norm(x_ref[0], lng_ref[...], lnb_ref[...]).astype(BF16)

    act_prev = act_s[...].astype(BF16)
    out_cols = D_MODEL // (N_KV_HEADS * nblk)

    def project_out(c):
        cs = slice(c * out_cols, (c + 1) * out_cols)
        yb_ref[0, :, cs] = _dot(act_prev, wout_ref[:, cs]).astype(yb_ref.dtype)

    def lanes_bias(b):
        return jnp.concatenate([b] * (tq // V7X_LANES), axis=1)

    scale = HEAD_DIM ** -0.5
    cos = cost_ref[...] * scale
    sin = sint_ref[...] * scale

    def project_q(g):
        rs = slice(g * rows_g, (g + 1) * rows_g)
        qt = _dot_nt(wqt_ref[rs, :], h) + lanes_bias(bqt_ref[rs, :])
        out = []
        for hh in range(GROUP):
            r0 = hh * HEAD_DIM
            q1, q2 = qt[r0:r0 + HALF], qt[r0 + HALF:r0 + HEAD_DIM]
            out.append((q1 * cos - q2 * sin, q2 * cos + q1 * sin))
        return out

    def store_q(g, roped):
        for hh, (lo, hi) in enumerate(roped):
            r0 = g * rows_g + hh * HEAD_DIM
            qt_s[r0:r0 + HALF, :] = lo
            qt_s[r0 + HALF:r0 + HEAD_DIM, :] = hi

    def project_gate(g):
        rs = slice(g * rows_g, (g + 1) * rows_g)
        gat = _dot_nt(wgat_ref[rs, :], h) + lanes_bias(bgat_ref[rs, :])
        return gat * _sigmoid(gat)

    i = jnp.maximum(j - 1, 0) % n_t
    k = k_s[...].astype(BF16)
    vt = vt_s[...].astype(BF16)
    k_carry, vt_carry = kprev_s[...].astype(BF16), vprev_s[...].astype(BF16)
    kmeta = kmeta_ref[...]
    vmetat = vmetat_ref[...]

    key_row = lax.broadcasted_iota(jnp.int32, (BLOCK, GROUP * BLOCK), 0)
    qry_col = lax.broadcasted_iota(jnp.int32, (BLOCK, GROUP * BLOCK), 1) % BLOCK
    cur_ok = key_row <= qry_col
    prev_in_window = key_row > qry_col
    first_prev_ok = prev_in_window & (i > 0)

    def scores(g, jb):
        tok = slice(jb * BLOCK, (jb + 1) * BLOCK)
        ks = slice(g * KV_SLOT, g * KV_SLOT + HEAD_DIM)
        if jb == 0:
            k_prev, prev_ok = k_carry, first_prev_ok
        else:
            k_prev, prev_ok = k[(jb - 1) * BLOCK:jb * BLOCK], prev_in_window
        q_g = jnp.concatenate(
            [qt_s[(g * GROUP + hh) * HEAD_DIM:(g * GROUP + hh + 1) * HEAD_DIM, tok].astype(BF16)
             for hh in range(GROUP)], axis=1)
        s_prev = jnp.where(prev_ok, _dot(k_prev[:, ks], q_g), NEG_INF)
        s_cm = _dot(jnp.concatenate([k[tok, ks], kmeta[:, ks]], axis=0), q_g)
        s_cur = jnp.where(cur_ok, s_cm[:BLOCK], NEG_INF)
        return s_prev, s_cur, s_cm[BLOCK:]

    def finish(g, jb, s):
        s_prev, s_cur, s_meta = s
        tok = slice(jb * BLOCK, (jb + 1) * BLOCK)
        vs = slice(g * HEAD_DIM, (g + 1) * HEAD_DIM)
        vt_prev = vt_carry if jb == 0 else vt[:, (jb - 1) * BLOCK:jb * BLOCK]
        sink = sink_ref[g]
        m = jnp.maximum(
            jnp.maximum(jnp.max(s_prev, axis=0, keepdims=True), jnp.max(s_cur, axis=0, keepdims=True)),
            jnp.maximum(jnp.max(s_meta, axis=0, keepdims=True), sink))
        p_prev = jnp.exp(s_prev - m)
        p_cur = jnp.exp(s_cur - m)
        p_meta = jnp.exp(s_meta - m)
        denom = (jnp.sum(p_prev, axis=0, keepdims=True) + jnp.sum(p_cur, axis=0, keepdims=True)
                 + jnp.sum(p_meta, axis=0, keepdims=True) + jnp.exp(sink - m))
        p_cm = jnp.concatenate([p_cur, p_meta], axis=0).astype(BF16)
        v_cm = jnp.concatenate([vt[vs, tok], vmetat[vs]], axis=1)
        o = _dot(vt_prev[vs], p_prev.astype(BF16)) + _dot(v_cm, p_cm)
        o = o * (1.0 / denom)
        for hh in range(GROUP):
            r0 = (g * GROUP + hh) * HEAD_DIM
            ot_s[r0:r0 + HEAD_DIM, tok] = o[:, hh * BLOCK:(hh + 1) * BLOCK]

    gates = [None] * N_KV_HEADS
    held_q = None
    for g in range(N_KV_HEADS):
        pg = (g - 1) % N_KV_HEADS
        for jb in range(nblk):
            s = scores(g, jb)
            if jb == 0:
                roped = project_q(pg)
                if pg < g:
                    store_q(pg, roped)
                else:
                    held_q = roped
            elif jb == 1:
                gates[pg] = project_gate(pg)
            project_out(g * nblk + jb)
            finish(g, jb, s)
    store_q(N_KV_HEADS - 1, held_q)

    k_new = _dot(h, wk_ref[...]) + bk_ref[...]
    vt_new = _dot_nt(wvt_ref[...], h) + lanes_bias(bvt_ref[...])

    act_s[...] = (ot_s[...] * gat_s[...]).T

    kprev_s[...] = k_s[tq - BLOCK:, :]
    vprev_s[...] = vt_s[:, tq - BLOCK:]
    for g in range(N_KV_HEADS):
        sl = slice(g * KV_SLOT, (g + 1) * KV_SLOT)
        k_s[:, sl] = _rope_rows(k_new[:, sl], cosr_ref[...], sinr_ref[...])
        gat_s[g * rows_g:(g + 1) * rows_g, :] = gates[g]
    vt_s[...] = vt_new


def _attn_branch(x, p, tabs, kmeta, vmetat, *, tq):
    b, t_len, _ = x.shape
    n_t = t_len // tq
    n_tiles = b * n_t
    kern = functools.partial(_attn_kernel, tq=tq, n_t=n_t)
    proj_tile = lambda j: jnp.minimum(j, n_tiles - 1)
    out_tile = lambda j: jnp.maximum(j - 2, 0)
    in_specs = [
        pl.BlockSpec((1, tq, D_MODEL), lambda j: (proj_tile(j) // n_t, proj_tile(j) % n_t, 0)),
        _const_spec((1, D_MODEL)), _const_spec((1, D_MODEL)),
        _const_spec((D_MODEL, D_MODEL)), _const_spec((D_MODEL, V7X_LANES)),
        _const_spec((D_MODEL, N_KV_HEADS * KV_SLOT)), _const_spec((1, N_KV_HEADS * KV_SLOT)),
        _const_spec((D_KV, D_MODEL)), _const_spec((D_KV, V7X_LANES)),
        _const_spec((D_MODEL, D_MODEL)), _const_spec((D_MODEL, V7X_LANES)),
        _const_spec((D_MODEL, D_MODEL)),
        pl.BlockSpec((HALF, tq), lambda j: (0, proj_tile(j) % n_t)),
        pl.BlockSpec((HALF, tq), lambda j: (0, proj_tile(j) % n_t)),
        pl.BlockSpec((tq, KV_SLOT), lambda j: (proj_tile(j) % n_t, 0)),
        pl.BlockSpec((tq, KV_SLOT), lambda j: (proj_tile(j) % n_t, 0)),
        _const_spec((N_META, N_KV_HEADS * KV_SLOT)), _const_spec((D_KV, N_META)),
        _const_spec((N_KV_HEADS, 1, GROUP * BLOCK)),
    ]
    return pl.pallas_call(
        kern,
        grid=(n_tiles + 2,),
        in_specs=in_specs,
        out_specs=pl.BlockSpec((1, tq, D_MODEL), lambda j: (out_tile(j) // n_t, out_tile(j) % n_t, 0)),
        out_shape=jax.ShapeDtypeStruct((b, t_len, D_MODEL), BF16),
        scratch_shapes=[pltpu.VMEM((D_MODEL, tq), F32),
                        pltpu.VMEM((tq, N_KV_HEADS * KV_SLOT), F32),
                        pltpu.VMEM((D_KV, tq), F32),
                        pltpu.VMEM((D_MODEL, tq), F32),
                        pltpu.VMEM((BLOCK, N_KV_HEADS * KV_SLOT), F32),
                        pltpu.VMEM((D_KV, BLOCK), F32),
                        pltpu.VMEM((D_MODEL, tq), F32),
                        pltpu.VMEM((tq, D_MODEL), F32)],
        compiler_params=pltpu.CompilerParams(dimension_semantics=("arbitrary",),
                                             vmem_limit_bytes=V7X_VMEM_LIMIT),
        name="attn_branch",
    )(x, p["ln_g"], p["ln_b"], p["wqt"], p["bqt"], p["wk"], p["bk"], p["wvt"], p["bvt"],
      p["wgat"], p["bgat"], p["w_attn_out"], tabs["cos_t"], tabs["sin_t"], tabs["cos_r"], tabs["sin_r"],
      kmeta, vmetat, p["sink_rows"])


def _merge_kernel(x_ref, ya_ref, yb_ref, lng_ref, lnb_ref, wmg_ref, bmg_ref, wo_ref, bo_ref,
                  g2_ref, b2_ref, out_ref):
    h32 = _layer_norm(x_ref[0], lng_ref[...], lnb_ref[...])
    h = h32.astype(BF16)
    ga = _sigmoid(_dot(h, wmg_ref[:, :D_MODEL]) + bmg_ref[:, :D_MODEL])
    gb = _sigmoid(_dot(h, wmg_ref[:, D_MODEL:]) + bmg_ref[:, D_MODEL:])
    mixed = ga * ya_ref[0].astype(F32) + gb * yb_ref[0].astype(F32)
    out = _dot(mixed.astype(BF16), wo_ref[...]) + bo_ref[...]
    out_ref[0] = _layer_norm(DEEPNORM_ALPHA * h32 + out, g2_ref[...], b2_ref[...])


def _merge(x, ya, yb, p, *, tm):
    b, t_len, _ = x.shape
    tile = pl.BlockSpec((1, tm, D_MODEL), lambda bi, i: (bi, i, 0))
    in_specs = [tile, tile, tile,
                _const_spec((1, D_MODEL)), _const_spec((1, D_MODEL)),
                _const_spec((D_MODEL, 2 * D_MODEL)), _const_spec((1, 2 * D_MODEL)),
                _const_spec((D_MODEL, D_MODEL)), _const_spec((1, D_MODEL)),
                _const_spec((1, D_MODEL)), _const_spec((1, D_MODEL))]
    return pl.pallas_call(
        _merge_kernel,
        grid=(b, t_len // tm),
        in_specs=in_specs,
        out_specs=tile,
        out_shape=jax.ShapeDtypeStruct((b, t_len, D_MODEL), F32),
        compiler_params=pltpu.CompilerParams(dimension_semantics=("arbitrary", "arbitrary"),
                                             vmem_limit_bytes=V7X_VMEM_LIMIT),
        name="merge",
    )(x, ya, yb, p["ln_g"], p["ln_b"], p["wmg"], p["bmg"], p["w_o"], p["b_o"], p["ln2_g"], p["ln2_b"])


def _rope_tables(t_total):
    inv = ROPE_THETA ** (-jnp.arange(HALF, dtype=F32) / HALF)
    ang = jnp.arange(t_total, dtype=F32)[:, None] * inv[None, :]
    cos, sin = jnp.cos(ang), jnp.sin(ang)
    zeros = jnp.zeros_like(cos)
    cos_r = jnp.concatenate([cos, cos, zeros, zeros], axis=1)
    sin_r = jnp.concatenate([-sin, sin, zeros, zeros], axis=1)
    return cos, sin, cos_r, sin_r


def _slotted(w):
    lead = w.shape[:-1]
    w4 = w.reshape(lead + (N_KV_HEADS, HEAD_DIM))
    w4 = jnp.concatenate([w4, jnp.zeros_like(w4)], axis=-1)
    return w4.reshape(lead + (N_KV_HEADS * KV_SLOT,))


def kernel(x, meta_tokens, ln_emb_g, ln_emb_b, w_in, b_in, conv_w, conv_b, w_ra, b_ra, w_ri, b_ri,
           lru_lambda, sinks, w_rnn_out, w_attn_out, w_o, b_o, ln_g, ln_b):
    b, seq, _ = x.shape
    assert b == V7X_SUBLANES and w_in.shape[0] == DEPTH
    nb, rb = N_RNN_BLOCKS, RNN_BLOCK
    w = w_in[0].astype(BF16)
    bi = b_in[0]
    row = lambda v: v.reshape(1, -1)
    blocks = lambda v: v.reshape(nb, 1, rb)
    lanes = lambda v: jnp.broadcast_to(v[:, None], (v.shape[0], V7X_LANES))

    common = {"ln_g": row(ln_emb_g), "ln_b": row(ln_emb_b)}
    p_rnn = dict(common, w_in=w,
                 bx=blocks(bi[:OFF_GR]), bg=blocks(bi[OFF_GR:OFF_Q]),
                 cw=conv_w[0].reshape(CONV_WIDTH, nb, rb).transpose(1, 0, 2), cb=blocks(conv_b[0]),
                 wra=w_ra[0].astype(BF16), bra=blocks(b_ra[0]),
                 wri=w_ri[0].astype(BF16), bri=blocks(b_ri[0]),
                 lam=blocks(lru_lambda[0]), w_rnn_out=w_rnn_out[0].astype(BF16))
    p_attn = dict(common,
                  wqt=w[:, OFF_Q:OFF_K].T, bqt=lanes(bi[OFF_Q:OFF_K]),
                  wk=_slotted(w[:, OFF_K:OFF_V]), bk=row(_slotted(bi[OFF_K:OFF_V])),
                  wvt=w[:, OFF_V:OFF_GA].T, bvt=lanes(bi[OFF_V:OFF_GA]),
                  wgat=w[:, OFF_GA:OFF_G].T, bgat=lanes(bi[OFF_GA:OFF_G]),
                  wv=w[:, OFF_V:OFF_GA], bv=row(bi[OFF_V:OFF_GA]),
                  w_attn_out=w_attn_out[0].astype(BF16),
                  sink_rows=jnp.repeat(sinks[0].astype(F32), BLOCK).reshape(N_KV_HEADS, 1, GROUP * BLOCK))
    p_merge = dict(common, wmg=w[:, OFF_G:], bmg=row(bi[OFF_G:]),
                   w_o=w_o[0].astype(BF16), b_o=row(b_o[0]), ln2_g=row(ln_g[0]), ln2_b=row(ln_b[0]))

    cos, sin, cos_r, sin_r = _rope_tables(N_META + seq)
    tabs = {"cos_t": cos[N_META:].T, "sin_t": sin[N_META:].T, "cos_r": cos_r[N_META:], "sin_r": sin_r[N_META:]}

    meta = meta_tokens.astype(x.dtype)
    meta_b = jnp.broadcast_to(meta[None], (V7X_SUBLANES, N_META, D_MODEL))
    zero_hist = jnp.zeros((nb, HIST_ROWS, rb), F32)
    zero_state = jnp.zeros((nb, V7X_SUBLANES, rb), F32)
    _, hist0, state0 = _rnn_branch(meta_b, p_rnn, zero_hist, zero_state, tt=N_META, seq_start=True)
    kmeta, vmeta = _meta_kv(meta, p_attn, cos_r[:N_META], sin_r[:N_META])

    ya, _, _ = _rnn_branch(x, p_rnn, hist0, state0, tt=32, seq_start=False)
    yb = _attn_branch(x, p_attn, tabs, kmeta, vmeta.T, tq=256)
    return _merge(x, ya, yb, p_merge, tm=256)
```

```python
import functools

import jax
import jax.numpy as jnp
from jax import lax
from jax.experimental import pallas as pl
from jax.experimental.pallas import tpu as pltpu

D_MODEL = 2048
N_META = 16
N_RNN_BLOCKS = 8
RNN_BLOCK = 256
CONV_WIDTH = 4
LRU_C = 8.0
HEAD_DIM = 64
HALF = HEAD_DIM // 2
N_Q_HEADS = 32
N_KV_HEADS = 4
GROUP = 8
D_KV = N_KV_HEADS * HEAD_DIM
BLOCK = 128
ROPE_THETA = 10000.0
NEG_INF = -1e30
LN_EPS = 1e-5
DEPTH = 1
DEEPNORM_ALPHA = (2.0 * DEPTH) ** 0.25
OFF_XR = 0
OFF_GR = D_MODEL
OFF_Q = 2 * D_MODEL
OFF_K = OFF_Q + D_MODEL
OFF_V = OFF_K + D_KV
OFF_GA = OFF_V + D_KV
OFF_G = OFF_GA + D_MODEL

V7X_SUBLANES = 8
V7X_LANES = 128
V7X_VMEM_LIMIT = 56 * 1024 * 1024

KV_SLOT = V7X_LANES
HIST_ROWS = (CONV_WIDTH - 1) * V7X_SUBLANES
MERGE_COLS = 512
MERGE_BLOCKS = D_MODEL // MERGE_COLS

BF16 = jnp.bfloat16
F32 = jnp.float32


def _layer_norm(x, g, b):
    mu = jnp.mean(x, axis=-1, keepdims=True)
    xc = x - mu
    var = jnp.mean(xc * xc, axis=-1, keepdims=True)
    return xc * lax.rsqrt(var + LN_EPS) * g + b


def _sigmoid(x):
    return 1.0 / (1.0 + jnp.exp(-x))


def _dot(a, b):
    return jnp.dot(a, b, preferred_element_type=F32)


def _dot_nt(a, b):
    return lax.dot_general(a, b, (((1,), (1,)), ((), ())), preferred_element_type=F32)


def _const_spec(shape):
    nd = len(shape)
    return pl.BlockSpec(shape, lambda *_: (0,) * nd, pipeline_mode=pl.Buffered(1))


def _rnn_kernel(x_ref, lng_ref, lnb_ref, *refs, tt, seq_start):
    wx_refs, wg_refs = refs[:N_RNN_BLOCKS], refs[N_RNN_BLOCKS:2 * N_RNN_BLOCKS]
    (bx_ref, bg_ref, cw_ref, cb_ref, wra_ref, bra_ref, wri_ref, bri_ref, lam_ref, wout_ref, perm_ref,
     permt_ref, hist0_ref, state0_ref, ya_ref, hist_out_ref, state_out_ref,
     hist_s, state_s) = refs[2 * N_RNN_BLOCKS:]
    rows = tt * V7X_SUBLANES
    nb = N_RNN_BLOCKS
    j = pl.program_id(0)

    @pl.when(j == 0)
    def _():
        hist_s[...] = hist0_ref[...]
        state_s[...] = state0_ref[...]

    x = x_ref[...].reshape(rows, D_MODEL)
    h = _layer_norm(x, lng_ref[...], lnb_ref[...]).astype(BF16)
    h = _dot(perm_ref[...], h).astype(BF16)

    def project(n):
        return _dot(h, wx_refs[n][...]) + bx_ref[n], _dot(h, wg_refs[n][...]) + bg_ref[n]

    ys = []
    nxt = project(0)
    for n in range(nb):
        xr, gr = nxt
        if n + 1 < nb:
            nxt = project(n + 1)
        xe = jnp.concatenate([hist_s[n], xr], axis=0)
        cw = cw_ref[n]
        conv = cb_ref[n] + cw[0:1] * xr
        for k in range(1, CONV_WIDTH):
            off = HIST_ROWS - k * V7X_SUBLANES
            conv = conv + cw[k:k + 1] * xe[off:off + rows]
        hist_s[n] = xe[rows:rows + HIST_ROWS]
        cb16 = conv.astype(BF16)
        gate_r = _sigmoid(_dot(cb16, wra_ref[n]) + bra_ref[n])
        gate_i = _sigmoid(_dot(cb16, wri_ref[n]) + bri_ref[n])
        lam = lam_ref[n]
        log_sig = jnp.minimum(lam, 0.0) - jnp.log1p(jnp.exp(-jnp.abs(lam)))
        a = jnp.exp((LRU_C * gate_r) * log_sig)
        mult = jnp.sqrt(1.0 - a * a)
        if seq_start:
            first = lax.broadcasted_iota(jnp.int32, (rows, RNN_BLOCK), 0) < V7X_SUBLANES
            mult = jnp.where(first & (j == 0), 1.0, mult)
        u = mult * gate_i * conv
        hcur = state_s[n]
        hs = []
        for t in range(tt):
            sl = slice(t * V7X_SUBLANES, (t + 1) * V7X_SUBLANES)
            hcur = a[sl] * hcur + u[sl]
            hs.append(hcur)
        state_s[n] = hcur
        hr = jnp.concatenate(hs, axis=0)
        ys.append((hr * (gr * _sigmoid(gr))).astype(BF16))
    y = jnp.concatenate(ys, axis=1)
    y = _dot(permt_ref[...], y).astype(BF16)
    ya = _dot(y, wout_ref[...])
    ya_ref[...] = ya.reshape(V7X_SUBLANES, tt, D_MODEL).astype(ya_ref.dtype)

    @pl.when(j == pl.num_programs(0) - 1)
    def _():
        hist_out_ref[...] = hist_s[...]
        state_out_ref[...] = state_s[...]


def _rnn_branch(x, p, hist0, state0, *, tt, seq_start):
    t_len = x.shape[1]
    nb, rb = N_RNN_BLOCKS, RNN_BLOCK
    rows = tt * V7X_SUBLANES
    r = jnp.arange(rows)
    perm = jax.nn.one_hot((r % V7X_SUBLANES) * tt + r // V7X_SUBLANES, rows, dtype=BF16)
    kern = functools.partial(_rnn_kernel, tt=tt, seq_start=seq_start)
    n_tiles = t_len // tt
    w_in_block = lambda c: pl.BlockSpec((D_MODEL, rb), lambda j: (0, c), pipeline_mode=pl.Buffered(1))
    in_specs = [
        pl.BlockSpec((V7X_SUBLANES, tt, D_MODEL), lambda j: (0, j, 0)),
        _const_spec((1, D_MODEL)), _const_spec((1, D_MODEL)),
        *[w_in_block(OFF_XR // rb + n) for n in range(nb)],
        *[w_in_block(OFF_GR // rb + n) for n in range(nb)],
        _const_spec((nb, 1, rb)), _const_spec((nb, 1, rb)),
        _const_spec((nb, CONV_WIDTH, rb)), _const_spec((nb, 1, rb)),
        _const_spec((nb, rb, rb)), _const_spec((nb, 1, rb)),
        _const_spec((nb, rb, rb)), _const_spec((nb, 1, rb)),
        _const_spec((nb, 1, rb)),
        _const_spec((D_MODEL, D_MODEL)),
        _const_spec((rows, rows)), _const_spec((rows, rows)),
        _const_spec((nb, HIST_ROWS, rb)), _const_spec((nb, V7X_SUBLANES, rb)),
    ]
    out_shape = (
        jax.ShapeDtypeStruct((V7X_SUBLANES, t_len, D_MODEL), BF16),
        jax.ShapeDtypeStruct((nb, HIST_ROWS, rb), F32),
        jax.ShapeDtypeStruct((nb, V7X_SUBLANES, rb), F32),
    )
    out_specs = (
        pl.BlockSpec((V7X_SUBLANES, tt, D_MODEL), lambda j: (0, j, 0)),
        pl.BlockSpec((nb, HIST_ROWS, rb), lambda j: (0, 0, 0)),
        pl.BlockSpec((nb, V7X_SUBLANES, rb), lambda j: (0, 0, 0)),
    )
    return pl.pallas_call(
        kern,
        grid=(n_tiles,),
        in_specs=in_specs,
        out_specs=out_specs,
        out_shape=out_shape,
        scratch_shapes=[pltpu.VMEM((nb, HIST_ROWS, rb), F32), pltpu.VMEM((nb, V7X_SUBLANES, rb), F32)],
        compiler_params=pltpu.CompilerParams(dimension_semantics=("arbitrary",),
                                             vmem_limit_bytes=V7X_VMEM_LIMIT),
        name="rnn_branch_start" if seq_start else "rnn_branch",
    )(x, p["ln_g"], p["ln_b"], *([p["w_in"]] * (2 * nb)), p["bx"], p["bg"], p["cw"], p["cb"],
      p["wra"], p["bra"], p["wri"], p["bri"], p["lam"], p["w_rnn_out"], perm, perm.T, hist0, state0)


def _rope_rows(k, cos, sin_signed):
    lane = lax.broadcasted_iota(jnp.int32, k.shape, 1)
    partner = jnp.where((lane % HEAD_DIM) < HALF,
                        pltpu.roll(k, V7X_LANES - HALF, axis=1),
                        pltpu.roll(k, HALF, axis=1))
    return k * cos + partner * sin_signed


def _meta_kv_kernel(m_ref, lng_ref, lnb_ref, wk_ref, bk_ref, wv_ref, bv_ref, cos_ref, sin_ref,
                    k_ref, v_ref):
    h = _layer_norm(m_ref[...], lng_ref[...], lnb_ref[...]).astype(BF16)
    k = _dot(h, wk_ref[...]) + bk_ref[...]
    v = _dot(h, wv_ref[...]) + bv_ref[...]
    for g in range(N_KV_HEADS):
        sl = slice(g * KV_SLOT, (g + 1) * KV_SLOT)
        k_ref[:, sl] = _rope_rows(k[:, sl], cos_ref[...], sin_ref[...]).astype(BF16)
    v_ref[...] = v.astype(BF16)


def _meta_kv(meta, p, cos_rows, sin_rows):
    return pl.pallas_call(
        _meta_kv_kernel,
        out_shape=(jax.ShapeDtypeStruct((N_META, N_KV_HEADS * KV_SLOT), BF16),
                   jax.ShapeDtypeStruct((N_META, D_KV), BF16)),
        name="meta_kv",
    )(meta, p["ln_g"], p["ln_b"], p["wk"], p["bk"], p["wv"], p["bv"], cos_rows, sin_rows)


def _attn_kernel(x_ref, lng_ref, lnb_ref, wqt_ref, bqt_ref, wk_ref, bk_ref, wvt_ref, bvt_ref,
                 wgat_ref, bgat_ref, wout_ref, cost_ref, sint_ref, cosr_ref, sinr_ref,
                 kmeta_ref, vmetat_ref, sink_ref, yb_ref,
                 qt_s, k_s, vt_s, gat_s, kprev_s, vprev_s, ot_s, act_s, *, tq, n_t):
    nblk = tq // BLOCK
    rows_g = GROUP * HEAD_DIM
    j = pl.program_id(0)

    @pl.when(j == 0)
    def _():
        for ref in (qt_s, k_s, vt_s, gat_s, kprev_s, vprev_s, act_s):
            ref[...] = jnp.zeros_like(ref)

    h = _layer_norm(x_ref[0], lng_ref[...], lnb_ref[...]).astype(BF16)

    act_prev = act_s[...].astype(BF16)
    out_cols = D_MODEL // (N_KV_HEADS * nblk)

    def project_out(c):
        cs = slice(c * out_cols, (c + 1) * out_cols)
        yb_ref[0, :, cs] = _dot(act_prev, wout_ref[:, cs]).astype(yb_ref.dtype)

    def lanes_bias(b):
        return jnp.concatenate([b] * (tq // V7X_LANES), axis=1)

    scale = HEAD_DIM ** -0.5
    cos = cost_ref[...] * scale
    sin = sint_ref[...] * scale

    def project_q(g):
        rs = slice(g * rows_g, (g + 1) * rows_g)
        qt = _dot_nt(wqt_ref[rs, :], h) + lanes_bias(bqt_ref[rs, :])
        out = []
        for hh in range(GROUP):
            r0 = hh * HEAD_DIM
            q1, q2 = qt[r0:r0 + HALF], qt[r0 + HALF:r0 + HEAD_DIM]
            out.append((q1 * cos - q2 * sin, q2 * cos + q1 * sin))
        return out

    def store_q(g, roped):
        for hh, (lo, hi) in enumerate(roped):
            r0 = g * rows_g + hh * HEAD_DIM
            qt_s[r0:r0 + HALF, :] = lo
            qt_s[r0 + HALF:r0 + HEAD_DIM, :] = hi

    def project_gate(g):
        rs = slice(g * rows_g, (g + 1) * rows_g)
        gat = _dot_nt(wgat_ref[rs, :], h) + lanes_bias(bgat_ref[rs, :])
        return gat * _sigmoid(gat)

    i = jnp.maximum(j - 1, 0) % n_t
    k = k_s[...].astype(BF16)
    vt = vt_s[...].astype(BF16)
    k_carry, vt_carry = kprev_s[...].astype(BF16), vprev_s[...].astype(BF16)
    kmeta = kmeta_ref[...]
    vmetat = vmetat_ref[...]

    key_row = lax.broadcasted_iota(jnp.int32, (BLOCK, GROUP * BLOCK), 0)
    qry_col = lax.broadcasted_iota(jnp.int32, (BLOCK, GROUP * BLOCK), 1) % BLOCK
    cur_ok = key_row <= qry_col
    prev_in_window = key_row > qry_col
    first_prev_ok = prev_in_window & (i > 0)

    def scores(g, jb):
        tok = slice(jb * BLOCK, (jb + 1) * BLOCK)
        ks = slice(g * KV_SLOT, g * KV_SLOT + HEAD_DIM)
        if jb == 0:
            k_prev, prev_ok = k_carry, first_prev_ok
        else:
            k_prev, prev_ok = k[(jb - 1) * BLOCK:jb * BLOCK], prev_in_window
        q_g = jnp.concatenate(
            [qt_s[(g * GROUP + hh) * HEAD_DIM:(g * GROUP + hh + 1) * HEAD_DIM, tok].astype(BF16)
             for hh in range(GROUP)], axis=1)
        s_prev = jnp.where(prev_ok, _dot(k_prev[:, ks], q_g), NEG_INF)
        s_cm = _dot(jnp.concatenate([k[tok, ks], kmeta[:, ks]], axis=0), q_g)
        s_cur = jnp.where(cur_ok, s_cm[:BLOCK], NEG_INF)
        return s_prev, s_cur, s_cm[BLOCK:]

    def finish(g, jb, s):
        s_prev, s_cur, s_meta = s
        tok = slice(jb * BLOCK, (jb + 1) * BLOCK)
        vs = slice(g * HEAD_DIM, (g + 1) * HEAD_DIM)
        vt_prev = vt_carry if jb == 0 else vt[:, (jb - 1) * BLOCK:jb * BLOCK]
        sink = sink_ref[g]
        m = jnp.maximum(
            jnp.maximum(jnp.max(s_prev, axis=0, keepdims=True), jnp.max(s_cur, axis=0, keepdims=True)),
            jnp.maximum(jnp.max(s_meta, axis=0, keepdims=True), sink))
        p_prev = jnp.exp(s_prev - m)
        p_cur = jnp.exp(s_cur - m)
        p_meta = jnp.exp(s_meta - m)
        denom = (jnp.sum(p_prev, axis=0, keepdims=True) + jnp.sum(p_cur, axis=0, keepdims=True)
                 + jnp.sum(p_meta, axis=0, keepdims=True) + jnp.exp(sink - m))
        p_cm = jnp.concatenate([p_cur, p_meta], axis=0).astype(BF16)
        v_cm = jnp.concatenate([vt[vs, tok], vmetat[vs]], axis=1)
        o = _dot(vt_prev[vs], p_prev.astype(BF16)) + _dot(v_cm, p_cm)
        o = o * (1.0 / denom)
        for hh in range(GROUP):
            r0 = (g * GROUP + hh) * HEAD_DIM
            ot_s[r0:r0 + HEAD_DIM, tok] = o[:, hh * BLOCK:(hh + 1) * BLOCK]

    gates = [None] * N_KV_HEADS
    held_q = None
    for g in range(N_KV_HEADS):
        pg = (g - 1) % N_KV_HEADS
        for jb in range(nblk):
            s = scores(g, jb)
            if jb == 0:
                roped = project_q(pg)
                if pg < g:
                    store_q(pg, roped)
                else:
                    held_q = roped
            elif jb == 1:
                gates[pg] = project_gate(pg)
            project_out(g * nblk + jb)
            finish(g, jb, s)
    store_q(N_KV_HEADS - 1, held_q)

    k_new = _dot(h, wk_ref[...]) + bk_ref[...]
    vt_new = _dot_nt(wvt_ref[...], h) + lanes_bias(bvt_ref[...])

    act_s[...] = (ot_s[...] * gat_s[...]).T

    kprev_s[...] = k_s[tq - BLOCK:, :]
    vprev_s[...] = vt_s[:, tq - BLOCK:]
    for g in range(N_KV_HEADS):
        sl = slice(g * KV_SLOT, (g + 1) * KV_SLOT)
        k_s[:, sl] = _rope_rows(k_new[:, sl], cosr_ref[...], sinr_ref[...])
        gat_s[g * rows_g:(g + 1) * rows_g, :] = gates[g]
    vt_s[...] = vt_new


def _attn_branch(x, p, tabs, kmeta, vmetat, *, tq):
    b, t_len, _ = x.shape
    n_t = t_len // tq
    n_tiles = b * n_t
    kern = functools.partial(_attn_kernel, tq=tq, n_t=n_t)
    proj_tile = lambda j: jnp.minimum(j, n_tiles - 1)
    out_tile = lambda j: jnp.maximum(j - 2, 0)
    in_specs = [
        pl.BlockSpec((1, tq, D_MODEL), lambda j: (proj_tile(j) // n_t, proj_tile(j) % n_t, 0)),
        _const_spec((1, D_MODEL)), _const_spec((1, D_MODEL)),
        _const_spec((D_MODEL, D_MODEL)), _const_spec((D_MODEL, V7X_LANES)),
        _const_spec((D_MODEL, N_KV_HEADS * KV_SLOT)), _const_spec((1, N_KV_HEADS * KV_SLOT)),
        _const_spec((D_KV, D_MODEL)), _const_spec((D_KV, V7X_LANES)),
        _const_spec((D_MODEL, D_MODEL)), _const_spec((D_MODEL, V7X_LANES)),
        _const_spec((D_MODEL, D_MODEL)),
        pl.BlockSpec((HALF, tq), lambda j: (0, proj_tile(j) % n_t)),
        pl.BlockSpec((HALF, tq), lambda j: (0, proj_tile(j) % n_t)),
        pl.BlockSpec((tq, KV_SLOT), lambda j: (proj_tile(j) % n_t, 0)),
        pl.BlockSpec((tq, KV_SLOT), lambda j: (proj_tile(j) % n_t, 0)),
        _const_spec((N_META, N_KV_HEADS * KV_SLOT)), _const_spec((D_KV, N_META)),
        _const_spec((N_KV_HEADS, 1, GROUP * BLOCK)),
    ]
    return pl.pallas_call(
        kern,
        grid=(n_tiles + 2,),
        in_specs=in_specs,
        out_specs=pl.BlockSpec((1, tq, D_MODEL), lambda j: (out_tile(j) // n_t, out_tile(j) % n_t, 0)),
        out_shape=jax.ShapeDtypeStruct((b, t_len, D_MODEL), BF16),
        scratch_shapes=[pltpu.VMEM((D_MODEL, tq), F32),
                        pltpu.VMEM((tq, N_KV_HEADS * KV_SLOT), F32),
                        pltpu.VMEM((D_KV, tq), F32),
                        pltpu.VMEM((D_MODEL, tq), F32),
                        pltpu.VMEM((BLOCK, N_KV_HEADS * KV_SLOT), F32),
                        pltpu.VMEM((D_KV, BLOCK), F32),
                        pltpu.VMEM((D_MODEL, tq), F32),
                        pltpu.VMEM((tq, D_MODEL), F32)],
        compiler_params=pltpu.CompilerParams(dimension_semantics=("arbitrary",),
                                             vmem_limit_bytes=V7X_VMEM_LIMIT),
        name="attn_branch",
    )(x, p["ln_g"], p["ln_b"], p["wqt"], p["bqt"], p["wk"], p["bk"], p["wvt"], p["bvt"],
      p["wgat"], p["bgat"], p["w_attn_out"], tabs["cos_t"], tabs["sin_t"], tabs["cos_r"], tabs["sin_r"],
      kmeta, vmetat, p["sink_rows"])


def _merge_kernel(x_ref, ya_ref, yb_ref, lng_ref, lnb_ref, *refs):
    wga_refs, wgb_refs = refs[:MERGE_BLOCKS], refs[MERGE_BLOCKS:2 * MERGE_BLOCKS]
    bmg_ref, wo_ref, bo_ref, g2_ref, b2_ref, out_ref = refs[2 * MERGE_BLOCKS:]
    h32 = _layer_norm(x_ref[0], lng_ref[...], lnb_ref[...])
    h = h32.astype(BF16)
    mixed = []
    for c in range(MERGE_BLOCKS):
        cs = slice(c * MERGE_COLS, (c + 1) * MERGE_COLS)
        gs = slice(D_MODEL + c * MERGE_COLS, D_MODEL + (c + 1) * MERGE_COLS)
        ga = _sigmoid(_dot(h, wga_refs[c][...]) + bmg_ref[:, cs])
        gb = _sigmoid(_dot(h, wgb_refs[c][...]) + bmg_ref[:, gs])
        mixed.append(ga * ya_ref[0, :, cs].astype(F32) + gb * yb_ref[0, :, cs].astype(F32))
    mixed = jnp.concatenate(mixed, axis=1)
    out = _dot(mixed.astype(BF16), wo_ref[...]) + bo_ref[...]
    out_ref[0] = _layer_norm(DEEPNORM_ALPHA * h32 + out, g2_ref[...], b2_ref[...])


def _merge(x, ya, yb, p, *, tm):
    b, t_len, _ = x.shape
    tile = pl.BlockSpec((1, tm, D_MODEL), lambda bi, i: (bi, i, 0))
    w_in_block = lambda c: pl.BlockSpec((D_MODEL, MERGE_COLS), lambda bi, i: (0, c), pipeline_mode=pl.Buffered(1))
    in_specs = [tile, tile, tile,
                _const_spec((1, D_MODEL)), _const_spec((1, D_MODEL)),
                *[w_in_block(OFF_G // MERGE_COLS + c) for c in range(MERGE_BLOCKS)],
                *[w_in_block((OFF_G + D_MODEL) // MERGE_COLS + c) for c in range(MERGE_BLOCKS)],
                _const_spec((1, 2 * D_MODEL)),
                _const_spec((D_MODEL, D_MODEL)), _const_spec((1, D_MODEL)),
                _const_spec((1, D_MODEL)), _const_spec((1, D_MODEL))]
    return pl.pallas_call(
        _merge_kernel,
        grid=(b, t_len // tm),
        in_specs=in_specs,
        out_specs=tile,
        out_shape=jax.ShapeDtypeStruct((b, t_len, D_MODEL), F32),
        compiler_params=pltpu.CompilerParams(dimension_semantics=("arbitrary", "arbitrary"),
                                             vmem_limit_bytes=V7X_VMEM_LIMIT),
        name="merge",
    )(x, ya, yb, p["ln_g"], p["ln_b"], *([p["w_in"]] * (2 * MERGE_BLOCKS)), p["bmg"], p["w_o"], p["b_o"],
      p["ln2_g"], p["ln2_b"])


def _rope_tables(t_total):
    inv = ROPE_THETA ** (-jnp.arange(HALF, dtype=F32) / HALF)
    ang = jnp.arange(t_total, dtype=F32)[:, None] * inv[None, :]
    cos, sin = jnp.cos(ang), jnp.sin(ang)
    zeros = jnp.zeros_like(cos)
    cos_r = jnp.concatenate([cos, cos, zeros, zeros], axis=1)
    sin_r = jnp.concatenate([-sin, sin, zeros, zeros], axis=1)
    return cos, sin, cos_r, sin_r


def _slotted(w):
    lead = w.shape[:-1]
    w4 = w.reshape(lead + (N_KV_HEADS, HEAD_DIM))
    w4 = jnp.concatenate([w4, jnp.zeros_like(w4)], axis=-1)
    return w4.reshape(lead + (N_KV_HEADS * KV_SLOT,))


def kernel(x, meta_tokens, ln_emb_g, ln_emb_b, w_in, b_in, conv_w, conv_b, w_ra, b_ra, w_ri, b_ri,
           lru_lambda, sinks, w_rnn_out, w_attn_out, w_o, b_o, ln_g, ln_b):
    b, seq, _ = x.shape
    assert b == V7X_SUBLANES and w_in.shape[0] == DEPTH
    nb, rb = N_RNN_BLOCKS, RNN_BLOCK
    w = w_in[0].astype(BF16)
    bi = b_in[0]
    row = lambda v: v.reshape(1, -1)
    blocks = lambda v: v.reshape(nb, 1, rb)
    lanes = lambda v: jnp.broadcast_to(v[:, None], (v.shape[0], V7X_LANES))

    common = {"ln_g": row(ln_emb_g), "ln_b": row(ln_emb_b)}
    p_rnn = dict(common, w_in=w,
                 bx=blocks(bi[:OFF_GR]), bg=blocks(bi[OFF_GR:OFF_Q]),
                 cw=conv_w[0].reshape(CONV_WIDTH, nb, rb).transpose(1, 0, 2), cb=blocks(conv_b[0]),
                 wra=w_ra[0].astype(BF16), bra=blocks(b_ra[0]),
                 wri=w_ri[0].astype(BF16), bri=blocks(b_ri[0]),
                 lam=blocks(lru_lambda[0]), w_rnn_out=w_rnn_out[0].astype(BF16))
    p_attn = dict(common,
                  wqt=w[:, OFF_Q:OFF_K].T, bqt=lanes(bi[OFF_Q:OFF_K]),
                  wk=_slotted(w[:, OFF_K:OFF_V]), bk=row(_slotted(bi[OFF_K:OFF_V])),
                  wvt=w[:, OFF_V:OFF_GA].T, bvt=lanes(bi[OFF_V:OFF_GA]),
                  wgat=w[:, OFF_GA:OFF_G].T, bgat=lanes(bi[OFF_GA:OFF_G]),
                  wv=w[:, OFF_V:OFF_GA], bv=row(bi[OFF_V:OFF_GA]),
                  w_attn_out=w_attn_out[0].astype(BF16),
                  sink_rows=jnp.repeat(sinks[0].astype(F32), BLOCK).reshape(N_KV_HEADS, 1, GROUP * BLOCK))
    p_merge = dict(common, w_in=w, bmg=row(bi[OFF_G:]),
                   w_o=w_o[0].astype(BF16), b_o=row(b_o[0]), ln2_g=row(ln_g[0]), ln2_b=row(ln_b[0]))

    cos, sin, cos_r, sin_r = _rope_tables(N_META + seq)
    tabs = {"cos_t": cos[N_META:].T, "sin_t": sin[N_META:].T, "cos_r": cos_r[N_META:], "sin_r": sin_r[N_META:]}

    meta = meta_tokens.astype(x.dtype)
    meta_b = jnp.broadcast_to(meta[None], (V7X_SUBLANES, N_META, D_MODEL))
    zero_hist = jnp.zeros((nb, HIST_ROWS, rb), F32)
    zero_state = jnp.zeros((nb, V7X_SUBLANES, rb), F32)
    _, hist0, state0 = _rnn_branch(meta_b, p_rnn, zero_hist, zero_state, tt=N_META, seq_start=True)
    kmeta, vmeta = _meta_kv(meta, p_attn, cos_r[:N_META], sin_r[:N_META])

    ya, _, _ = _rnn_branch(x, p_rnn, hist0, state0, tt=32, seq_start=False)
    yb = _attn_branch(x, p_attn, tabs, kmeta, vmeta.T, tq=256)
    return _merge(x, ya, yb, p_merge, tm=256)
```

```python
import functools

import jax
import jax.numpy as jnp
from jax import lax
from jax.experimental import pallas as pl
from jax.experimental.pallas import tpu as pltpu

D_MODEL = 2048
N_META = 16
N_RNN_BLOCKS = 8
RNN_BLOCK = 256
CONV_WIDTH = 4
LRU_C = 8.0
HEAD_DIM = 64
HALF = HEAD_DIM // 2
N_Q_HEADS = 32
N_KV_HEADS = 4
GROUP = 8
D_KV = N_KV_HEADS * HEAD_DIM
BLOCK = 128
ROPE_THETA = 10000.0
NEG_INF = -1e30
LN_EPS = 1e-5
DEPTH = 1
DEEPNORM_ALPHA = (2.0 * DEPTH) ** 0.25
OFF_XR = 0
OFF_GR = D_MODEL
OFF_Q = 2 * D_MODEL
OFF_K = OFF_Q + D_MODEL
OFF_V = OFF_K + D_KV
OFF_GA = OFF_V + D_KV
OFF_G = OFF_GA + D_MODEL

V7X_SUBLANES = 8
V7X_LANES = 128
V7X_VMEM_LIMIT = 56 * 1024 * 1024

KV_SLOT = V7X_LANES
HIST_ROWS = (CONV_WIDTH - 1) * V7X_SUBLANES
MERGE_COLS = 512
MERGE_BLOCKS = D_MODEL // MERGE_COLS

BF16 = jnp.bfloat16
F32 = jnp.float32


def _layer_norm(x, g, b):
    mu = jnp.mean(x, axis=-1, keepdims=True)
    xc = x - mu
    var = jnp.mean(xc * xc, axis=-1, keepdims=True)
    return xc * lax.rsqrt(var + LN_EPS) * g + b


def _sigmoid(x):
    return 1.0 / (1.0 + jnp.exp(-x))


def _dot(a, b):
    return jnp.dot(a, b, preferred_element_type=F32)


def _dot_nt(a, b):
    return lax.dot_general(a, b, (((1,), (1,)), ((), ())), preferred_element_type=F32)


def _const_spec(shape):
    nd = len(shape)
    return pl.BlockSpec(shape, lambda *_: (0,) * nd, pipeline_mode=pl.Buffered(1))


def _rnn_kernel(x_ref, lng_ref, lnb_ref, *refs, tt, seq_start, normalised_input):
    wx_refs, wg_refs = refs[:N_RNN_BLOCKS], refs[N_RNN_BLOCKS:2 * N_RNN_BLOCKS]
    (bx_ref, bg_ref, cw_ref, cb_ref, wra_ref, bra_ref, wri_ref, bri_ref, lam_ref, wout_ref, perm_ref,
     permt_ref, hist0_ref, state0_ref, ya_ref, hist_out_ref, state_out_ref,
     hist_s, state_s) = refs[2 * N_RNN_BLOCKS:]
    rows = tt * V7X_SUBLANES
    nb = N_RNN_BLOCKS
    j = pl.program_id(0)

    @pl.when(j == 0)
    def _():
        hist_s[...] = hist0_ref[...]
        state_s[...] = state0_ref[...]

    h = x_ref[...].reshape(rows, D_MODEL)
    if not normalised_input:
        h = _layer_norm(h, lng_ref[...], lnb_ref[...]).astype(BF16)
    h = _dot(perm_ref[...], h).astype(BF16)

    def project(n):
        return _dot(h, wx_refs[n][...]) + bx_ref[n], _dot(h, wg_refs[n][...]) + bg_ref[n]

    ys = []
    nxt = project(0)
    for n in range(nb):
        xr, gr = nxt
        if n + 1 < nb:
            nxt = project(n + 1)
        xe = jnp.concatenate([hist_s[n], xr], axis=0)
        cw = cw_ref[n]
        conv = cb_ref[n] + cw[0:1] * xr
        for k in range(1, CONV_WIDTH):
            off = HIST_ROWS - k * V7X_SUBLANES
            conv = conv + cw[k:k + 1] * xe[off:off + rows]
        hist_s[n] = xe[rows:rows + HIST_ROWS]
        cb16 = conv.astype(BF16)
        gate_r = _sigmoid(_dot(cb16, wra_ref[n]) + bra_ref[n])
        gate_i = _sigmoid(_dot(cb16, wri_ref[n]) + bri_ref[n])
        lam = lam_ref[n]
        log_sig = jnp.minimum(lam, 0.0) - jnp.log1p(jnp.exp(-jnp.abs(lam)))
        a = jnp.exp((LRU_C * gate_r) * log_sig)
        mult = jnp.sqrt(1.0 - a * a)
        if seq_start:
            first = lax.broadcasted_iota(jnp.int32, (rows, RNN_BLOCK), 0) < V7X_SUBLANES
            mult = jnp.where(first & (j == 0), 1.0, mult)
        u = mult * gate_i * conv
        hcur = state_s[n]
        hs = []
        for t in range(tt):
            sl = slice(t * V7X_SUBLANES, (t + 1) * V7X_SUBLANES)
            hcur = a[sl] * hcur + u[sl]
            hs.append(hcur)
        state_s[n] = hcur
        hr = jnp.concatenate(hs, axis=0)
        ys.append((hr * (gr * _sigmoid(gr))).astype(BF16))
    y = jnp.concatenate(ys, axis=1)
    y = _dot(permt_ref[...], y).astype(BF16)
    ya = _dot(y, wout_ref[...])
    ya_ref[...] = ya.reshape(V7X_SUBLANES, tt, D_MODEL).astype(ya_ref.dtype)

    @pl.when(j == pl.num_programs(0) - 1)
    def _():
        hist_out_ref[...] = hist_s[...]
        state_out_ref[...] = state_s[...]


def _rnn_branch(x, p, hist0, state0, *, tt, seq_start):
    t_len = x.shape[1]
    nb, rb = N_RNN_BLOCKS, RNN_BLOCK
    rows = tt * V7X_SUBLANES
    r = jnp.arange(rows)
    perm = jax.nn.one_hot((r % V7X_SUBLANES) * tt + r // V7X_SUBLANES, rows, dtype=BF16)
    assert x.dtype in (F32, BF16)
    kern = functools.partial(_rnn_kernel, tt=tt, seq_start=seq_start, normalised_input=x.dtype == BF16)
    n_tiles = t_len // tt
    w_in_block = lambda c: pl.BlockSpec((D_MODEL, rb), lambda j: (0, c), pipeline_mode=pl.Buffered(1))
    in_specs = [
        pl.BlockSpec((V7X_SUBLANES, tt, D_MODEL), lambda j: (0, j, 0)),
        _const_spec((1, D_MODEL)), _const_spec((1, D_MODEL)),
        *[w_in_block(OFF_XR // rb + n) for n in range(nb)],
        *[w_in_block(OFF_GR // rb + n) for n in range(nb)],
        _const_spec((nb, 1, rb)), _const_spec((nb, 1, rb)),
        _const_spec((nb, CONV_WIDTH, rb)), _const_spec((nb, 1, rb)),
        _const_spec((nb, rb, rb)), _const_spec((nb, 1, rb)),
        _const_spec((nb, rb, rb)), _const_spec((nb, 1, rb)),
        _const_spec((nb, 1, rb)),
        _const_spec((D_MODEL, D_MODEL)),
        _const_spec((rows, rows)), _const_spec((rows, rows)),
        _const_spec((nb, HIST_ROWS, rb)), _const_spec((nb, V7X_SUBLANES, rb)),
    ]
    out_shape = (
        jax.ShapeDtypeStruct((V7X_SUBLANES, t_len, D_MODEL), BF16),
        jax.ShapeDtypeStruct((nb, HIST_ROWS, rb), F32),
        jax.ShapeDtypeStruct((nb, V7X_SUBLANES, rb), F32),
    )
    out_specs = (
        pl.BlockSpec((V7X_SUBLANES, tt, D_MODEL), lambda j: (0, j, 0)),
        pl.BlockSpec((nb, HIST_ROWS, rb), lambda j: (0, 0, 0)),
        pl.BlockSpec((nb, V7X_SUBLANES, rb), lambda j: (0, 0, 0)),
    )
    return pl.pallas_call(
        kern,
        grid=(n_tiles,),
        in_specs=in_specs,
        out_specs=out_specs,
        out_shape=out_shape,
        scratch_shapes=[pltpu.VMEM((nb, HIST_ROWS, rb), F32), pltpu.VMEM((nb, V7X_SUBLANES, rb), F32)],
        compiler_params=pltpu.CompilerParams(dimension_semantics=("arbitrary",),
                                             vmem_limit_bytes=V7X_VMEM_LIMIT),
        name="rnn_branch_start" if seq_start else "rnn_branch",
    )(x, p["ln_g"], p["ln_b"], *([p["w_in"]] * (2 * nb)), p["bx"], p["bg"], p["cw"], p["cb"],
      p["wra"], p["bra"], p["wri"], p["bri"], p["lam"], p["w_rnn_out"], perm, perm.T, hist0, state0)


def _rope_rows(k, cos, sin_signed):
    lane = lax.broadcasted_iota(jnp.int32, k.shape, 1)
    partner = jnp.where((lane % HEAD_DIM) < HALF,
                        pltpu.roll(k, V7X_LANES - HALF, axis=1),
                        pltpu.roll(k, HALF, axis=1))
    return k * cos + partner * sin_signed


def _meta_kv_kernel(m_ref, lng_ref, lnb_ref, wk_ref, bk_ref, wv_ref, bv_ref, cos_ref, sin_ref,
                    k_ref, v_ref):
    h = _layer_norm(m_ref[...], lng_ref[...], lnb_ref[...]).astype(BF16)
    k = _dot(h, wk_ref[...]) + bk_ref[...]
    v = _dot(h, wv_ref[...]) + bv_ref[...]
    for g in range(N_KV_HEADS):
        sl = slice(g * KV_SLOT, (g + 1) * KV_SLOT)
        k_ref[:, sl] = _rope_rows(k[:, sl], cos_ref[...], sin_ref[...]).astype(BF16)
    v_ref[...] = v.astype(BF16)


def _meta_kv(meta, p, cos_rows, sin_rows):
    return pl.pallas_call(
        _meta_kv_kernel,
        out_shape=(jax.ShapeDtypeStruct((N_META, N_KV_HEADS * KV_SLOT), BF16),
                   jax.ShapeDtypeStruct((N_META, D_KV), BF16)),
        name="meta_kv",
    )(meta, p["ln_g"], p["ln_b"], p["wk"], p["bk"], p["wv"], p["bv"], cos_rows, sin_rows)


def _attn_kernel(x_ref, lng_ref, lnb_ref, wqt_ref, bqt_ref, wk_ref, bk_ref, wvt_ref, bvt_ref,
                 wgat_ref, bgat_ref, wout_ref, cost_ref, sint_ref, cosr_ref, sinr_ref,
                 kmeta_ref, vmetat_ref, sink_ref, yb_ref, h_ref,
                 qt_s, k_s, vt_s, gat_s, kprev_s, vprev_s, ot_s, act_s, *, tq, n_t):
    nblk = tq // BLOCK
    rows_g = GROUP * HEAD_DIM
    j = pl.program_id(0)

    @pl.when(j == 0)
    def _():
        for ref in (qt_s, k_s, vt_s, gat_s, kprev_s, vprev_s, act_s):
            ref[...] = jnp.zeros_like(ref)

    h = _layer_norm(x_ref[0], lng_ref[...], lnb_ref[...]).astype(BF16)
    h_ref[0] = h

    act_prev = act_s[...].astype(BF16)
    out_cols = D_MODEL // (N_KV_HEADS * nblk)

    def project_out(c):
        cs = slice(c * out_cols, (c + 1) * out_cols)
        yb_ref[0, :, cs] = _dot(act_prev, wout_ref[:, cs]).astype(yb_ref.dtype)

    def lanes_bias(b):
        return jnp.concatenate([b] * (tq // V7X_LANES), axis=1)

    scale = HEAD_DIM ** -0.5
    cos = cost_ref[...] * scale
    sin = sint_ref[...] * scale

    def project_q(g):
        rs = slice(g * rows_g, (g + 1) * rows_g)
        qt = _dot_nt(wqt_ref[rs, :], h) + lanes_bias(bqt_ref[rs, :])
        out = []
        for hh in range(GROUP):
            r0 = hh * HEAD_DIM
            q1, q2 = qt[r0:r0 + HALF], qt[r0 + HALF:r0 + HEAD_DIM]
            out.append((q1 * cos - q2 * sin, q2 * cos + q1 * sin))
        return out

    def store_q(g, roped):
        for hh, (lo, hi) in enumerate(roped):
            r0 = g * rows_g + hh * HEAD_DIM
            qt_s[r0:r0 + HALF, :] = lo
            qt_s[r0 + HALF:r0 + HEAD_DIM, :] = hi

    def project_gate(g):
        rs = slice(g * rows_g, (g + 1) * rows_g)
        gat = _dot_nt(wgat_ref[rs, :], h) + lanes_bias(bgat_ref[rs, :])
        return gat * _sigmoid(gat)

    i = jnp.maximum(j - 1, 0) % n_t
    k = k_s[...].astype(BF16)
    vt = vt_s[...].astype(BF16)
    k_carry, vt_carry = kprev_s[...].astype(BF16), vprev_s[...].astype(BF16)
    kmeta = kmeta_ref[...]
    vmetat = vmetat_ref[...]

    key_row = lax.broadcasted_iota(jnp.int32, (BLOCK, GROUP * BLOCK), 0)
    qry_col = lax.broadcasted_iota(jnp.int32, (BLOCK, GROUP * BLOCK), 1) % BLOCK
    cur_ok = key_row <= qry_col
    prev_in_window = key_row > qry_col
    first_prev_ok = prev_in_window & (i > 0)

    def scores(g, jb):
        tok = slice(jb * BLOCK, (jb + 1) * BLOCK)
        ks = slice(g * KV_SLOT, g * KV_SLOT + HEAD_DIM)
        if jb == 0:
            k_prev, prev_ok = k_carry, first_prev_ok
        else:
            k_prev, prev_ok = k[(jb - 1) * BLOCK:jb * BLOCK], prev_in_window
        q_g = jnp.concatenate(
            [qt_s[(g * GROUP + hh) * HEAD_DIM:(g * GROUP + hh + 1) * HEAD_DIM, tok].astype(BF16)
             for hh in range(GROUP)], axis=1)
        s_prev = jnp.where(prev_ok, _dot(k_prev[:, ks], q_g), NEG_INF)
        s_cm = _dot(jnp.concatenate([k[tok, ks], kmeta[:, ks]], axis=0), q_g)
        s_cur = jnp.where(cur_ok, s_cm[:BLOCK], NEG_INF)
        return s_prev, s_cur, s_cm[BLOCK:]

    def finish(g, jb, s):
        s_prev, s_cur, s_meta = s
        tok = slice(jb * BLOCK, (jb + 1) * BLOCK)
        vs = slice(g * HEAD_DIM, (g + 1) * HEAD_DIM)
        vt_prev = vt_carry if jb == 0 else vt[:, (jb - 1) * BLOCK:jb * BLOCK]
        sink = sink_ref[g]
        m = jnp.maximum(
            jnp.maximum(jnp.max(s_prev, axis=0, keepdims=True), jnp.max(s_cur, axis=0, keepdims=True)),
            jnp.maximum(jnp.max(s_meta, axis=0, keepdims=True), sink))
        p_prev = jnp.exp(s_prev - m)
        p_cur = jnp.exp(s_cur - m)
        p_meta = jnp.exp(s_meta - m)
        denom = (jnp.sum(p_prev, axis=0, keepdims=True) + jnp.sum(p_cur, axis=0, keepdims=True)
                 + jnp.sum(p_meta, axis=0, keepdims=True) + jnp.exp(sink - m))
        p_cm = jnp.concatenate([p_cur, p_meta], axis=0).astype(BF16)
        v_cm = jnp.concatenate([vt[vs, tok], vmetat[vs]], axis=1)
        o = _dot(vt_prev[vs], p_prev.astype(BF16)) + _dot(v_cm, p_cm)
        o = o * (1.0 / denom)
        for hh in range(GROUP):
            r0 = (g * GROUP + hh) * HEAD_DIM
            ot_s[r0:r0 + HEAD_DIM, tok] = o[:, hh * BLOCK:(hh + 1) * BLOCK]

    gates = [None] * N_KV_HEADS
    held_q = None
    for g in range(N_KV_HEADS):
        pg = (g - 1) % N_KV_HEADS
        for jb in range(nblk):
            s = scores(g, jb)
            if jb == 0:
                roped = project_q(pg)
                if pg < g:
                    store_q(pg, roped)
                else:
                    held_q = roped
            elif jb == 1:
                gates[pg] = project_gate(pg)
            project_out(g * nblk + jb)
            finish(g, jb, s)
    store_q(N_KV_HEADS - 1, held_q)

    k_new = _dot(h, wk_ref[...]) + bk_ref[...]
    vt_new = _dot_nt(wvt_ref[...], h) + lanes_bias(bvt_ref[...])

    act_s[...] = (ot_s[...] * gat_s[...]).T

    kprev_s[...] = k_s[tq - BLOCK:, :]
    vprev_s[...] = vt_s[:, tq - BLOCK:]
    for g in range(N_KV_HEADS):
        sl = slice(g * KV_SLOT, (g + 1) * KV_SLOT)
        k_s[:, sl] = _rope_rows(k_new[:, sl], cosr_ref[...], sinr_ref[...])
        gat_s[g * rows_g:(g + 1) * rows_g, :] = gates[g]
    vt_s[...] = vt_new


def _attn_branch(x, p, tabs, kmeta, vmetat, *, tq):
    b, t_len, _ = x.shape
    n_t = t_len // tq
    n_tiles = b * n_t
    kern = functools.partial(_attn_kernel, tq=tq, n_t=n_t)
    proj_tile = lambda j: jnp.minimum(j, n_tiles - 1)
    out_tile = lambda j: jnp.maximum(j - 2, 0)
    in_specs = [
        pl.BlockSpec((1, tq, D_MODEL), lambda j: (proj_tile(j) // n_t, proj_tile(j) % n_t, 0)),
        _const_spec((1, D_MODEL)), _const_spec((1, D_MODEL)),
        _const_spec((D_MODEL, D_MODEL)), _const_spec((D_MODEL, V7X_LANES)),
        _const_spec((D_MODEL, N_KV_HEADS * KV_SLOT)), _const_spec((1, N_KV_HEADS * KV_SLOT)),
        _const_spec((D_KV, D_MODEL)), _const_spec((D_KV, V7X_LANES)),
        _const_spec((D_MODEL, D_MODEL)), _const_spec((D_MODEL, V7X_LANES)),
        _const_spec((D_MODEL, D_MODEL)),
        pl.BlockSpec((HALF, tq), lambda j: (0, proj_tile(j) % n_t)),
        pl.BlockSpec((HALF, tq), lambda j: (0, proj_tile(j) % n_t)),
        pl.BlockSpec((tq, KV_SLOT), lambda j: (proj_tile(j) % n_t, 0)),
        pl.BlockSpec((tq, KV_SLOT), lambda j: (proj_tile(j) % n_t, 0)),
        _const_spec((N_META, N_KV_HEADS * KV_SLOT)), _const_spec((D_KV, N_META)),
        _const_spec((N_KV_HEADS, 1, GROUP * BLOCK)),
    ]
    return pl.pallas_call(
        kern,
        grid=(n_tiles + 2,),
        in_specs=in_specs,
        out_specs=(pl.BlockSpec((1, tq, D_MODEL), lambda j: (out_tile(j) // n_t, out_tile(j) % n_t, 0)),
                   pl.BlockSpec((1, tq, D_MODEL), lambda j: (proj_tile(j) // n_t, proj_tile(j) % n_t, 0))),
        out_shape=(jax.ShapeDtypeStruct((b, t_len, D_MODEL), BF16),
                   jax.ShapeDtypeStruct((b, t_len, D_MODEL), BF16)),
        scratch_shapes=[pltpu.VMEM((D_MODEL, tq), F32),
                        pltpu.VMEM((tq, N_KV_HEADS * KV_SLOT), F32),
                        pltpu.VMEM((D_KV, tq), F32),
                        pltpu.VMEM((D_MODEL, tq), F32),
                        pltpu.VMEM((BLOCK, N_KV_HEADS * KV_SLOT), F32),
                        pltpu.VMEM((D_KV, BLOCK), F32),
                        pltpu.VMEM((D_MODEL, tq), F32),
                        pltpu.VMEM((tq, D_MODEL), F32)],
        compiler_params=pltpu.CompilerParams(dimension_semantics=("arbitrary",),
                                             vmem_limit_bytes=V7X_VMEM_LIMIT),
        name="attn_branch",
    )(x, p["ln_g"], p["ln_b"], p["wqt"], p["bqt"], p["wk"], p["bk"], p["wvt"], p["bvt"],
      p["wgat"], p["bgat"], p["w_attn_out"], tabs["cos_t"], tabs["sin_t"], tabs["cos_r"], tabs["sin_r"],
      kmeta, vmetat, p["sink_rows"])


def _merge_kernel(x_ref, h_ref, ya_ref, yb_ref, lng_ref, lnb_ref, *refs):
    wga_refs, wgb_refs = refs[:MERGE_BLOCKS], refs[MERGE_BLOCKS:2 * MERGE_BLOCKS]
    bmg_ref, wo_ref, bo_ref, g2_ref, b2_ref, out_ref = refs[2 * MERGE_BLOCKS:]
    h = h_ref[0]
    mixed = []
    for c in range(MERGE_BLOCKS):
        cs = slice(c * MERGE_COLS, (c + 1) * MERGE_COLS)
        gs = slice(D_MODEL + c * MERGE_COLS, D_MODEL + (c + 1) * MERGE_COLS)
        ga = _sigmoid(_dot(h, wga_refs[c][...]) + bmg_ref[:, cs])
        gb = _sigmoid(_dot(h, wgb_refs[c][...]) + bmg_ref[:, gs])
        mixed.append(ga * ya_ref[0, :, cs].astype(F32) + gb * yb_ref[0, :, cs].astype(F32))
    mixed = jnp.concatenate(mixed, axis=1)
    out = _dot(mixed.astype(BF16), wo_ref[...]) + bo_ref[...]
    h32 = _layer_norm(x_ref[0], lng_ref[...], lnb_ref[...])
    out_ref[0] = _layer_norm(DEEPNORM_ALPHA * h32 + out, g2_ref[...], b2_ref[...])


def _merge(x, h, ya, yb, p, *, tm):
    b, t_len, _ = x.shape
    tile = pl.BlockSpec((1, tm, D_MODEL), lambda bi, i: (bi, i, 0))
    w_in_block = lambda c: pl.BlockSpec((D_MODEL, MERGE_COLS), lambda bi, i: (0, c), pipeline_mode=pl.Buffered(1))
    in_specs = [tile, tile, tile, tile,
                _const_spec((1, D_MODEL)), _const_spec((1, D_MODEL)),
                *[w_in_block(OFF_G // MERGE_COLS + c) for c in range(MERGE_BLOCKS)],
                *[w_in_block((OFF_G + D_MODEL) // MERGE_COLS + c) for c in range(MERGE_BLOCKS)],
                _const_spec((1, 2 * D_MODEL)),
                _const_spec((D_MODEL, D_MODEL)), _const_spec((1, D_MODEL)),
                _const_spec((1, D_MODEL)), _const_spec((1, D_MODEL))]
    return pl.pallas_call(
        _merge_kernel,
        grid=(b, t_len // tm),
        in_specs=in_specs,
        out_specs=tile,
        out_shape=jax.ShapeDtypeStruct((b, t_len, D_MODEL), F32),
        compiler_params=pltpu.CompilerParams(dimension_semantics=("arbitrary", "arbitrary"),
                                             vmem_limit_bytes=V7X_VMEM_LIMIT),
        name="merge",
    )(x, h, ya, yb, p["ln_g"], p["ln_b"], *([p["w_in"]] * (2 * MERGE_BLOCKS)), p["bmg"], p["w_o"], p["b_o"],
      p["ln2_g"], p["ln2_b"])


def _rope_tables(t_total):
    inv = ROPE_THETA ** (-jnp.arange(HALF, dtype=F32) / HALF)
    ang = jnp.arange(t_total, dtype=F32)[:, None] * inv[None, :]
    cos, sin = jnp.cos(ang), jnp.sin(ang)
    zeros = jnp.zeros_like(cos)
    cos_r = jnp.concatenate([cos, cos, zeros, zeros], axis=1)
    sin_r = jnp.concatenate([-sin, sin, zeros, zeros], axis=1)
    return cos, sin, cos_r, sin_r


def _slotted(w):
    lead = w.shape[:-1]
    w4 = w.reshape(lead + (N_KV_HEADS, HEAD_DIM))
    w4 = jnp.concatenate([w4, jnp.zeros_like(w4)], axis=-1)
    return w4.reshape(lead + (N_KV_HEADS * KV_SLOT,))


def kernel(x, meta_tokens, ln_emb_g, ln_emb_b, w_in, b_in, conv_w, conv_b, w_ra, b_ra, w_ri, b_ri,
           lru_lambda, sinks, w_rnn_out, w_attn_out, w_o, b_o, ln_g, ln_b):
    b, seq, _ = x.shape
    assert b == V7X_SUBLANES and w_in.shape[0] == DEPTH
    nb, rb = N_RNN_BLOCKS, RNN_BLOCK
    w = w_in[0].astype(BF16)
    bi = b_in[0]
    row = lambda v: v.reshape(1, -1)
    blocks = lambda v: v.reshape(nb, 1, rb)
    lanes = lambda v: jnp.broadcast_to(v[:, None], (v.shape[0], V7X_LANES))

    common = {"ln_g": row(ln_emb_g), "ln_b": row(ln_emb_b)}
    p_rnn = dict(common, w_in=w,
                 bx=blocks(bi[:OFF_GR]), bg=blocks(bi[OFF_GR:OFF_Q]),
                 cw=conv_w[0].reshape(CONV_WIDTH, nb, rb).transpose(1, 0, 2), cb=blocks(conv_b[0]),
                 wra=w_ra[0].astype(BF16), bra=blocks(b_ra[0]),
                 wri=w_ri[0].astype(BF16), bri=blocks(b_ri[0]),
                 lam=blocks(lru_lambda[0]), w_rnn_out=w_rnn_out[0].astype(BF16))
    p_attn = dict(common,
                  wqt=w[:, OFF_Q:OFF_K].T, bqt=lanes(bi[OFF_Q:OFF_K]),
                  wk=_slotted(w[:, OFF_K:OFF_V]), bk=row(_slotted(bi[OFF_K:OFF_V])),
                  wvt=w[:, OFF_V:OFF_GA].T, bvt=lanes(bi[OFF_V:OFF_GA]),
                  wgat=w[:, OFF_GA:OFF_G].T, bgat=lanes(bi[OFF_GA:OFF_G]),
                  wv=w[:, OFF_V:OFF_GA], bv=row(bi[OFF_V:OFF_GA]),
                  w_attn_out=w_attn_out[0].astype(BF16),
                  sink_rows=jnp.repeat(sinks[0].astype(F32), BLOCK).reshape(N_KV_HEADS, 1, GROUP * BLOCK))
    p_merge = dict(common, w_in=w, bmg=row(bi[OFF_G:]),
                   w_o=w_o[0].astype(BF16), b_o=row(b_o[0]), ln2_g=row(ln_g[0]), ln2_b=row(ln_b[0]))

    cos, sin, cos_r, sin_r = _rope_tables(N_META + seq)
    tabs = {"cos_t": cos[N_META:].T, "sin_t": sin[N_META:].T, "cos_r": cos_r[N_META:], "sin_r": sin_r[N_META:]}

    meta = meta_tokens.astype(x.dtype)
    meta_b = jnp.broadcast_to(meta[None], (V7X_SUBLANES, N_META, D_MODEL))
    zero_hist = jnp.zeros((nb, HIST_ROWS, rb), F32)
    zero_state = jnp.zeros((nb, V7X_SUBLANES, rb), F32)
    _, hist0, state0 = _rnn_branch(meta_b, p_rnn, zero_hist, zero_state, tt=N_META, seq_start=True)
    kmeta, vmeta = _meta_kv(meta, p_attn, cos_r[:N_META], sin_r[:N_META])

    yb, h = _attn_branch(x, p_attn, tabs, kmeta, vmeta.T, tq=256)
    ya, _, _ = _rnn_branch(h, p_rnn, hist0, state0, tt=32, seq_start=False)
    return _merge(x, h, ya, yb, p_merge, tm=256)
```

```python
import functools

import jax
import jax.numpy as jnp
from jax import lax
from jax.experimental import pallas as pl
from jax.experimental.pallas import tpu as pltpu

D_MODEL = 2048
N_META = 16
N_RNN_BLOCKS = 8
RNN_BLOCK = 256
CONV_WIDTH = 4
LRU_C = 8.0
HEAD_DIM = 64
HALF = HEAD_DIM // 2
N_Q_HEADS = 32
N_KV_HEADS = 4
GROUP = 8
D_KV = N_KV_HEADS * HEAD_DIM
BLOCK = 128
ROPE_THETA = 10000.0
NEG_INF = -1e30
LN_EPS = 1e-5
DEPTH = 1
DEEPNORM_ALPHA = (2.0 * DEPTH) ** 0.25
OFF_XR = 0
OFF_GR = D_MODEL
OFF_Q = 2 * D_MODEL
OFF_K = OFF_Q + D_MODEL
OFF_V = OFF_K + D_KV
OFF_GA = OFF_V + D_KV
OFF_G = OFF_GA + D_MODEL

V7X_SUBLANES = 8
V7X_LANES = 128
V7X_VMEM_LIMIT = 56 * 1024 * 1024

KV_SLOT = V7X_LANES
HIST_ROWS = (CONV_WIDTH - 1) * V7X_SUBLANES
MERGE_COLS = 512
MERGE_BLOCKS = D_MODEL // MERGE_COLS

BF16 = jnp.bfloat16
F32 = jnp.float32


def _layer_norm(x, g, b):
    mu = jnp.mean(x, axis=-1, keepdims=True)
    xc = x - mu
    var = jnp.mean(xc * xc, axis=-1, keepdims=True)
    return xc * lax.rsqrt(var + LN_EPS) * g + b


def _sigmoid(x):
    return 1.0 / (1.0 + jnp.exp(-x))


def _dot(a, b):
    return jnp.dot(a, b, preferred_element_type=F32)


def _dot_nt(a, b):
    return lax.dot_general(a, b, (((1,), (1,)), ((), ())), preferred_element_type=F32)


def _const_spec(shape):
    nd = len(shape)
    return pl.BlockSpec(shape, lambda *_: (0,) * nd, pipeline_mode=pl.Buffered(1))


def _rnn_kernel(x_ref, lng_ref, lnb_ref, *refs, tt, seq_start, normalised_input):
    wx_refs, wg_refs = refs[:N_RNN_BLOCKS], refs[N_RNN_BLOCKS:2 * N_RNN_BLOCKS]
    (bx_ref, bg_ref, cw_ref, cb_ref, wra_ref, bra_ref, wri_ref, bri_ref, lam_ref, wout_ref, perm_ref,
     permt_ref, hist0_ref, state0_ref, ya_ref, hist_out_ref, state_out_ref,
     hist_s, state_s) = refs[2 * N_RNN_BLOCKS:]
    rows = tt * V7X_SUBLANES
    nb = N_RNN_BLOCKS
    j = pl.program_id(0)

    @pl.when(j == 0)
    def _():
        hist_s[...] = hist0_ref[...]
        state_s[...] = state0_ref[...]

    h = x_ref[...].reshape(rows, D_MODEL)
    if not normalised_input:
        h = _layer_norm(h, lng_ref[...], lnb_ref[...]).astype(BF16)
    h = _dot(perm_ref[...], h).astype(BF16)

    def project(n):
        return _dot(h, wx_refs[n][...]) + bx_ref[n], _dot(h, wg_refs[n][...]) + bg_ref[n]

    ys = []
    nxt = project(0)
    for n in range(nb):
        xr, gr = nxt
        if n + 1 < nb:
            nxt = project(n + 1)
        xe = jnp.concatenate([hist_s[n], xr], axis=0)
        cw = cw_ref[n]
        conv = cb_ref[n] + cw[0:1] * xr
        for k in range(1, CONV_WIDTH):
            off = HIST_ROWS - k * V7X_SUBLANES
            conv = conv + cw[k:k + 1] * xe[off:off + rows]
        hist_s[n] = xe[rows:rows + HIST_ROWS]
        cb16 = conv.astype(BF16)
        gate_r = _sigmoid(_dot(cb16, wra_ref[n]) + bra_ref[n])
        gate_i = _sigmoid(_dot(cb16, wri_ref[n]) + bri_ref[n])
        lam = lam_ref[n]
        log_sig = jnp.minimum(lam, 0.0) - jnp.log1p(jnp.exp(-jnp.abs(lam)))
        a = jnp.exp((LRU_C * gate_r) * log_sig)
        mult = jnp.sqrt(1.0 - a * a)
        if seq_start:
            first = lax.broadcasted_iota(jnp.int32, (rows, RNN_BLOCK), 0) < V7X_SUBLANES
            mult = jnp.where(first & (j == 0), 1.0, mult)
        u = mult * gate_i * conv
        hcur = state_s[n]
        hs = []
        for t in range(tt):
            sl = slice(t * V7X_SUBLANES, (t + 1) * V7X_SUBLANES)
            hcur = a[sl] * hcur + u[sl]
            hs.append(hcur)
        state_s[n] = hcur
        hr = jnp.concatenate(hs, axis=0)
        ys.append((hr * (gr * _sigmoid(gr))).astype(BF16))
    y = jnp.concatenate(ys, axis=1)
    y = _dot(permt_ref[...], y).astype(BF16)
    ya = _dot(y, wout_ref[...])
    ya_ref[...] = ya.reshape(V7X_SUBLANES, tt, D_MODEL).astype(ya_ref.dtype)

    @pl.when(j == pl.num_programs(0) - 1)
    def _():
        hist_out_ref[...] = hist_s[...]
        state_out_ref[...] = state_s[...]


def _rnn_branch(x, p, hist0, state0, *, tt, seq_start):
    t_len = x.shape[1]
    nb, rb = N_RNN_BLOCKS, RNN_BLOCK
    rows = tt * V7X_SUBLANES
    r = jnp.arange(rows)
    perm = jax.nn.one_hot((r % V7X_SUBLANES) * tt + r // V7X_SUBLANES, rows, dtype=BF16)
    assert x.dtype in (F32, BF16)
    kern = functools.partial(_rnn_kernel, tt=tt, seq_start=seq_start, normalised_input=x.dtype == BF16)
    n_tiles = t_len // tt
    w_in_block = lambda c: pl.BlockSpec((D_MODEL, rb), lambda j: (0, c), pipeline_mode=pl.Buffered(1))
    in_specs = [
        pl.BlockSpec((V7X_SUBLANES, tt, D_MODEL), lambda j: (0, j, 0)),
        _const_spec((1, D_MODEL)), _const_spec((1, D_MODEL)),
        *[w_in_block(OFF_XR // rb + n) for n in range(nb)],
        *[w_in_block(OFF_GR // rb + n) for n in range(nb)],
        _const_spec((nb, 1, rb)), _const_spec((nb, 1, rb)),
        _const_spec((nb, CONV_WIDTH, rb)), _const_spec((nb, 1, rb)),
        _const_spec((nb, rb, rb)), _const_spec((nb, 1, rb)),
        _const_spec((nb, rb, rb)), _const_spec((nb, 1, rb)),
        _const_spec((nb, 1, rb)),
        _const_spec((D_MODEL, D_MODEL)),
        _const_spec((rows, rows)), _const_spec((rows, rows)),
        _const_spec((nb, HIST_ROWS, rb)), _const_spec((nb, V7X_SUBLANES, rb)),
    ]
    out_shape = (
        jax.ShapeDtypeStruct((V7X_SUBLANES, t_len, D_MODEL), BF16),
        jax.ShapeDtypeStruct((nb, HIST_ROWS, rb), F32),
        jax.ShapeDtypeStruct((nb, V7X_SUBLANES, rb), F32),
    )
    out_specs = (
        pl.BlockSpec((V7X_SUBLANES, tt, D_MODEL), lambda j: (0, j, 0)),
        pl.BlockSpec((nb, HIST_ROWS, rb), lambda j: (0, 0, 0)),
        pl.BlockSpec((nb, V7X_SUBLANES, rb), lambda j: (0, 0, 0)),
    )
    return pl.pallas_call(
        kern,
        grid=(n_tiles,),
        in_specs=in_specs,
        out_specs=out_specs,
        out_shape=out_shape,
        scratch_shapes=[pltpu.VMEM((nb, HIST_ROWS, rb), F32), pltpu.VMEM((nb, V7X_SUBLANES, rb), F32)],
        compiler_params=pltpu.CompilerParams(dimension_semantics=("arbitrary",),
                                             vmem_limit_bytes=V7X_VMEM_LIMIT),
        name="rnn_branch_start" if seq_start else "rnn_branch",
    )(x, p["ln_g"], p["ln_b"], *([p["w_in"]] * (2 * nb)), p["bx"], p["bg"], p["cw"], p["cb"],
      p["wra"], p["bra"], p["wri"], p["bri"], p["lam"], p["w_rnn_out"], perm, perm.T, hist0, state0)


def _rope_rows(k, cos, sin_signed):
    lane = lax.broadcasted_iota(jnp.int32, k.shape, 1)
    partner = jnp.where((lane % HEAD_DIM) < HALF,
                        pltpu.roll(k, V7X_LANES - HALF, axis=1),
                        pltpu.roll(k, HALF, axis=1))
    return k * cos + partner * sin_signed


def _meta_kv_kernel(m_ref, lng_ref, lnb_ref, wk_ref, bk_ref, wv_ref, bv_ref, cos_ref, sin_ref,
                    k_ref, v_ref):
    h = _layer_norm(m_ref[...], lng_ref[...], lnb_ref[...]).astype(BF16)
    k = _dot(h, wk_ref[...]) + bk_ref[...]
    v = _dot(h, wv_ref[...]) + bv_ref[...]
    for g in range(N_KV_HEADS):
        sl = slice(g * KV_SLOT, (g + 1) * KV_SLOT)
        k_ref[:, sl] = _rope_rows(k[:, sl], cos_ref[...], sin_ref[...]).astype(BF16)
    v_ref[...] = v.astype(BF16)


def _meta_kv(meta, p, cos_rows, sin_rows):
    return pl.pallas_call(
        _meta_kv_kernel,
        out_shape=(jax.ShapeDtypeStruct((N_META, N_KV_HEADS * KV_SLOT), BF16),
                   jax.ShapeDtypeStruct((N_META, D_KV), BF16)),
        name="meta_kv",
    )(meta, p["ln_g"], p["ln_b"], p["wk"], p["bk"], p["wv"], p["bv"], cos_rows, sin_rows)


def _attn_kernel(x_ref, lng_ref, lnb_ref, wqt_ref, bqt_ref, wk_ref, bk_ref, wvt_ref, bvt_ref,
                 wgat_ref, bgat_ref, wout_ref, cost_ref, sint_ref, cosr_ref, sinr_ref,
                 kmeta_ref, vmetat_ref, sink_ref, yb_ref, h_ref,
                 qt_s, k_s, vt_s, gat_s, kprev_s, vprev_s, ot_s, act_s, *, tq, n_t):
    nblk = tq // BLOCK
    rows_g = GROUP * HEAD_DIM
    j = pl.program_id(0)

    @pl.when(j == 0)
    def _():
        for ref in (qt_s, k_s, vt_s, gat_s, kprev_s, vprev_s, act_s):
            ref[...] = jnp.zeros_like(ref)

    h = _layer_norm(x_ref[0], lng_ref[...], lnb_ref[...]).astype(BF16)
    h_ref[0] = h

    act_prev = act_s[...].astype(BF16)
    out_cols = D_MODEL // (N_KV_HEADS * nblk)

    def project_out(c):
        cs = slice(c * out_cols, (c + 1) * out_cols)
        yb_ref[0, :, cs] = _dot(act_prev, wout_ref[:, cs]).astype(yb_ref.dtype)

    def lanes_bias(b):
        return jnp.concatenate([b] * (tq // V7X_LANES), axis=1)

    scale = HEAD_DIM ** -0.5
    cos = cost_ref[...] * scale
    sin = sint_ref[...] * scale

    def project_q(g):
        rs = slice(g * rows_g, (g + 1) * rows_g)
        qt = _dot_nt(wqt_ref[rs, :], h) + lanes_bias(bqt_ref[rs, :])
        out = []
        for hh in range(GROUP):
            r0 = hh * HEAD_DIM
            q1, q2 = qt[r0:r0 + HALF], qt[r0 + HALF:r0 + HEAD_DIM]
            out.append((q1 * cos - q2 * sin, q2 * cos + q1 * sin))
        return out

    def store_q(g, roped):
        for hh, (lo, hi) in enumerate(roped):
            r0 = g * rows_g + hh * HEAD_DIM
            qt_s[r0:r0 + HALF, :] = lo
            qt_s[r0 + HALF:r0 + HEAD_DIM, :] = hi

    def project_gate(g):
        rs = slice(g * rows_g, (g + 1) * rows_g)
        gat = _dot_nt(wgat_ref[rs, :], h) + lanes_bias(bgat_ref[rs, :])
        return gat * _sigmoid(gat)

    i = jnp.maximum(j - 1, 0) % n_t
    k = k_s[...].astype(BF16)
    vt = vt_s[...].astype(BF16)
    k_carry, vt_carry = kprev_s[...].astype(BF16), vprev_s[...].astype(BF16)
    kmeta = kmeta_ref[...]
    vmetat = vmetat_ref[...]

    key_row = lax.broadcasted_iota(jnp.int32, (BLOCK, GROUP * BLOCK), 0)
    qry_col = lax.broadcasted_iota(jnp.int32, (BLOCK, GROUP * BLOCK), 1) % BLOCK
    cur_ok = key_row <= qry_col
    prev_in_window = key_row > qry_col
    first_prev_ok = prev_in_window & (i > 0)

    def scores(g, jb):
        tok = slice(jb * BLOCK, (jb + 1) * BLOCK)
        ks = slice(g * KV_SLOT, g * KV_SLOT + HEAD_DIM)
        k_prev = k_carry if jb == 0 else k[(jb - 1) * BLOCK:jb * BLOCK]
        q_g = jnp.concatenate(
            [qt_s[(g * GROUP + hh) * HEAD_DIM:(g * GROUP + hh + 1) * HEAD_DIM, tok].astype(BF16)
             for hh in range(GROUP)], axis=1)
        s_prev = _dot(k_prev[:, ks], q_g)
        if jb == 0:
            s_prev = jnp.where(first_prev_ok, s_prev, NEG_INF)
        s_cm = _dot(jnp.concatenate([k[tok, ks], kmeta[:, ks]], axis=0), q_g)
        s_band = jnp.where(cur_ok, s_cm[:BLOCK], s_prev)
        return s_band, s_cm[BLOCK:]

    def finish(g, jb, s):
        s_band, s_meta = s
        tok = slice(jb * BLOCK, (jb + 1) * BLOCK)
        vs = slice(g * HEAD_DIM, (g + 1) * HEAD_DIM)
        vt_prev = vt_carry if jb == 0 else vt[:, (jb - 1) * BLOCK:jb * BLOCK]
        sink = sink_ref[g]
        m = jnp.maximum(jnp.maximum(jnp.max(s_band, axis=0, keepdims=True),
                                    jnp.max(s_meta, axis=0, keepdims=True)), sink)
        p_band = jnp.exp(s_band - m)
        p_meta = jnp.exp(s_meta - m)
        denom = (jnp.sum(p_band, axis=0, keepdims=True) + jnp.sum(p_meta, axis=0, keepdims=True)
                 + jnp.exp(sink - m))
        p_cur = jnp.where(cur_ok, p_band, 0.0)
        p_prev = jnp.where(cur_ok, 0.0, p_band)
        p_cm = jnp.concatenate([p_cur, p_meta], axis=0).astype(BF16)
        v_cm = jnp.concatenate([vt[vs, tok], vmetat[vs]], axis=1)
        o = _dot(vt_prev[vs], p_prev.astype(BF16)) + _dot(v_cm, p_cm)
        o = o * (1.0 / denom)
        for hh in range(GROUP):
            r0 = (g * GROUP + hh) * HEAD_DIM
            ot_s[r0:r0 + HEAD_DIM, tok] = o[:, hh * BLOCK:(hh + 1) * BLOCK]

    gates = [None] * N_KV_HEADS
    held_q = None
    for g in range(N_KV_HEADS):
        pg = (g - 1) % N_KV_HEADS
        for jb in range(nblk):
            s = scores(g, jb)
            if jb == 0:
                roped = project_q(pg)
                if pg < g:
                    store_q(pg, roped)
                else:
                    held_q = roped
            elif jb == 1:
                gates[pg] = project_gate(pg)
            project_out(g * nblk + jb)
            finish(g, jb, s)
    store_q(N_KV_HEADS - 1, held_q)

    k_new = _dot(h, wk_ref[...]) + bk_ref[...]
    vt_new = _dot_nt(wvt_ref[...], h) + lanes_bias(bvt_ref[...])

    act_s[...] = (ot_s[...] * gat_s[...]).T

    kprev_s[...] = k_s[tq - BLOCK:, :]
    vprev_s[...] = vt_s[:, tq - BLOCK:]
    for g in range(N_KV_HEADS):
        sl = slice(g * KV_SLOT, (g + 1) * KV_SLOT)
        k_s[:, sl] = _rope_rows(k_new[:, sl], cosr_ref[...], sinr_ref[...])
        gat_s[g * rows_g:(g + 1) * rows_g, :] = gates[g]
    vt_s[...] = vt_new


def _attn_branch(x, p, tabs, kmeta, vmetat, *, tq):
    b, t_len, _ = x.shape
    n_t = t_len // tq
    n_tiles = b * n_t
    kern = functools.partial(_attn_kernel, tq=tq, n_t=n_t)
    proj_tile = lambda j: jnp.minimum(j, n_tiles - 1)
    out_tile = lambda j: jnp.maximum(j - 2, 0)
    in_specs = [
        pl.BlockSpec((1, tq, D_MODEL), lambda j: (proj_tile(j) // n_t, proj_tile(j) % n_t, 0)),
        _const_spec((1, D_MODEL)), _const_spec((1, D_MODEL)),
        _const_spec((D_MODEL, D_MODEL)), _const_spec((D_MODEL, V7X_LANES)),
        _const_spec((D_MODEL, N_KV_HEADS * KV_SLOT)), _const_spec((1, N_KV_HEADS * KV_SLOT)),
        _const_spec((D_KV, D_MODEL)), _const_spec((D_KV, V7X_LANES)),
        _const_spec((D_MODEL, D_MODEL)), _const_spec((D_MODEL, V7X_LANES)),
        _const_spec((D_MODEL, D_MODEL)),
        pl.BlockSpec((HALF, tq), lambda j: (0, proj_tile(j) % n_t)),
        pl.BlockSpec((HALF, tq), lambda j: (0, proj_tile(j) % n_t)),
        pl.BlockSpec((tq, KV_SLOT), lambda j: (proj_tile(j) % n_t, 0)),
        pl.BlockSpec((tq, KV_SLOT), lambda j: (proj_tile(j) % n_t, 0)),
        _const_spec((N_META, N_KV_HEADS * KV_SLOT)), _const_spec((D_KV, N_META)),
        _const_spec((N_KV_HEADS, 1, GROUP * BLOCK)),
    ]
    return pl.pallas_call(
        kern,
        grid=(n_tiles + 2,),
        in_specs=in_specs,
        out_specs=(pl.BlockSpec((1, tq, D_MODEL), lambda j: (out_tile(j) // n_t, out_tile(j) % n_t, 0)),
                   pl.BlockSpec((1, tq, D_MODEL), lambda j: (proj_tile(j) // n_t, proj_tile(j) % n_t, 0))),
        out_shape=(jax.ShapeDtypeStruct((b, t_len, D_MODEL), BF16),
                   jax.ShapeDtypeStruct((b, t_len, D_MODEL), BF16)),
        scratch_shapes=[pltpu.VMEM((D_MODEL, tq), F32),
                        pltpu.VMEM((tq, N_KV_HEADS * KV_SLOT), F32),
                        pltpu.VMEM((D_KV, tq), F32),
                        pltpu.VMEM((D_MODEL, tq), F32),
                        pltpu.VMEM((BLOCK, N_KV_HEADS * KV_SLOT), F32),
                        pltpu.VMEM((D_KV, BLOCK), F32),
                        pltpu.VMEM((D_MODEL, tq), F32),
                        pltpu.VMEM((tq, D_MODEL), F32)],
        compiler_params=pltpu.CompilerParams(dimension_semantics=("arbitrary",),
                                             vmem_limit_bytes=V7X_VMEM_LIMIT),
        name="attn_branch",
    )(x, p["ln_g"], p["ln_b"], p["wqt"], p["bqt"], p["wk"], p["bk"], p["wvt"], p["bvt"],
      p["wgat"], p["bgat"], p["w_attn_out"], tabs["cos_t"], tabs["sin_t"], tabs["cos_r"], tabs["sin_r"],
      kmeta, vmetat, p["sink_rows"])


def _merge_kernel(x_ref, h_ref, ya_ref, yb_ref, lng_ref, lnb_ref, *refs):
    wga_refs, wgb_refs = refs[:MERGE_BLOCKS], refs[MERGE_BLOCKS:2 * MERGE_BLOCKS]
    bmg_ref, wo_ref, bo_ref, g2_ref, b2_ref, out_ref = refs[2 * MERGE_BLOCKS:]
    h = h_ref[0]
    mixed = []
    for c in range(MERGE_BLOCKS):
        cs = slice(c * MERGE_COLS, (c + 1) * MERGE_COLS)
        gs = slice(D_MODEL + c * MERGE_COLS, D_MODEL + (c + 1) * MERGE_COLS)
        ga = _sigmoid(_dot(h, wga_refs[c][...]) + bmg_ref[:, cs])
        gb = _sigmoid(_dot(h, wgb_refs[c][...]) + bmg_ref[:, gs])
        mixed.append(ga * ya_ref[0, :, cs].astype(F32) + gb * yb_ref[0, :, cs].astype(F32))
    mixed = jnp.concatenate(mixed, axis=1)
    out = _dot(mixed.astype(BF16), wo_ref[...]) + bo_ref[...]
    h32 = _layer_norm(x_ref[0], lng_ref[...], lnb_ref[...])
    out_ref[0] = _layer_norm(DEEPNORM_ALPHA * h32 + out, g2_ref[...], b2_ref[...])


def _merge(x, h, ya, yb, p, *, tm):
    b, t_len, _ = x.shape
    tile = pl.BlockSpec((1, tm, D_MODEL), lambda bi, i: (bi, i, 0))
    w_in_block = lambda c: pl.BlockSpec((D_MODEL, MERGE_COLS), lambda bi, i: (0, c), pipeline_mode=pl.Buffered(1))
    in_specs = [tile, tile, tile, tile,
                _const_spec((1, D_MODEL)), _const_spec((1, D_MODEL)),
                *[w_in_block(OFF_G // MERGE_COLS + c) for c in range(MERGE_BLOCKS)],
                *[w_in_block((OFF_G + D_MODEL) // MERGE_COLS + c) for c in range(MERGE_BLOCKS)],
                _const_spec((1, 2 * D_MODEL)),
                _const_spec((D_MODEL, D_MODEL)), _const_spec((1, D_MODEL)),
                _const_spec((1, D_MODEL)), _const_spec((1, D_MODEL))]
    return pl.pallas_call(
        _merge_kernel,
        grid=(b, t_len // tm),
        in_specs=in_specs,
        out_specs=tile,
        out_shape=jax.ShapeDtypeStruct((b, t_len, D_MODEL), F32),
        compiler_params=pltpu.CompilerParams(dimension_semantics=("arbitrary", "arbitrary"),
                                             vmem_limit_bytes=V7X_VMEM_LIMIT),
        name="merge",
    )(x, h, ya, yb, p["ln_g"], p["ln_b"], *([p["w_in"]] * (2 * MERGE_BLOCKS)), p["bmg"], p["w_o"], p["b_o"],
      p["ln2_g"], p["ln2_b"])


def _rope_tables(t_total):
    inv = ROPE_THETA ** (-jnp.arange(HALF, dtype=F32) / HALF)
    ang = jnp.arange(t_total, dtype=F32)[:, None] * inv[None, :]
    cos, sin = jnp.cos(ang), jnp.sin(ang)
    zeros = jnp.zeros_like(cos)
    cos_r = jnp.concatenate([cos, cos, zeros, zeros], axis=1)
    sin_r = jnp.concatenate([-sin, sin, zeros, zeros], axis=1)
    return cos, sin, cos_r, sin_r


def _slotted(w):
    lead = w.shape[:-1]
    w4 = w.reshape(lead + (N_KV_HEADS, HEAD_DIM))
    w4 = jnp.concatenate([w4, jnp.zeros_like(w4)], axis=-1)
    return w4.reshape(lead + (N_KV_HEADS * KV_SLOT,))


def kernel(x, meta_tokens, ln_emb_g, ln_emb_b, w_in, b_in, conv_w, conv_b, w_ra, b_ra, w_ri, b_ri,
           lru_lambda, sinks, w_rnn_out, w_attn_out, w_o, b_o, ln_g, ln_b):
    b, seq, _ = x.shape
    assert b == V7X_SUBLANES and w_in.shape[0] == DEPTH
    nb, rb = N_RNN_BLOCKS, RNN_BLOCK
    w = w_in[0].astype(BF16)
    bi = b_in[0]
    row = lambda v: v.reshape(1, -1)
    blocks = lambda v: v.reshape(nb, 1, rb)
    lanes = lambda v: jnp.broadcast_to(v[:, None], (v.shape[0], V7X_LANES))

    common = {"ln_g": row(ln_emb_g), "ln_b": row(ln_emb_b)}
    p_rnn = dict(common, w_in=w,
                 bx=blocks(bi[:OFF_GR]), bg=blocks(bi[OFF_GR:OFF_Q]),
                 cw=conv_w[0].reshape(CONV_WIDTH, nb, rb).transpose(1, 0, 2), cb=blocks(conv_b[0]),
                 wra=w_ra[0].astype(BF16), bra=blocks(b_ra[0]),
                 wri=w_ri[0].astype(BF16), bri=blocks(b_ri[0]),
                 lam=blocks(lru_lambda[0]), w_rnn_out=w_rnn_out[0].astype(BF16))
    p_attn = dict(common,
                  wqt=w[:, OFF_Q:OFF_K].T, bqt=lanes(bi[OFF_Q:OFF_K]),
                  wk=_slotted(w[:, OFF_K:OFF_V]), bk=row(_slotted(bi[OFF_K:OFF_V])),
                  wvt=w[:, OFF_V:OFF_GA].T, bvt=lanes(bi[OFF_V:OFF_GA]),
                  wgat=w[:, OFF_GA:OFF_G].T, bgat=lanes(bi[OFF_GA:OFF_G]),
                  wv=w[:, OFF_V:OFF_GA], bv=row(bi[OFF_V:OFF_GA]),
                  w_attn_out=w_attn_out[0].astype(BF16),
                  sink_rows=jnp.repeat(sinks[0].astype(F32), BLOCK).reshape(N_KV_HEADS, 1, GROUP * BLOCK))
    p_merge = dict(common, w_in=w, bmg=row(bi[OFF_G:]),
                   w_o=w_o[0].astype(BF16), b_o=row(b_o[0]), ln2_g=row(ln_g[0]), ln2_b=row(ln_b[0]))

    cos, sin, cos_r, sin_r = _rope_tables(N_META + seq)
    tabs = {"cos_t": cos[N_META:].T, "sin_t": sin[N_META:].T, "cos_r": cos_r[N_META:], "sin_r": sin_r[N_META:]}

    meta = meta_tokens.astype(x.dtype)
    meta_b = jnp.broadcast_to(meta[None], (V7X_SUBLANES, N_META, D_MODEL))
    zero_hist = jnp.zeros((nb, HIST_ROWS, rb), F32)
    zero_state = jnp.zeros((nb, V7X_SUBLANES, rb), F32)
    _, hist0, state0 = _rnn_branch(meta_b, p_rnn, zero_hist, zero_state, tt=N_META, seq_start=True)
    kmeta, vmeta = _meta_kv(meta, p_attn, cos_r[:N_META], sin_r[:N_META])

    yb, h = _attn_branch(x, p_attn, tabs, kmeta, vmeta.T, tq=256)
    ya, _, _ = _rnn_branch(h, p_rnn, hist0, state0, tt=32, seq_start=False)
    return _merge(x, h, ya, yb, p_merge, tm=256)
```

```python
import functools

import jax
import jax.numpy as jnp
from jax import lax
from jax.experimental import pallas as pl
from jax.experimental.pallas import tpu as pltpu

D_MODEL = 2048
N_META = 16
N_RNN_BLOCKS = 8
RNN_BLOCK = 256
CONV_WIDTH = 4
LRU_C = 8.0
HEAD_DIM = 64
HALF = HEAD_DIM // 2
N_Q_HEADS = 32
N_KV_HEADS = 4
GROUP = 8
D_KV = N_KV_HEADS * HEAD_DIM
BLOCK = 128
ROPE_THETA = 10000.0
NEG_INF = -1e30
LN_EPS = 1e-5
DEPTH = 1
DEEPNORM_ALPHA = (2.0 * DEPTH) ** 0.25
OFF_XR = 0
OFF_GR = D_MODEL
OFF_Q = 2 * D_MODEL
OFF_K = OFF_Q + D_MODEL
OFF_V = OFF_K + D_KV
OFF_GA = OFF_V + D_KV
OFF_G = OFF_GA + D_MODEL

V7X_SUBLANES = 8
V7X_LANES = 128
V7X_VMEM_LIMIT = 56 * 1024 * 1024

KV_SLOT = V7X_LANES
HIST_ROWS = (CONV_WIDTH - 1) * V7X_SUBLANES
RNN_PERM_STEPS = 32
MERGE_COLS = 512
MERGE_BLOCKS = D_MODEL // MERGE_COLS

BF16 = jnp.bfloat16
F32 = jnp.float32


def _layer_norm(x, g, b):
    mu = jnp.mean(x, axis=-1, keepdims=True)
    xc = x - mu
    var = jnp.mean(xc * xc, axis=-1, keepdims=True)
    return xc * lax.rsqrt(var + LN_EPS) * g + b


def _sigmoid(x):
    return 1.0 / (1.0 + jnp.exp(-x))


def _dot(a, b):
    return jnp.dot(a, b, preferred_element_type=F32)


def _dot_nt(a, b):
    return lax.dot_general(a, b, (((1,), (1,)), ((), ())), preferred_element_type=F32)


def _const_spec(shape):
    nd = len(shape)
    return pl.BlockSpec(shape, lambda *_: (0,) * nd, pipeline_mode=pl.Buffered(1))


def _rnn_kernel(x_ref, lng_ref, lnb_ref, *refs, tt, seq_start, normalised_input):
    wx_refs, wg_refs = refs[:N_RNN_BLOCKS], refs[N_RNN_BLOCKS:2 * N_RNN_BLOCKS]
    (bx_ref, bg_ref, cw_ref, cb_ref, wra_ref, bra_ref, wri_ref, bri_ref, lam_ref, wout_ref, perm_ref,
     permt_ref, hist0_ref, state0_ref, ya_ref, hist_out_ref, state_out_ref,
     y_s, hist_s, state_s) = refs[2 * N_RNN_BLOCKS:]
    rows = tt * V7X_SUBLANES
    nb = N_RNN_BLOCKS
    j = pl.program_id(0)

    @pl.when(j == 0)
    def _():
        y_s[...] = jnp.zeros_like(y_s)
        hist_s[...] = hist0_ref[...]
        state_s[...] = state0_ref[...]

    ts = perm_ref.shape[0] // V7X_SUBLANES
    parts = []
    for s0 in range(0, tt, ts):
        hs_ = x_ref[:, s0:s0 + ts, :].reshape(ts * V7X_SUBLANES, D_MODEL)
        if not normalised_input:
            hs_ = _layer_norm(hs_, lng_ref[...], lnb_ref[...]).astype(BF16)
        parts.append(_dot(perm_ref[...], hs_).astype(BF16))
    h = jnp.concatenate(parts, axis=0)

    def project(n):
        return _dot(h, wx_refs[n][...]) + bx_ref[n], _dot(h, wg_refs[n][...]) + bg_ref[n]

    sub = ts * V7X_SUBLANES
    y_prev = y_s[...].astype(BF16)

    def project_out(n):
        cs = slice(n * RNN_BLOCK, (n + 1) * RNN_BLOCK)
        ya = _dot(y_prev, wout_ref[:, cs])
        for k, s0 in enumerate(range(0, tt, ts)):
            ya_ref[:, s0:s0 + ts, cs] = (ya[k * sub:(k + 1) * sub]
                                         .reshape(V7X_SUBLANES, ts, RNN_BLOCK).astype(ya_ref.dtype))

    ys = []
    nxt = project(0)
    for n in range(nb):
        xr, gr = nxt
        if n + 1 < nb:
            nxt = project(n + 1)
        project_out(n)
        xe = jnp.concatenate([hist_s[n], xr], axis=0)
        cw = cw_ref[n]
        conv = cb_ref[n] + cw[0:1] * xr
        for k in range(1, CONV_WIDTH):
            off = HIST_ROWS - k * V7X_SUBLANES
            conv = conv + cw[k:k + 1] * xe[off:off + rows]
        hist_s[n] = xe[rows:rows + HIST_ROWS]
        cb16 = conv.astype(BF16)
        gate_r = _sigmoid(_dot(cb16, wra_ref[n]) + bra_ref[n])
        gate_i = _sigmoid(_dot(cb16, wri_ref[n]) + bri_ref[n])
        lam = lam_ref[n]
        log_sig = jnp.minimum(lam, 0.0) - jnp.log1p(jnp.exp(-jnp.abs(lam)))
        a = jnp.exp((LRU_C * gate_r) * log_sig)
        mult = jnp.sqrt(1.0 - a * a)
        if seq_start:
            first = lax.broadcasted_iota(jnp.int32, (rows, RNN_BLOCK), 0) < V7X_SUBLANES
            mult = jnp.where(first & (j == 0), 1.0, mult)
        u = mult * gate_i * conv
        hcur = state_s[n]
        hs = []
        for t in range(tt):
            sl = slice(t * V7X_SUBLANES, (t + 1) * V7X_SUBLANES)
            hcur = a[sl] * hcur + u[sl]
            hs.append(hcur)
        state_s[n] = hcur
        hr = jnp.concatenate(hs, axis=0)
        ys.append((hr * (gr * _sigmoid(gr))).astype(BF16))
    y = jnp.concatenate(ys, axis=1)
    for r0 in range(0, rows, sub):
        y_s[r0:r0 + sub, :] = _dot(permt_ref[...], y[r0:r0 + sub])

    @pl.when(j == pl.num_programs(0) - 2)
    def _():
        hist_out_ref[...] = hist_s[...]
        state_out_ref[...] = state_s[...]


def _rnn_branch(x, p, hist0, state0, *, tt, seq_start):
    t_len = x.shape[1]
    nb, rb = N_RNN_BLOCKS, RNN_BLOCK
    rows = tt * V7X_SUBLANES
    ts = min(tt, RNN_PERM_STEPS)
    prow = ts * V7X_SUBLANES
    r = jnp.arange(prow)
    perm = jax.nn.one_hot((r % V7X_SUBLANES) * ts + r // V7X_SUBLANES, prow, dtype=BF16)
    assert x.dtype in (F32, BF16)
    kern = functools.partial(_rnn_kernel, tt=tt, seq_start=seq_start, normalised_input=x.dtype == BF16)
    n_tiles = t_len // tt
    w_in_block = lambda c: pl.BlockSpec((D_MODEL, rb), lambda j: (0, c), pipeline_mode=pl.Buffered(1))
    in_specs = [
        pl.BlockSpec((V7X_SUBLANES, tt, D_MODEL), lambda j: (0, jnp.minimum(j, n_tiles - 1), 0)),
        _const_spec((1, D_MODEL)), _const_spec((1, D_MODEL)),
        *[w_in_block(OFF_XR // rb + n) for n in range(nb)],
        *[w_in_block(OFF_GR // rb + n) for n in range(nb)],
        _const_spec((nb, 1, rb)), _const_spec((nb, 1, rb)),
        _const_spec((nb, CONV_WIDTH, rb)), _const_spec((nb, 1, rb)),
        _const_spec((nb, rb, rb)), _const_spec((nb, 1, rb)),
        _const_spec((nb, rb, rb)), _const_spec((nb, 1, rb)),
        _const_spec((nb, 1, rb)),
        _const_spec((D_MODEL, D_MODEL)),
        _const_spec((prow, prow)), _const_spec((prow, prow)),
        _const_spec((nb, HIST_ROWS, rb)), _const_spec((nb, V7X_SUBLANES, rb)),
    ]
    out_shape = (
        jax.ShapeDtypeStruct((V7X_SUBLANES, t_len, D_MODEL), BF16),
        jax.ShapeDtypeStruct((nb, HIST_ROWS, rb), F32),
        jax.ShapeDtypeStruct((nb, V7X_SUBLANES, rb), F32),
    )
    out_specs = (
        pl.BlockSpec((V7X_SUBLANES, tt, D_MODEL), lambda j: (0, jnp.maximum(j - 1, 0), 0)),
        pl.BlockSpec((nb, HIST_ROWS, rb), lambda j: (0, 0, 0)),
        pl.BlockSpec((nb, V7X_SUBLANES, rb), lambda j: (0, 0, 0)),
    )
    return pl.pallas_call(
        kern,
        grid=(n_tiles + 1,),
        in_specs=in_specs,
        out_specs=out_specs,
        out_shape=out_shape,
        scratch_shapes=[pltpu.VMEM((rows, D_MODEL), F32),
                        pltpu.VMEM((nb, HIST_ROWS, rb), F32), pltpu.VMEM((nb, V7X_SUBLANES, rb), F32)],
        compiler_params=pltpu.CompilerParams(dimension_semantics=("arbitrary",),
                                             vmem_limit_bytes=V7X_VMEM_LIMIT),
        name="rnn_branch_start" if seq_start else "rnn_branch",
    )(x, p["ln_g"], p["ln_b"], *([p["w_in"]] * (2 * nb)), p["bx"], p["bg"], p["cw"], p["cb"],
      p["wra"], p["bra"], p["wri"], p["bri"], p["lam"], p["w_rnn_out"], perm, perm.T, hist0, state0)


def _rope_rows(k, cos, sin_signed):
    lane = lax.broadcasted_iota(jnp.int32, k.shape, 1)
    partner = jnp.where((lane % HEAD_DIM) < HALF,
                        pltpu.roll(k, V7X_LANES - HALF, axis=1),
                        pltpu.roll(k, HALF, axis=1))
    return k * cos + partner * sin_signed


def _meta_kv_kernel(m_ref, lng_ref, lnb_ref, wk_ref, bk_ref, wv_ref, bv_ref, cos_ref, sin_ref,
                    k_ref, v_ref):
    h = _layer_norm(m_ref[...], lng_ref[...], lnb_ref[...]).astype(BF16)
    k = _dot(h, wk_ref[...]) + bk_ref[...]
    v = _dot(h, wv_ref[...]) + bv_ref[...]
    for g in range(N_KV_HEADS):
        sl = slice(g * KV_SLOT, (g + 1) * KV_SLOT)
        k_ref[:, sl] = _rope_rows(k[:, sl], cos_ref[...], sin_ref[...]).astype(BF16)
    v_ref[...] = v.astype(BF16)


def _meta_kv(meta, p, cos_rows, sin_rows):
    return pl.pallas_call(
        _meta_kv_kernel,
        out_shape=(jax.ShapeDtypeStruct((N_META, N_KV_HEADS * KV_SLOT), BF16),
                   jax.ShapeDtypeStruct((N_META, D_KV), BF16)),
        name="meta_kv",
    )(meta, p["ln_g"], p["ln_b"], p["wk"], p["bk"], p["wv"], p["bv"], cos_rows, sin_rows)


def _attn_kernel(x_ref, lng_ref, lnb_ref, wqt_ref, bqt_ref, wk_ref, bk_ref, wvt_ref, bvt_ref,
                 wgat_ref, bgat_ref, wout_ref, cost_ref, sint_ref, cosr_ref, sinr_ref,
                 kmeta_ref, vmetat_ref, sink_ref, yb_ref, h_ref,
                 qt_s, k_s, vt_s, gat_s, kprev_s, vprev_s, ot_s, act_s, *, tq, n_t):
    nblk = tq // BLOCK
    rows_g = GROUP * HEAD_DIM
    j = pl.program_id(0)

    @pl.when(j == 0)
    def _():
        for ref in (qt_s, k_s, vt_s, gat_s, kprev_s, vprev_s, act_s):
            ref[...] = jnp.zeros_like(ref)

    h = _layer_norm(x_ref[0], lng_ref[...], lnb_ref[...]).astype(BF16)
    h_ref[0] = h

    act_prev = act_s[...].astype(BF16)
    out_cols = D_MODEL // (N_KV_HEADS * nblk)

    def project_out(c):
        cs = slice(c * out_cols, (c + 1) * out_cols)
        yb_ref[0, :, cs] = _dot(act_prev, wout_ref[:, cs]).astype(yb_ref.dtype)

    def lanes_bias(b):
        return jnp.concatenate([b] * (tq // V7X_LANES), axis=1)

    scale = HEAD_DIM ** -0.5
    cos = cost_ref[...] * scale
    sin = sint_ref[...] * scale

    def project_q(g):
        rs = slice(g * rows_g, (g + 1) * rows_g)
        qt = _dot_nt(wqt_ref[rs, :], h) + lanes_bias(bqt_ref[rs, :])
        out = []
        for hh in range(GROUP):
            r0 = hh * HEAD_DIM
            q1, q2 = qt[r0:r0 + HALF], qt[r0 + HALF:r0 + HEAD_DIM]
            out.append((q1 * cos - q2 * sin, q2 * cos + q1 * sin))
        return out

    def store_q(g, roped):
        for hh, (lo, hi) in enumerate(roped):
            r0 = g * rows_g + hh * HEAD_DIM
            qt_s[r0:r0 + HALF, :] = lo
            qt_s[r0 + HALF:r0 + HEAD_DIM, :] = hi

    def project_gate(g):
        rs = slice(g * rows_g, (g + 1) * rows_g)
        gat = _dot_nt(wgat_ref[rs, :], h) + lanes_bias(bgat_ref[rs, :])
        return gat * _sigmoid(gat)

    i = jnp.maximum(j - 1, 0) % n_t
    k = k_s[...].astype(BF16)
    vt = vt_s[...].astype(BF16)
    k_carry, vt_carry = kprev_s[...].astype(BF16), vprev_s[...].astype(BF16)
    kmeta = kmeta_ref[...]
    vmetat = vmetat_ref[...]

    key_row = lax.broadcasted_iota(jnp.int32, (BLOCK, GROUP * BLOCK), 0)
    qry_col = lax.broadcasted_iota(jnp.int32, (BLOCK, GROUP * BLOCK), 1) % BLOCK
    cur_ok = key_row <= qry_col
    prev_in_window = key_row > qry_col
    first_prev_ok = prev_in_window & (i > 0)

    def scores(g, jb):
        tok = slice(jb * BLOCK, (jb + 1) * BLOCK)
        ks = slice(g * KV_SLOT, g * KV_SLOT + HEAD_DIM)
        k_prev = k_carry if jb == 0 else k[(jb - 1) * BLOCK:jb * BLOCK]
        q_g = jnp.concatenate(
            [qt_s[(g * GROUP + hh) * HEAD_DIM:(g * GROUP + hh + 1) * HEAD_DIM, tok].astype(BF16)
             for hh in range(GROUP)], axis=1)
        s_prev = _dot(k_prev[:, ks], q_g)
        if jb == 0:
            s_prev = jnp.where(first_prev_ok, s_prev, NEG_INF)
        s_cm = _dot(jnp.concatenate([k[tok, ks], kmeta[:, ks]], axis=0), q_g)
        s_band = jnp.where(cur_ok, s_cm[:BLOCK], s_prev)
        return s_band, s_cm[BLOCK:]

    def finish(g, jb, s):
        s_band, s_meta = s
        tok = slice(jb * BLOCK, (jb + 1) * BLOCK)
        vs = slice(g * HEAD_DIM, (g + 1) * HEAD_DIM)
        vt_prev = vt_carry if jb == 0 else vt[:, (jb - 1) * BLOCK:jb * BLOCK]
        sink = sink_ref[g]
        m = jnp.maximum(jnp.maximum(jnp.max(s_band, axis=0, keepdims=True),
                                    jnp.max(s_meta, axis=0, keepdims=True)), sink)
        p_band = jnp.exp(s_band - m)
        p_meta = jnp.exp(s_meta - m)
        denom = (jnp.sum(p_band, axis=0, keepdims=True) + jnp.sum(p_meta, axis=0, keepdims=True)
                 + jnp.exp(sink - m))
        p_cur = jnp.where(cur_ok, p_band, 0.0)
        p_prev = jnp.where(cur_ok, 0.0, p_band)
        p_cm = jnp.concatenate([p_cur, p_meta], axis=0).astype(BF16)
        v_cm = jnp.concatenate([vt[vs, tok], vmetat[vs]], axis=1)
        o = _dot(vt_prev[vs], p_prev.astype(BF16)) + _dot(v_cm, p_cm)
        o = o * (1.0 / denom)
        for hh in range(GROUP):
            r0 = (g * GROUP + hh) * HEAD_DIM
            ot_s[r0:r0 + HEAD_DIM, tok] = o[:, hh * BLOCK:(hh + 1) * BLOCK]

    gates = [None] * N_KV_HEADS
    held_q = None
    for g in range(N_KV_HEADS):
        pg = (g - 1) % N_KV_HEADS
        for jb in range(nblk):
            s = scores(g, jb)
            if jb == 0:
                roped = project_q(pg)
                if pg < g:
                    store_q(pg, roped)
                else:
                    held_q = roped
            elif jb == 1:
                gates[pg] = project_gate(pg)
            project_out(g * nblk + jb)
            finish(g, jb, s)
    store_q(N_KV_HEADS - 1, held_q)

    k_new = _dot(h, wk_ref[...]) + bk_ref[...]
    vt_new = _dot_nt(wvt_ref[...], h) + lanes_bias(bvt_ref[...])

    act_s[...] = (ot_s[...] * gat_s[...]).T

    kprev_s[...] = k_s[tq - BLOCK:, :]
    vprev_s[...] = vt_s[:, tq - BLOCK:]
    for g in range(N_KV_HEADS):
        sl = slice(g * KV_SLOT, (g + 1) * KV_SLOT)
        k_s[:, sl] = _rope_rows(k_new[:, sl], cosr_ref[...], sinr_ref[...])
        gat_s[g * rows_g:(g + 1) * rows_g, :] = gates[g]
    vt_s[...] = vt_new


def _attn_branch(x, p, tabs, kmeta, vmetat, *, tq):
    b, t_len, _ = x.shape
    n_t = t_len // tq
    n_tiles = b * n_t
    kern = functools.partial(_attn_kernel, tq=tq, n_t=n_t)
    proj_tile = lambda j: jnp.minimum(j, n_tiles - 1)
    out_tile = lambda j: jnp.maximum(j - 2, 0)
    in_specs = [
        pl.BlockSpec((1, tq, D_MODEL), lambda j: (proj_tile(j) // n_t, proj_tile(j) % n_t, 0)),
        _const_spec((1, D_MODEL)), _const_spec((1, D_MODEL)),
        _const_spec((D_MODEL, D_MODEL)), _const_spec((D_MODEL, V7X_LANES)),
        _const_spec((D_MODEL, N_KV_HEADS * KV_SLOT)), _const_spec((1, N_KV_HEADS * KV_SLOT)),
        _const_spec((D_KV, D_MODEL)), _const_spec((D_KV, V7X_LANES)),
        _const_spec((D_MODEL, D_MODEL)), _const_spec((D_MODEL, V7X_LANES)),
        _const_spec((D_MODEL, D_MODEL)),
        pl.BlockSpec((HALF, tq), lambda j: (0, proj_tile(j) % n_t)),
        pl.BlockSpec((HALF, tq), lambda j: (0, proj_tile(j) % n_t)),
        pl.BlockSpec((tq, KV_SLOT), lambda j: (proj_tile(j) % n_t, 0)),
        pl.BlockSpec((tq, KV_SLOT), lambda j: (proj_tile(j) % n_t, 0)),
        _const_spec((N_META, N_KV_HEADS * KV_SLOT)), _const_spec((D_KV, N_META)),
        _const_spec((N_KV_HEADS, 1, GROUP * BLOCK)),
    ]
    return pl.pallas_call(
        kern,
        grid=(n_tiles + 2,),
        in_specs=in_specs,
        out_specs=(pl.BlockSpec((1, tq, D_MODEL), lambda j: (out_tile(j) // n_t, out_tile(j) % n_t, 0)),
                   pl.BlockSpec((1, tq, D_MODEL), lambda j: (proj_tile(j) // n_t, proj_tile(j) % n_t, 0))),
        out_shape=(jax.ShapeDtypeStruct((b, t_len, D_MODEL), BF16),
                   jax.ShapeDtypeStruct((b, t_len, D_MODEL), BF16)),
        scratch_shapes=[pltpu.VMEM((D_MODEL, tq), F32),
                        pltpu.VMEM((tq, N_KV_HEADS * KV_SLOT), F32),
                        pltpu.VMEM((D_KV, tq), F32),
                        pltpu.VMEM((D_MODEL, tq), F32),
                        pltpu.VMEM((BLOCK, N_KV_HEADS * KV_SLOT), F32),
                        pltpu.VMEM((D_KV, BLOCK), F32),
                        pltpu.VMEM((D_MODEL, tq), F32),
                        pltpu.VMEM((tq, D_MODEL), F32)],
        compiler_params=pltpu.CompilerParams(dimension_semantics=("arbitrary",),
                                             vmem_limit_bytes=V7X_VMEM_LIMIT),
        name="attn_branch",
    )(x, p["ln_g"], p["ln_b"], p["wqt"], p["bqt"], p["wk"], p["bk"], p["wvt"], p["bvt"],
      p["wgat"], p["bgat"], p["w_attn_out"], tabs["cos_t"], tabs["sin_t"], tabs["cos_r"], tabs["sin_r"],
      kmeta, vmetat, p["sink_rows"])


def _merge_kernel(x_ref, h_ref, ya_ref, yb_ref, lng_ref, lnb_ref, *refs):
    wga_refs, wgb_refs = refs[:MERGE_BLOCKS], refs[MERGE_BLOCKS:2 * MERGE_BLOCKS]
    bmg_ref, wo_ref, bo_ref, g2_ref, b2_ref, out_ref = refs[2 * MERGE_BLOCKS:]
    h = h_ref[0]
    mixed = []
    for c in range(MERGE_BLOCKS):
        cs = slice(c * MERGE_COLS, (c + 1) * MERGE_COLS)
        gs = slice(D_MODEL + c * MERGE_COLS, D_MODEL + (c + 1) * MERGE_COLS)
        ga = _sigmoid(_dot(h, wga_refs[c][...]) + bmg_ref[:, cs])
        gb = _sigmoid(_dot(h, wgb_refs[c][...]) + bmg_ref[:, gs])
        mixed.append(ga * ya_ref[0, :, cs].astype(F32) + gb * yb_ref[0, :, cs].astype(F32))
    mixed = jnp.concatenate(mixed, axis=1)
    out = _dot(mixed.astype(BF16), wo_ref[...]) + bo_ref[...]
    h32 = _layer_norm(x_ref[0], lng_ref[...], lnb_ref[...])
    out_ref[0] = _layer_norm(DEEPNORM_ALPHA * h32 + out, g2_ref[...], b2_ref[...])


def _merge(x, h, ya, yb, p, *, tm):
    b, t_len, _ = x.shape
    tile = pl.BlockSpec((1, tm, D_MODEL), lambda bi, i: (bi, i, 0))
    w_in_block = lambda c: pl.BlockSpec((D_MODEL, MERGE_COLS), lambda bi, i: (0, c), pipeline_mode=pl.Buffered(1))
    in_specs = [tile, tile, tile, tile,
                _const_spec((1, D_MODEL)), _const_spec((1, D_MODEL)),
                *[w_in_block(OFF_G // MERGE_COLS + c) for c in range(MERGE_BLOCKS)],
                *[w_in_block((OFF_G + D_MODEL) // MERGE_COLS + c) for c in range(MERGE_BLOCKS)],
                _const_spec((1, 2 * D_MODEL)),
                _const_spec((D_MODEL, D_MODEL)), _const_spec((1, D_MODEL)),
                _const_spec((1, D_MODEL)), _const_spec((1, D_MODEL))]
    return pl.pallas_call(
        _merge_kernel,
        grid=(b, t_len // tm),
        in_specs=in_specs,
        out_specs=tile,
        out_shape=jax.ShapeDtypeStruct((b, t_len, D_MODEL), F32),
        compiler_params=pltpu.CompilerParams(dimension_semantics=("arbitrary", "arbitrary"),
                                             vmem_limit_bytes=V7X_VMEM_LIMIT),
        name="merge",
    )(x, h, ya, yb, p["ln_g"], p["ln_b"], *([p["w_in"]] * (2 * MERGE_BLOCKS)), p["bmg"], p["w_o"], p["b_o"],
      p["ln2_g"], p["ln2_b"])


def _rope_tables(t_total):
    inv = ROPE_THETA ** (-jnp.arange(HALF, dtype=F32) / HALF)
    ang = jnp.arange(t_total, dtype=F32)[:, None] * inv[None, :]
    cos, sin = jnp.cos(ang), jnp.sin(ang)
    zeros = jnp.zeros_like(cos)
    cos_r = jnp.concatenate([cos, cos, zeros, zeros], axis=1)
    sin_r = jnp.concatenate([-sin, sin, zeros, zeros], axis=1)
    return cos, sin, cos_r, sin_r


def _slotted(w):
    lead = w.shape[:-1]
    w4 = w.reshape(lead + (N_KV_HEADS, HEAD_DIM))
    w4 = jnp.concatenate([w4, jnp.zeros_like(w4)], axis=-1)
    return w4.reshape(lead + (N_KV_HEADS * KV_SLOT,))


def kernel(x, meta_tokens, ln_emb_g, ln_emb_b, w_in, b_in, conv_w, conv_b, w_ra, b_ra, w_ri, b_ri,
           lru_lambda, sinks, w_rnn_out, w_attn_out, w_o, b_o, ln_g, ln_b):
    b, seq, _ = x.shape
    assert b == V7X_SUBLANES and w_in.shape[0] == DEPTH
    nb, rb = N_RNN_BLOCKS, RNN_BLOCK
    w = w_in[0].astype(BF16)
    bi = b_in[0]
    row = lambda v: v.reshape(1, -1)
    blocks = lambda v: v.reshape(nb, 1, rb)
    lanes = lambda v: jnp.broadcast_to(v[:, None], (v.shape[0], V7X_LANES))

    common = {"ln_g": row(ln_emb_g), "ln_b": row(ln_emb_b)}
    p_rnn = dict(common, w_in=w,
                 bx=blocks(bi[:OFF_GR]), bg=blocks(bi[OFF_GR:OFF_Q]),
                 cw=conv_w[0].reshape(CONV_WIDTH, nb, rb).transpose(1, 0, 2), cb=blocks(conv_b[0]),
                 wra=w_ra[0].astype(BF16), bra=blocks(b_ra[0]),
                 wri=w_ri[0].astype(BF16), bri=blocks(b_ri[0]),
                 lam=blocks(lru_lambda[0]), w_rnn_out=w_rnn_out[0].astype(BF16))
    p_attn = dict(common,
                  wqt=w[:, OFF_Q:OFF_K].T, bqt=lanes(bi[OFF_Q:OFF_K]),
                  wk=_slotted(w[:, OFF_K:OFF_V]), bk=row(_slotted(bi[OFF_K:OFF_V])),
                  wvt=w[:, OFF_V:OFF_GA].T, bvt=lanes(bi[OFF_V:OFF_GA]),
                  wgat=w[:, OFF_GA:OFF_G].T, bgat=lanes(bi[OFF_GA:OFF_G]),
                  wv=w[:, OFF_V:OFF_GA], bv=row(bi[OFF_V:OFF_GA]),
                  w_attn_out=w_attn_out[0].astype(BF16),
                  sink_rows=jnp.repeat(sinks[0].astype(F32), BLOCK).reshape(N_KV_HEADS, 1, GROUP * BLOCK))
    p_merge = dict(common, w_in=w, bmg=row(bi[OFF_G:]),
                   w_o=w_o[0].astype(BF16), b_o=row(b_o[0]), ln2_g=row(ln_g[0]), ln2_b=row(ln_b[0]))

    cos, sin, cos_r, sin_r = _rope_tables(N_META + seq)
    tabs = {"cos_t": cos[N_META:].T, "sin_t": sin[N_META:].T, "cos_r": cos_r[N_META:], "sin_r": sin_r[N_META:]}

    meta = meta_tokens.astype(x.dtype)
    meta_b = jnp.broadcast_to(meta[None], (V7X_SUBLANES, N_META, D_MODEL))
    zero_hist = jnp.zeros((nb, HIST_ROWS, rb), F32)
    zero_state = jnp.zeros((nb, V7X_SUBLANES, rb), F32)
    _, hist0, state0 = _rnn_branch(meta_b, p_rnn, zero_hist, zero_state, tt=N_META, seq_start=True)
    kmeta, vmeta = _meta_kv(meta, p_attn, cos_r[:N_META], sin_r[:N_META])

    yb, h = _attn_branch(x, p_attn, tabs, kmeta, vmeta.T, tq=256)
    ya, _, _ = _rnn_branch(h, p_rnn, hist0, state0, tt=64, seq_start=False)
    return _merge(x, h, ya, yb, p_merge, tm=256)
```

```python
import functools

import jax
import jax.numpy as jnp
import numpy as np
from jax import lax
from jax.experimental import pallas as pl
from jax.experimental.pallas import tpu as pltpu

D_MODEL = 2048
N_META = 16
N_RNN_BLOCKS = 8
RNN_BLOCK = 256
CONV_WIDTH = 4
LRU_C = 8.0
HEAD_DIM = 64
HALF = HEAD_DIM // 2
N_Q_HEADS = 32
N_KV_HEADS = 4
GROUP = 8
D_KV = N_KV_HEADS * HEAD_DIM
BLOCK = 128
ROPE_THETA = 10000.0
NEG_INF = -1e30
LN_EPS = 1e-5
DEPTH = 1
DEEPNORM_ALPHA = (2.0 * DEPTH) ** 0.25
OFF_XR = 0
OFF_GR = D_MODEL
OFF_Q = 2 * D_MODEL
OFF_K = OFF_Q + D_MODEL
OFF_V = OFF_K + D_KV
OFF_GA = OFF_V + D_KV
OFF_G = OFF_GA + D_MODEL

V7X_SUBLANES = 8
V7X_LANES = 128
V7X_VMEM_LIMIT = 56 * 1024 * 1024

ATTN_TILE = 2 * BLOCK
RNN_TILE_STEPS = 64
MERGE_TILE = 256

KV_SLOT = V7X_LANES
HIST_ROWS = (CONV_WIDTH - 1) * V7X_SUBLANES
RNN_PERM_STEPS = 32
MERGE_COLS = 512
MERGE_BLOCKS = D_MODEL // MERGE_COLS

BF16 = jnp.bfloat16
F32 = jnp.float32


def _layer_norm(x, g, b):
    mu = jnp.mean(x, axis=-1, keepdims=True)
    xc = x - mu
    var = jnp.mean(xc * xc, axis=-1, keepdims=True)
    return xc * lax.rsqrt(var + LN_EPS) * g + b


def _sigmoid(x):
    return 1.0 / (1.0 + jnp.exp(-x))


def _dot(a, b):
    return jnp.dot(a, b, preferred_element_type=F32)


def _dot_nt(a, b):
    return lax.dot_general(a, b, (((1,), (1,)), ((), ())), preferred_element_type=F32)


def _const_spec(shape):
    nd = len(shape)
    return pl.BlockSpec(shape, lambda *_: (0,) * nd, pipeline_mode=pl.Buffered(1))


def _rnn_kernel(x_ref, lng_ref, lnb_ref, *refs, tt, seq_start, normalised_input):
    wx_refs, wg_refs = refs[:N_RNN_BLOCKS], refs[N_RNN_BLOCKS:2 * N_RNN_BLOCKS]
    (bx_ref, bg_ref, cw_ref, cb_ref, wra_ref, bra_ref, wri_ref, bri_ref, lam_ref, wout_ref, perm_ref,
     permt_ref, hist0_ref, state0_ref, ya_ref, hist_out_ref, state_out_ref,
     y_s, hist_s, state_s) = refs[2 * N_RNN_BLOCKS:]
    rows = tt * V7X_SUBLANES
    nb = N_RNN_BLOCKS
    j = pl.program_id(0)

    @pl.when(j == 0)
    def _():
        y_s[...] = jnp.zeros_like(y_s)
        hist_s[...] = hist0_ref[...]
        state_s[...] = state0_ref[...]

    ts = perm_ref.shape[0] // V7X_SUBLANES
    parts = []
    for s0 in range(0, tt, ts):
        hs_ = x_ref[:, s0:s0 + ts, :].reshape(ts * V7X_SUBLANES, D_MODEL)
        if not normalised_input:
            hs_ = _layer_norm(hs_, lng_ref[...], lnb_ref[...]).astype(BF16)
        parts.append(_dot(perm_ref[...], hs_).astype(BF16))
    h = jnp.concatenate(parts, axis=0)

    def project(n):
        return _dot(h, wx_refs[n][...]) + bx_ref[n], _dot(h, wg_refs[n][...]) + bg_ref[n]

    sub = ts * V7X_SUBLANES
    y_prev = y_s[...].astype(BF16)

    def project_out(n):
        cs = slice(n * RNN_BLOCK, (n + 1) * RNN_BLOCK)
        ya = _dot(y_prev, wout_ref[:, cs])
        for k, s0 in enumerate(range(0, tt, ts)):
            ya_ref[:, s0:s0 + ts, cs] = (ya[k * sub:(k + 1) * sub]
                                         .reshape(V7X_SUBLANES, ts, RNN_BLOCK).astype(ya_ref.dtype))

    ys = []
    nxt = project(0)
    for n in range(nb):
        xr, gr = nxt
        if n + 1 < nb:
            nxt = project(n + 1)
        project_out(n)
        xe = jnp.concatenate([hist_s[n], xr], axis=0)
        cw = cw_ref[n]
        conv = cb_ref[n] + cw[0:1] * xr
        for k in range(1, CONV_WIDTH):
            off = HIST_ROWS - k * V7X_SUBLANES
            conv = conv + cw[k:k + 1] * xe[off:off + rows]
        hist_s[n] = xe[rows:rows + HIST_ROWS]
        cb16 = conv.astype(BF16)
        gate_r = _sigmoid(_dot(cb16, wra_ref[n]) + bra_ref[n])
        gate_i = _sigmoid(_dot(cb16, wri_ref[n]) + bri_ref[n])
        lam = lam_ref[n]
        log_sig = jnp.minimum(lam, 0.0) - jnp.log1p(jnp.exp(-jnp.abs(lam)))
        a = jnp.exp((LRU_C * gate_r) * log_sig)
        mult = jnp.sqrt(1.0 - a * a)
        if seq_start:
            first = lax.broadcasted_iota(jnp.int32, (rows, RNN_BLOCK), 0) < V7X_SUBLANES
            mult = jnp.where(first & (j == 0), 1.0, mult)
        u = mult * gate_i * conv
        hcur = state_s[n]
        hs = []
        for t in range(tt):
            sl = slice(t * V7X_SUBLANES, (t + 1) * V7X_SUBLANES)
            hcur = a[sl] * hcur + u[sl]
            hs.append(hcur)
        state_s[n] = hcur
        hr = jnp.concatenate(hs, axis=0)
        ys.append((hr * (gr * _sigmoid(gr))).astype(BF16))
    y = jnp.concatenate(ys, axis=1)
    for r0 in range(0, rows, sub):
        y_s[r0:r0 + sub, :] = _dot(permt_ref[...], y[r0:r0 + sub])

    @pl.when(j == pl.num_programs(0) - 2)
    def _():
        hist_out_ref[...] = hist_s[...]
        state_out_ref[...] = state_s[...]


def _rnn_branch(x, p, hist0, state0, *, tt, seq_start):
    t_len = x.shape[1]
    nb, rb = N_RNN_BLOCKS, RNN_BLOCK
    rows = tt * V7X_SUBLANES
    ts = min(tt, RNN_PERM_STEPS)
    prow = ts * V7X_SUBLANES
    r = np.arange(prow)
    perm = np.zeros((prow, prow), np.float32)
    perm[r, (r % V7X_SUBLANES) * ts + r // V7X_SUBLANES] = 1.0
    perm, perm_t = jnp.asarray(perm, BF16), jnp.asarray(perm.T, BF16)
    assert x.dtype in (F32, BF16)
    kern = functools.partial(_rnn_kernel, tt=tt, seq_start=seq_start, normalised_input=x.dtype == BF16)
    n_tiles = t_len // tt
    w_in_block = lambda c: pl.BlockSpec((D_MODEL, rb), lambda j: (0, c), pipeline_mode=pl.Buffered(1))
    in_specs = [
        pl.BlockSpec((V7X_SUBLANES, tt, D_MODEL), lambda j: (0, jnp.minimum(j, n_tiles - 1), 0)),
        _const_spec((1, D_MODEL)), _const_spec((1, D_MODEL)),
        *[w_in_block(OFF_XR // rb + n) for n in range(nb)],
        *[w_in_block(OFF_GR // rb + n) for n in range(nb)],
        _const_spec((nb, 1, rb)), _const_spec((nb, 1, rb)),
        _const_spec((nb, CONV_WIDTH, rb)), _const_spec((nb, 1, rb)),
        _const_spec((nb, rb, rb)), _const_spec((nb, 1, rb)),
        _const_spec((nb, rb, rb)), _const_spec((nb, 1, rb)),
        _const_spec((nb, 1, rb)),
        _const_spec((D_MODEL, D_MODEL)),
        _const_spec((prow, prow)), _const_spec((prow, prow)),
        _const_spec((nb, HIST_ROWS, rb)), _const_spec((nb, V7X_SUBLANES, rb)),
    ]
    out_shape = (
        jax.ShapeDtypeStruct((V7X_SUBLANES, t_len, D_MODEL), BF16),
        jax.ShapeDtypeStruct((nb, HIST_ROWS, rb), F32),
        jax.ShapeDtypeStruct((nb, V7X_SUBLANES, rb), F32),
    )
    out_specs = (
        pl.BlockSpec((V7X_SUBLANES, tt, D_MODEL), lambda j: (0, jnp.maximum(j - 1, 0), 0)),
        pl.BlockSpec((nb, HIST_ROWS, rb), lambda j: (0, 0, 0)),
        pl.BlockSpec((nb, V7X_SUBLANES, rb), lambda j: (0, 0, 0)),
    )
    return pl.pallas_call(
        kern,
        grid=(n_tiles + 1,),
        in_specs=in_specs,
        out_specs=out_specs,
        out_shape=out_shape,
        scratch_shapes=[pltpu.VMEM((rows, D_MODEL), F32),
                        pltpu.VMEM((nb, HIST_ROWS, rb), F32), pltpu.VMEM((nb, V7X_SUBLANES, rb), F32)],
        compiler_params=pltpu.CompilerParams(dimension_semantics=("arbitrary",),
                                             vmem_limit_bytes=V7X_VMEM_LIMIT),
        name="rnn_branch_start" if seq_start else "rnn_branch",
    )(x, p["ln_g"], p["ln_b"], *([p["w_in"]] * (2 * nb)), p["bx"], p["bg"], p["cw"], p["cb"],
      p["wra"], p["bra"], p["wri"], p["bri"], p["lam"], p["w_rnn_out"], perm, perm_t, hist0, state0)


def _rope_rows(k, cos, sin_signed):
    lane = lax.broadcasted_iota(jnp.int32, k.shape, 1)
    partner = jnp.where((lane % HEAD_DIM) < HALF,
                        pltpu.roll(k, V7X_LANES - HALF, axis=1),
                        pltpu.roll(k, HALF, axis=1))
    return k * cos + partner * sin_signed


def _meta_kv_kernel(m_ref, lng_ref, lnb_ref, wk_ref, bk_ref, wv_ref, bv_ref, cos_ref, sin_ref,
                    k_ref, v_ref):
    h = _layer_norm(m_ref[...], lng_ref[...], lnb_ref[...]).astype(BF16)
    k = _dot(h, wk_ref[...]) + bk_ref[...]
    v = _dot(h, wv_ref[...]) + bv_ref[...]
    for g in range(N_KV_HEADS):
        sl = slice(g * KV_SLOT, (g + 1) * KV_SLOT)
        k_ref[:, sl] = _rope_rows(k[:, sl], cos_ref[...], sin_ref[...]).astype(BF16)
    v_ref[...] = v.astype(BF16)


def _meta_kv(meta, p, cos_rows, sin_rows):
    return pl.pallas_call(
        _meta_kv_kernel,
        out_shape=(jax.ShapeDtypeStruct((N_META, N_KV_HEADS * KV_SLOT), BF16),
                   jax.ShapeDtypeStruct((N_META, D_KV), BF16)),
        name="meta_kv",
    )(meta, p["ln_g"], p["ln_b"], p["wk"], p["bk"], p["wv"], p["bv"], cos_rows, sin_rows)


def _attn_kernel(x_ref, lng_ref, lnb_ref, wqt_ref, bqt_ref, wk_ref, bk_ref, wvt_ref, bvt_ref,
                 wgat_ref, bgat_ref, wout_ref, cost_ref, sint_ref, cosr_ref, sinr_ref,
                 kmeta_ref, vmetat_ref, sink_ref, yb_ref, h_ref,
                 qt_s, k_s, vt_s, gat_s, kprev_s, vprev_s, ot_s, act_s, *, tq, n_t):
    nblk = tq // BLOCK
    rows_g = GROUP * HEAD_DIM
    j = pl.program_id(0)

    @pl.when(j == 0)
    def _():
        for ref in (qt_s, k_s, vt_s, gat_s, kprev_s, vprev_s, act_s):
            ref[...] = jnp.zeros_like(ref)

    h = _layer_norm(x_ref[0], lng_ref[...], lnb_ref[...]).astype(BF16)
    h_ref[0] = h

    act_prev = act_s[...].astype(BF16)
    out_cols = D_MODEL // (N_KV_HEADS * nblk)

    def project_out(c):
        cs = slice(c * out_cols, (c + 1) * out_cols)
        yb_ref[0, :, cs] = _dot(act_prev, wout_ref[:, cs]).astype(yb_ref.dtype)

    def lanes_bias(b):
        return jnp.concatenate([b] * (tq // V7X_LANES), axis=1)

    scale = HEAD_DIM ** -0.5
    cos = cost_ref[...] * scale
    sin = sint_ref[...] * scale

    def project_q(g):
        rs = slice(g * rows_g, (g + 1) * rows_g)
        qt = _dot_nt(wqt_ref[rs, :], h) + lanes_bias(bqt_ref[rs, :])
        out = []
        for hh in range(GROUP):
            r0 = hh * HEAD_DIM
            q1, q2 = qt[r0:r0 + HALF], qt[r0 + HALF:r0 + HEAD_DIM]
            out.append((q1 * cos - q2 * sin, q2 * cos + q1 * sin))
        return out

    def store_q(g, roped):
        for hh, (lo, hi) in enumerate(roped):
            r0 = g * rows_g + hh * HEAD_DIM
            qt_s[r0:r0 + HALF, :] = lo
            qt_s[r0 + HALF:r0 + HEAD_DIM, :] = hi

    def project_gate(g):
        rs = slice(g * rows_g, (g + 1) * rows_g)
        gat = _dot_nt(wgat_ref[rs, :], h) + lanes_bias(bgat_ref[rs, :])
        return gat * _sigmoid(gat)

    i = jnp.maximum(j - 1, 0) % n_t
    k = k_s[...].astype(BF16)
    vt = vt_s[...].astype(BF16)
    k_carry, vt_carry = kprev_s[...].astype(BF16), vprev_s[...].astype(BF16)
    kmeta = kmeta_ref[...]
    vmetat = vmetat_ref[...]

    key_row = lax.broadcasted_iota(jnp.int32, (BLOCK, GROUP * BLOCK), 0)
    qry_col = lax.broadcasted_iota(jnp.int32, (BLOCK, GROUP * BLOCK), 1) % BLOCK
    cur_ok = key_row <= qry_col
    prev_in_window = key_row > qry_col
    first_prev_ok = prev_in_window & (i > 0)

    def scores(g, jb):
        tok = slice(jb * BLOCK, (jb + 1) * BLOCK)
        ks = slice(g * KV_SLOT, g * KV_SLOT + HEAD_DIM)
        k_prev = k_carry if jb == 0 else k[(jb - 1) * BLOCK:jb * BLOCK]
        q_g = jnp.concatenate(
            [qt_s[(g * GROUP + hh) * HEAD_DIM:(g * GROUP + hh + 1) * HEAD_DIM, tok].astype(BF16)
             for hh in range(GROUP)], axis=1)
        s_prev = _dot(k_prev[:, ks], q_g)
        if jb == 0:
            s_prev = jnp.where(first_prev_ok, s_prev, NEG_INF)
        s_cm = _dot(jnp.concatenate([k[tok, ks], kmeta[:, ks]], axis=0), q_g)
        s_band = jnp.where(cur_ok, s_cm[:BLOCK], s_prev)
        return s_band, s_cm[BLOCK:]

    def finish(g, jb, s):
        s_band, s_meta = s
        tok = slice(jb * BLOCK, (jb + 1) * BLOCK)
        vs = slice(g * HEAD_DIM, (g + 1) * HEAD_DIM)
        vt_prev = vt_carry if jb == 0 else vt[:, (jb - 1) * BLOCK:jb * BLOCK]
        sink = sink_ref[g]
        m = jnp.maximum(jnp.maximum(jnp.max(s_band, axis=0, keepdims=True),
                                    jnp.max(s_meta, axis=0, keepdims=True)), sink)
        p_band = jnp.exp(s_band - m)
        p_meta = jnp.exp(s_meta - m)
        denom = (jnp.sum(p_band, axis=0, keepdims=True) + jnp.sum(p_meta, axis=0, keepdims=True)
                 + jnp.exp(sink - m))
        p_cur = jnp.where(cur_ok, p_band, 0.0)
        p_prev = jnp.where(cur_ok, 0.0, p_band)
        p_cm = jnp.concatenate([p_cur, p_meta], axis=0).astype(BF16)
        v_cm = jnp.concatenate([vt[vs, tok], vmetat[vs]], axis=1)
        o = _dot(vt_prev[vs], p_prev.astype(BF16)) + _dot(v_cm, p_cm)
        o = o * (1.0 / denom)
        for hh in range(GROUP):
            r0 = (g * GROUP + hh) * HEAD_DIM
            ot_s[r0:r0 + HEAD_DIM, tok] = o[:, hh * BLOCK:(hh + 1) * BLOCK]

    gates = [None] * N_KV_HEADS
    held_q = None
    for g in range(N_KV_HEADS):
        pg = (g - 1) % N_KV_HEADS
        for jb in range(nblk):
            s = scores(g, jb)
            if jb == 0:
                roped = project_q(pg)
                if pg < g:
                    store_q(pg, roped)
                else:
                    held_q = roped
            elif jb == 1:
                gates[pg] = project_gate(pg)
            project_out(g * nblk + jb)
            finish(g, jb, s)
    store_q(N_KV_HEADS - 1, held_q)

    k_new = _dot(h, wk_ref[...]) + bk_ref[...]
    vt_new = _dot_nt(wvt_ref[...], h) + lanes_bias(bvt_ref[...])

    act_s[...] = (ot_s[...] * gat_s[...]).T

    kprev_s[...] = k_s[tq - BLOCK:, :]
    vprev_s[...] = vt_s[:, tq - BLOCK:]
    for g in range(N_KV_HEADS):
        sl = slice(g * KV_SLOT, (g + 1) * KV_SLOT)
        k_s[:, sl] = _rope_rows(k_new[:, sl], cosr_ref[...], sinr_ref[...])
        gat_s[g * rows_g:(g + 1) * rows_g, :] = gates[g]
    vt_s[...] = vt_new


def _attn_branch(x, p, tabs, kmeta, vmetat, *, tq):
    b, t_len, _ = x.shape
    assert tq == 2 * BLOCK
    n_t = t_len // tq
    n_tiles = b * n_t
    kern = functools.partial(_attn_kernel, tq=tq, n_t=n_t)
    proj_tile = lambda j: jnp.minimum(j, n_tiles - 1)
    out_tile = lambda j: jnp.maximum(j - 2, 0)
    in_specs = [
        pl.BlockSpec((1, tq, D_MODEL), lambda j: (proj_tile(j) // n_t, proj_tile(j) % n_t, 0)),
        _const_spec((1, D_MODEL)), _const_spec((1, D_MODEL)),
        _const_spec((D_MODEL, D_MODEL)), _const_spec((D_MODEL, V7X_LANES)),
        _const_spec((D_MODEL, N_KV_HEADS * KV_SLOT)), _const_spec((1, N_KV_HEADS * KV_SLOT)),
        _const_spec((D_KV, D_MODEL)), _const_spec((D_KV, V7X_LANES)),
        _const_spec((D_MODEL, D_MODEL)), _const_spec((D_MODEL, V7X_LANES)),
        _const_spec((D_MODEL, D_MODEL)),
        pl.BlockSpec((HALF, tq), lambda j: (0, proj_tile(j) % n_t)),
        pl.BlockSpec((HALF, tq), lambda j: (0, proj_tile(j) % n_t)),
        pl.BlockSpec((tq, KV_SLOT), lambda j: (proj_tile(j) % n_t, 0)),
        pl.BlockSpec((tq, KV_SLOT), lambda j: (proj_tile(j) % n_t, 0)),
        _const_spec((N_META, N_KV_HEADS * KV_SLOT)), _const_spec((D_KV, N_META)),
        _const_spec((N_KV_HEADS, 1, GROUP * BLOCK)),
    ]
    return pl.pallas_call(
        kern,
        grid=(n_tiles + 2,),
        in_specs=in_specs,
        out_specs=(pl.BlockSpec((1, tq, D_MODEL), lambda j: (out_tile(j) // n_t, out_tile(j) % n_t, 0)),
                   pl.BlockSpec((1, tq, D_MODEL), lambda j: (proj_tile(j) // n_t, proj_tile(j) % n_t, 0))),
        out_shape=(jax.ShapeDtypeStruct((b, t_len, D_MODEL), BF16),
                   jax.ShapeDtypeStruct((b, t_len, D_MODEL), BF16)),
        scratch_shapes=[pltpu.VMEM((D_MODEL, tq), F32),
                        pltpu.VMEM((tq, N_KV_HEADS * KV_SLOT), F32),
                        pltpu.VMEM((D_KV, tq), F32),
                        pltpu.VMEM((D_MODEL, tq), F32),
                        pltpu.VMEM((BLOCK, N_KV_HEADS * KV_SLOT), F32),
                        pltpu.VMEM((D_KV, BLOCK), F32),
                        pltpu.VMEM((D_MODEL, tq), F32),
                        pltpu.VMEM((tq, D_MODEL), F32)],
        compiler_params=pltpu.CompilerParams(dimension_semantics=("arbitrary",),
                                             vmem_limit_bytes=V7X_VMEM_LIMIT),
        name="attn_branch",
    )(x, p["ln_g"], p["ln_b"], p["wqt"], p["bqt"], p["wk"], p["bk"], p["wvt"], p["bvt"],
      p["wgat"], p["bgat"], p["w_attn_out"], tabs["cos_t"], tabs["sin_t"], tabs["cos_r"], tabs["sin_r"],
      kmeta, vmetat, p["sink_rows"])


def _merge_kernel(x_ref, h_ref, ya_ref, yb_ref, lng_ref, lnb_ref, *refs):
    wga_refs, wgb_refs = refs[:MERGE_BLOCKS], refs[MERGE_BLOCKS:2 * MERGE_BLOCKS]
    bmg_ref, wo_ref, bo_ref, g2_ref, b2_ref, out_ref = refs[2 * MERGE_BLOCKS:]
    h = h_ref[0]
    mixed = []
    for c in range(MERGE_BLOCKS):
        cs = slice(c * MERGE_COLS, (c + 1) * MERGE_COLS)
        gs = slice(D_MODEL + c * MERGE_COLS, D_MODEL + (c + 1) * MERGE_COLS)
        ga = _sigmoid(_dot(h, wga_refs[c][...]) + bmg_ref[:, cs])
        gb = _sigmoid(_dot(h, wgb_refs[c][...]) + bmg_ref[:, gs])
        mixed.append(ga * ya_ref[0, :, cs].astype(F32) + gb * yb_ref[0, :, cs].astype(F32))
    mixed = jnp.concatenate(mixed, axis=1)
    out = _dot(mixed.astype(BF16), wo_ref[...]) + bo_ref[...]
    h32 = _layer_norm(x_ref[0], lng_ref[...], lnb_ref[...])
    out_ref[0] = _layer_norm(DEEPNORM_ALPHA * h32 + out, g2_ref[...], b2_ref[...])


def _merge(x, h, ya, yb, p, *, tm):
    b, t_len, _ = x.shape
    tile = pl.BlockSpec((1, tm, D_MODEL), lambda bi, i: (bi, i, 0))
    w_in_block = lambda c: pl.BlockSpec((D_MODEL, MERGE_COLS), lambda bi, i: (0, c), pipeline_mode=pl.Buffered(1))
    in_specs = [tile, tile, tile, tile,
                _const_spec((1, D_MODEL)), _const_spec((1, D_MODEL)),
                *[w_in_block(OFF_G // MERGE_COLS + c) for c in range(MERGE_BLOCKS)],
                *[w_in_block((OFF_G + D_MODEL) // MERGE_COLS + c) for c in range(MERGE_BLOCKS)],
                _const_spec((1, 2 * D_MODEL)),
                _const_spec((D_MODEL, D_MODEL)), _const_spec((1, D_MODEL)),
                _const_spec((1, D_MODEL)), _const_spec((1, D_MODEL))]
    return pl.pallas_call(
        _merge_kernel,
        grid=(b, t_len // tm),
        in_specs=in_specs,
        out_specs=tile,
        out_shape=jax.ShapeDtypeStruct((b, t_len, D_MODEL), F32),
        compiler_params=pltpu.CompilerParams(dimension_semantics=("arbitrary", "arbitrary"),
                                             vmem_limit_bytes=V7X_VMEM_LIMIT),
        name="merge",
    )(x, h, ya, yb, p["ln_g"], p["ln_b"], *([p["w_in"]] * (2 * MERGE_BLOCKS)), p["bmg"], p["w_o"], p["b_o"],
      p["ln2_g"], p["ln2_b"])


def _rope_tables(t_total):
    inv = (ROPE_THETA ** (-np.arange(HALF, dtype=np.float32) / HALF)).astype(np.float32)
    ang = np.arange(t_total, dtype=np.float32)[:, None] * inv[None, :]
    cos, sin = np.cos(ang), np.sin(ang)
    zeros = np.zeros_like(cos)
    cos_r = np.concatenate([cos, cos, zeros, zeros], axis=1)
    sin_r = np.concatenate([-sin, sin, zeros, zeros], axis=1)
    return cos, sin, cos_r, sin_r


def _slotted(w):
    lead = w.shape[:-1]
    w4 = w.reshape(lead + (N_KV_HEADS, HEAD_DIM))
    w4 = jnp.concatenate([w4, jnp.zeros_like(w4)], axis=-1)
    return w4.reshape(lead + (N_KV_HEADS * KV_SLOT,))


def kernel(x, meta_tokens, ln_emb_g, ln_emb_b, w_in, b_in, conv_w, conv_b, w_ra, b_ra, w_ri, b_ri,
           lru_lambda, sinks, w_rnn_out, w_attn_out, w_o, b_o, ln_g, ln_b):
    b, seq, _ = x.shape
    assert b == V7X_SUBLANES and w_in.shape[0] == DEPTH
    nb, rb = N_RNN_BLOCKS, RNN_BLOCK
    w = w_in[0].astype(BF16)
    bi = b_in[0]
    row = lambda v: v.reshape(1, -1)
    blocks = lambda v: v.reshape(nb, 1, rb)
    lanes = lambda v: jnp.broadcast_to(v[:, None], (v.shape[0], V7X_LANES))

    common = {"ln_g": row(ln_emb_g), "ln_b": row(ln_emb_b)}
    p_rnn = dict(common, w_in=w,
                 bx=blocks(bi[:OFF_GR]), bg=blocks(bi[OFF_GR:OFF_Q]),
                 cw=conv_w[0].reshape(CONV_WIDTH, nb, rb).transpose(1, 0, 2), cb=blocks(conv_b[0]),
                 wra=w_ra[0].astype(BF16), bra=blocks(b_ra[0]),
                 wri=w_ri[0].astype(BF16), bri=blocks(b_ri[0]),
                 lam=blocks(lru_lambda[0]), w_rnn_out=w_rnn_out[0].astype(BF16))
    p_attn = dict(common,
                  wqt=w[:, OFF_Q:OFF_K].T, bqt=lanes(bi[OFF_Q:OFF_K]),
                  wk=_slotted(w[:, OFF_K:OFF_V]), bk=row(_slotted(bi[OFF_K:OFF_V])),
                  wvt=w[:, OFF_V:OFF_GA].T, bvt=lanes(bi[OFF_V:OFF_GA]),
                  wgat=w[:, OFF_GA:OFF_G].T, bgat=lanes(bi[OFF_GA:OFF_G]),
                  wv=w[:, OFF_V:OFF_GA], bv=row(bi[OFF_V:OFF_GA]),
                  w_attn_out=w_attn_out[0].astype(BF16),
                  sink_rows=jnp.repeat(sinks[0].astype(F32), BLOCK).reshape(N_KV_HEADS, 1, GROUP * BLOCK))
    p_merge = dict(common, w_in=w, bmg=row(bi[OFF_G:]),
                   w_o=w_o[0].astype(BF16), b_o=row(b_o[0]), ln2_g=row(ln_g[0]), ln2_b=row(ln_b[0]))

    cos, sin, cos_r, sin_r = _rope_tables(N_META + seq)
    tabs = {"cos_t": np.ascontiguousarray(cos[N_META:].T), "sin_t": np.ascontiguousarray(sin[N_META:].T),
            "cos_r": cos_r[N_META:], "sin_r": sin_r[N_META:]}

    meta = meta_tokens.astype(x.dtype)
    meta_b = jnp.broadcast_to(meta[None], (V7X_SUBLANES, N_META, D_MODEL))
    zero_hist = jnp.zeros((nb, HIST_ROWS, rb), F32)
    zero_state = jnp.zeros((nb, V7X_SUBLANES, rb), F32)
    _, hist0, state0 = _rnn_branch(meta_b, p_rnn, zero_hist, zero_state, tt=N_META, seq_start=True)
    kmeta, vmeta = _meta_kv(meta, p_attn, cos_r[:N_META], sin_r[:N_META])

    yb, h = _attn_branch(x, p_attn, tabs, kmeta, vmeta.T, tq=ATTN_TILE)
    ya, _, _ = _rnn_branch(h, p_rnn, hist0, state0, tt=RNN_TILE_STEPS, seq_start=False)
    return _merge(x, h, ya, yb, p_merge, tm=MERGE_TILE)
```

```python
import functools

import jax
import jax.numpy as jnp
import numpy as np
from jax import lax
from jax.experimental import pallas as pl
from jax.experimental.pallas import tpu as pltpu

D_MODEL = 2048
N_META = 16
N_RNN_BLOCKS = 8
RNN_BLOCK = 256
CONV_WIDTH = 4
LRU_C = 8.0
HEAD_DIM = 64
HALF = HEAD_DIM // 2
N_Q_HEADS = 32
N_KV_HEADS = 4
GROUP = 8
D_KV = N_KV_HEADS * HEAD_DIM
BLOCK = 128
ROPE_THETA = 10000.0
NEG_INF = -1e30
LN_EPS = 1e-5
DEPTH = 1
DEEPNORM_ALPHA = (2.0 * DEPTH) ** 0.25
OFF_XR = 0
OFF_GR = D_MODEL
OFF_Q = 2 * D_MODEL
OFF_K = OFF_Q + D_MODEL
OFF_V = OFF_K + D_KV
OFF_GA = OFF_V + D_KV
OFF_G = OFF_GA + D_MODEL

V7X_SUBLANES = 8
V7X_LANES = 128
V7X_VMEM_LIMIT = 56 * 1024 * 1024

ATTN_TILE = 2 * BLOCK
RNN_TILE_STEPS = 64
MERGE_TILE = 256

KV_SLOT = V7X_LANES
HIST_ROWS = (CONV_WIDTH - 1) * V7X_SUBLANES
RNN_PERM_STEPS = 32
MERGE_COLS = 512
MERGE_BLOCKS = D_MODEL // MERGE_COLS

BF16 = jnp.bfloat16
F32 = jnp.float32


def _layer_norm(x, g, b):
    mu = jnp.mean(x, axis=-1, keepdims=True)
    xc = x - mu
    var = jnp.mean(xc * xc, axis=-1, keepdims=True)
    return xc * lax.rsqrt(var + LN_EPS) * g + b


def _sigmoid(x):
    return 1.0 / (1.0 + jnp.exp(-x))


def _dot(a, b):
    return jnp.dot(a, b, preferred_element_type=F32)


def _dot_nt(a, b):
    return lax.dot_general(a, b, (((1,), (1,)), ((), ())), preferred_element_type=F32)


def _const_spec(shape):
    nd = len(shape)
    return pl.BlockSpec(shape, lambda *_: (0,) * nd, pipeline_mode=pl.Buffered(1))


def _rnn_kernel(x_ref, lng_ref, lnb_ref, *refs, tt, seq_start, normalised_input):
    wx_refs, wg_refs = refs[:N_RNN_BLOCKS], refs[N_RNN_BLOCKS:2 * N_RNN_BLOCKS]
    (bx_ref, bg_ref, cw_ref, cb_ref, wra_ref, bra_ref, wri_ref, bri_ref, lam_ref, wout_ref, perm_ref,
     permt_ref, hist0_ref, state0_ref, ya_ref, hist_out_ref, state_out_ref,
     y_s, hist_s, state_s) = refs[2 * N_RNN_BLOCKS:]
    rows = tt * V7X_SUBLANES
    nb = N_RNN_BLOCKS
    j = pl.program_id(0)

    @pl.when(j == 0)
    def _():
        y_s[...] = jnp.zeros_like(y_s)
        hist_s[...] = hist0_ref[...]
        state_s[...] = state0_ref[...]

    ts = perm_ref.shape[0] // V7X_SUBLANES
    parts = []
    for s0 in range(0, tt, ts):
        hs_ = x_ref[:, s0:s0 + ts, :].reshape(ts * V7X_SUBLANES, D_MODEL)
        if not normalised_input:
            hs_ = _layer_norm(hs_, lng_ref[...], lnb_ref[...]).astype(BF16)
        parts.append(_dot(perm_ref[...], hs_).astype(BF16))
    h = jnp.concatenate(parts, axis=0)

    def project(n):
        return _dot(h, wx_refs[n][...]) + bx_ref[n], _dot(h, wg_refs[n][...]) + bg_ref[n]

    sub = ts * V7X_SUBLANES
    y_prev = y_s[...].astype(BF16)

    def project_out(n):
        cs = slice(n * RNN_BLOCK, (n + 1) * RNN_BLOCK)
        ya = _dot(y_prev, wout_ref[:, cs])
        for k, s0 in enumerate(range(0, tt, ts)):
            ya_ref[:, s0:s0 + ts, cs] = (ya[k * sub:(k + 1) * sub]
                                         .reshape(V7X_SUBLANES, ts, RNN_BLOCK).astype(ya_ref.dtype))

    ys = []
    nxt = project(0)
    for n in range(nb):
        xr, gr = nxt
        if n + 1 < nb:
            nxt = project(n + 1)
        project_out(n)
        xe = jnp.concatenate([hist_s[n], xr], axis=0)
        cw = cw_ref[n]
        conv = cb_ref[n] + cw[0:1] * xr
        for k in range(1, CONV_WIDTH):
            off = HIST_ROWS - k * V7X_SUBLANES
            conv = conv + cw[k:k + 1] * xe[off:off + rows]
        hist_s[n] = xe[rows:rows + HIST_ROWS]
        cb16 = conv.astype(BF16)
        gate_r = _sigmoid(_dot(cb16, wra_ref[n]) + bra_ref[n])
        gate_i = _sigmoid(_dot(cb16, wri_ref[n]) + bri_ref[n])
        lam = lam_ref[n]
        log_sig = jnp.minimum(lam, 0.0) - jnp.log1p(jnp.exp(-jnp.abs(lam)))
        a = jnp.exp((LRU_C * gate_r) * log_sig)
        mult = jnp.sqrt(1.0 - a * a)
        if seq_start:
            first = lax.broadcasted_iota(jnp.int32, (rows, RNN_BLOCK), 0) < V7X_SUBLANES
            mult = jnp.where(first & (j == 0), 1.0, mult)
        u = mult * gate_i * conv
        hcur = state_s[n]
        hs = []
        for t in range(tt):
            sl = slice(t * V7X_SUBLANES, (t + 1) * V7X_SUBLANES)
            hcur = a[sl] * hcur + u[sl]
            hs.append(hcur)
        state_s[n] = hcur
        hr = jnp.concatenate(hs, axis=0)
        ys.append((hr * (gr * _sigmoid(gr))).astype(BF16))
    y = jnp.concatenate(ys, axis=1)
    for r0 in range(0, rows, sub):
        y_s[r0:r0 + sub, :] = _dot(permt_ref[...], y[r0:r0 + sub])

    @pl.when(j == pl.num_programs(0) - 2)
    def _():
        hist_out_ref[...] = hist_s[...]
        state_out_ref[...] = state_s[...]


def _rnn_branch(x, p, hist0, state0, *, tt, seq_start):
    t_len = x.shape[1]
    nb, rb = N_RNN_BLOCKS, RNN_BLOCK
    rows = tt * V7X_SUBLANES
    ts = min(tt, RNN_PERM_STEPS)
    prow = ts * V7X_SUBLANES
    r = np.arange(prow)
    perm = np.zeros((prow, prow), np.float32)
    perm[r, (r % V7X_SUBLANES) * ts + r // V7X_SUBLANES] = 1.0
    perm, perm_t = jnp.asarray(perm, BF16), jnp.asarray(perm.T, BF16)
    assert x.dtype in (F32, BF16)
    kern = functools.partial(_rnn_kernel, tt=tt, seq_start=seq_start, normalised_input=x.dtype == BF16)
    n_tiles = t_len // tt
    w_in_block = lambda c: pl.BlockSpec((D_MODEL, rb), lambda j: (0, c), pipeline_mode=pl.Buffered(1))
    in_specs = [
        pl.BlockSpec((V7X_SUBLANES, tt, D_MODEL), lambda j: (0, jnp.minimum(j, n_tiles - 1), 0)),
        _const_spec((1, D_MODEL)), _const_spec((1, D_MODEL)),
        *[w_in_block(OFF_XR // rb + n) for n in range(nb)],
        *[w_in_block(OFF_GR // rb + n) for n in range(nb)],
        _const_spec((nb, 1, rb)), _const_spec((nb, 1, rb)),
        _const_spec((nb, CONV_WIDTH, rb)), _const_spec((nb, 1, rb)),
        _const_spec((nb, rb, rb)), _const_spec((nb, 1, rb)),
        _const_spec((nb, rb, rb)), _const_spec((nb, 1, rb)),
        _const_spec((nb, 1, rb)),
        _const_spec((D_MODEL, D_MODEL)),
        _const_spec((prow, prow)), _const_spec((prow, prow)),
        _const_spec((nb, HIST_ROWS, rb)), _const_spec((nb, V7X_SUBLANES, rb)),
    ]
    out_shape = (
        jax.ShapeDtypeStruct((V7X_SUBLANES, t_len, D_MODEL), BF16),
        jax.ShapeDtypeStruct((nb, HIST_ROWS, rb), F32),
        jax.ShapeDtypeStruct((nb, V7X_SUBLANES, rb), F32),
    )
    out_specs = (
        pl.BlockSpec((V7X_SUBLANES, tt, D_MODEL), lambda j: (0, jnp.maximum(j - 1, 0), 0)),
        pl.BlockSpec((nb, HIST_ROWS, rb), lambda j: (0, 0, 0)),
        pl.BlockSpec((nb, V7X_SUBLANES, rb), lambda j: (0, 0, 0)),
    )
    return pl.pallas_call(
        kern,
        grid=(n_tiles + 1,),
        in_specs=in_specs,
        out_specs=out_specs,
        out_shape=out_shape,
        scratch_shapes=[pltpu.VMEM((rows, D_MODEL), F32),
                        pltpu.VMEM((nb, HIST_ROWS, rb), F32), pltpu.VMEM((nb, V7X_SUBLANES, rb), F32)],
        compiler_params=pltpu.CompilerParams(dimension_semantics=("arbitrary",),
                                             vmem_limit_bytes=V7X_VMEM_LIMIT),
        name="rnn_branch_start" if seq_start else "rnn_branch",
    )(x, p["ln_g"], p["ln_b"], *([p["w_in"]] * (2 * nb)), p["bx"], p["bg"], p["cw"], p["cb"],
      p["wra"], p["bra"], p["wri"], p["bri"], p["lam"], p["w_rnn_out"], perm, perm_t, hist0, state0)


def _rope_rows(k, cos, sin_signed):
    lane = lax.broadcasted_iota(jnp.int32, k.shape, 1)
    partner = jnp.where((lane % HEAD_DIM) < HALF,
                        pltpu.roll(k, V7X_LANES - HALF, axis=1),
                        pltpu.roll(k, HALF, axis=1))
    return k * cos + partner * sin_signed


def _meta_kv_kernel(m_ref, lng_ref, lnb_ref, wk_ref, bk_ref, wv_ref, bv_ref, cos_ref, sin_ref,
                    k_ref, v_ref):
    h = _layer_norm(m_ref[...], lng_ref[...], lnb_ref[...]).astype(BF16)
    k = _dot(h, wk_ref[...]) + bk_ref[...]
    v = _dot(h, wv_ref[...]) + bv_ref[...]
    for g in range(N_KV_HEADS):
        sl = slice(g * KV_SLOT, (g + 1) * KV_SLOT)
        k_ref[:, sl] = _rope_rows(k[:, sl], cos_ref[...], sin_ref[...]).astype(BF16)
    v_ref[...] = v.astype(BF16)


def _meta_kv(meta, p, cos_rows, sin_rows):
    return pl.pallas_call(
        _meta_kv_kernel,
        out_shape=(jax.ShapeDtypeStruct((N_META, N_KV_HEADS * KV_SLOT), BF16),
                   jax.ShapeDtypeStruct((N_META, D_KV), BF16)),
        name="meta_kv",
    )(meta, p["ln_g"], p["ln_b"], p["wk"], p["bk"], p["wv"], p["bv"], cos_rows, sin_rows)


def _attn_kernel(x_ref, lng_ref, lnb_ref, wqt_ref, bqt_ref, wk_ref, bk_ref, wvt_ref, bvt_ref,
                 wgat_ref, bgat_ref, wout_ref, cost_ref, sint_ref, cosr_ref, sinr_ref,
                 kmeta_ref, vmetat_ref, sink_ref, yb_ref, h_ref, h32_ref,
                 qt_s, k_s, vt_s, gat_s, kprev_s, vprev_s, ot_s, act_s, *, tq, n_t):
    nblk = tq // BLOCK
    rows_g = GROUP * HEAD_DIM
    j = pl.program_id(0)

    @pl.when(j == 0)
    def _():
        for ref in (qt_s, k_s, vt_s, gat_s, kprev_s, vprev_s, act_s):
            ref[...] = jnp.zeros_like(ref)

    h32 = _layer_norm(x_ref[0], lng_ref[...], lnb_ref[...])
    h = h32.astype(BF16)
    h32_ref[0] = h32
    h_ref[0] = h

    act_prev = act_s[...].astype(BF16)
    out_cols = D_MODEL // (N_KV_HEADS * nblk)

    def project_out(c):
        cs = slice(c * out_cols, (c + 1) * out_cols)
        yb_ref[0, :, cs] = _dot(act_prev, wout_ref[:, cs]).astype(yb_ref.dtype)

    def lanes_bias(b):
        return jnp.concatenate([b] * (tq // V7X_LANES), axis=1)

    scale = HEAD_DIM ** -0.5
    cos = cost_ref[...] * scale
    sin = sint_ref[...] * scale

    def project_q(g):
        rs = slice(g * rows_g, (g + 1) * rows_g)
        qt = _dot_nt(wqt_ref[rs, :], h) + lanes_bias(bqt_ref[rs, :])
        out = []
        for hh in range(GROUP):
            r0 = hh * HEAD_DIM
            q1, q2 = qt[r0:r0 + HALF], qt[r0 + HALF:r0 + HEAD_DIM]
            out.append((q1 * cos - q2 * sin, q2 * cos + q1 * sin))
        return out

    def store_q(g, roped):
        for hh, (lo, hi) in enumerate(roped):
            r0 = g * rows_g + hh * HEAD_DIM
            qt_s[r0:r0 + HALF, :] = lo
            qt_s[r0 + HALF:r0 + HEAD_DIM, :] = hi

    def project_gate(g):
        rs = slice(g * rows_g, (g + 1) * rows_g)
        gat = _dot_nt(wgat_ref[rs, :], h) + lanes_bias(bgat_ref[rs, :])
        return gat * _sigmoid(gat)

    i = jnp.maximum(j - 1, 0) % n_t
    k = k_s[...].astype(BF16)
    vt = vt_s[...].astype(BF16)
    k_carry, vt_carry = kprev_s[...].astype(BF16), vprev_s[...].astype(BF16)
    kmeta = kmeta_ref[...]
    vmetat = vmetat_ref[...]

    key_row = lax.broadcasted_iota(jnp.int32, (BLOCK, GROUP * BLOCK), 0)
    qry_col = lax.broadcasted_iota(jnp.int32, (BLOCK, GROUP * BLOCK), 1) % BLOCK
    cur_ok = key_row <= qry_col
    prev_in_window = key_row > qry_col
    first_prev_ok = prev_in_window & (i > 0)

    def scores(g, jb):
        tok = slice(jb * BLOCK, (jb + 1) * BLOCK)
        ks = slice(g * KV_SLOT, g * KV_SLOT + HEAD_DIM)
        k_prev = k_carry if jb == 0 else k[(jb - 1) * BLOCK:jb * BLOCK]
        q_g = jnp.concatenate(
            [qt_s[(g * GROUP + hh) * HEAD_DIM:(g * GROUP + hh + 1) * HEAD_DIM, tok].astype(BF16)
             for hh in range(GROUP)], axis=1)
        s_prev = _dot(k_prev[:, ks], q_g)
        if jb == 0:
            s_prev = jnp.where(first_prev_ok, s_prev, NEG_INF)
        s_cm = _dot(jnp.concatenate([k[tok, ks], kmeta[:, ks]], axis=0), q_g)
        s_band = jnp.where(cur_ok, s_cm[:BLOCK], s_prev)
        return s_band, s_cm[BLOCK:]

    def finish(g, jb, s):
        s_band, s_meta = s
        tok = slice(jb * BLOCK, (jb + 1) * BLOCK)
        vs = slice(g * HEAD_DIM, (g + 1) * HEAD_DIM)
        vt_prev = vt_carry if jb == 0 else vt[:, (jb - 1) * BLOCK:jb * BLOCK]
        sink = sink_ref[g]
        m = jnp.maximum(jnp.maximum(jnp.max(s_band, axis=0, keepdims=True),
                                    jnp.max(s_meta, axis=0, keepdims=True)), sink)
        p_band = jnp.exp(s_band - m)
        p_meta = jnp.exp(s_meta - m)
        denom = (jnp.sum(p_band, axis=0, keepdims=True) + jnp.sum(p_meta, axis=0, keepdims=True)
                 + jnp.exp(sink - m))
        p_cur = jnp.where(cur_ok, p_band, 0.0)
        p_prev = jnp.where(cur_ok, 0.0, p_band)
        p_cm = jnp.concatenate([p_cur, p_meta], axis=0).astype(BF16)
        v_cm = jnp.concatenate([vt[vs, tok], vmetat[vs]], axis=1)
        o = _dot(vt_prev[vs], p_prev.astype(BF16)) + _dot(v_cm, p_cm)
        o = o * (1.0 / denom)
        for hh in range(GROUP):
            r0 = (g * GROUP + hh) * HEAD_DIM
            ot_s[r0:r0 + HEAD_DIM, tok] = o[:, hh * BLOCK:(hh + 1) * BLOCK]

    gates = [None] * N_KV_HEADS
    held_q = None
    for g in range(N_KV_HEADS):
        pg = (g - 1) % N_KV_HEADS
        for jb in range(nblk):
            s = scores(g, jb)
            if jb == 0:
                roped = project_q(pg)
                if pg < g:
                    store_q(pg, roped)
                else:
                    held_q = roped
            elif jb == 1:
                gates[pg] = project_gate(pg)
            project_out(g * nblk + jb)
            finish(g, jb, s)
    store_q(N_KV_HEADS - 1, held_q)

    k_new = _dot(h, wk_ref[...]) + bk_ref[...]
    vt_new = _dot_nt(wvt_ref[...], h) + lanes_bias(bvt_ref[...])

    act_s[...] = (ot_s[...] * gat_s[...]).T

    kprev_s[...] = k_s[tq - BLOCK:, :]
    vprev_s[...] = vt_s[:, tq - BLOCK:]
    for g in range(N_KV_HEADS):
        sl = slice(g * KV_SLOT, (g + 1) * KV_SLOT)
        k_s[:, sl] = _rope_rows(k_new[:, sl], cosr_ref[...], sinr_ref[...])
        gat_s[g * rows_g:(g + 1) * rows_g, :] = gates[g]
    vt_s[...] = vt_new


def _attn_branch(x, p, tabs, kmeta, vmetat, *, tq):
    b, t_len, _ = x.shape
    assert tq == 2 * BLOCK
    n_t = t_len // tq
    n_tiles = b * n_t
    kern = functools.partial(_attn_kernel, tq=tq, n_t=n_t)
    proj_tile = lambda j: jnp.minimum(j, n_tiles - 1)
    out_tile = lambda j: jnp.maximum(j - 2, 0)
    in_specs = [
        pl.BlockSpec((1, tq, D_MODEL), lambda j: (proj_tile(j) // n_t, proj_tile(j) % n_t, 0)),
        _const_spec((1, D_MODEL)), _const_spec((1, D_MODEL)),
        _const_spec((D_MODEL, D_MODEL)), _const_spec((D_MODEL, V7X_LANES)),
        _const_spec((D_MODEL, N_KV_HEADS * KV_SLOT)), _const_spec((1, N_KV_HEADS * KV_SLOT)),
        _const_spec((D_KV, D_MODEL)), _const_spec((D_KV, V7X_LANES)),
        _const_spec((D_MODEL, D_MODEL)), _const_spec((D_MODEL, V7X_LANES)),
        _const_spec((D_MODEL, D_MODEL)),
        pl.BlockSpec((HALF, tq), lambda j: (0, proj_tile(j) % n_t)),
        pl.BlockSpec((HALF, tq), lambda j: (0, proj_tile(j) % n_t)),
        pl.BlockSpec((tq, KV_SLOT), lambda j: (proj_tile(j) % n_t, 0)),
        pl.BlockSpec((tq, KV_SLOT), lambda j: (proj_tile(j) % n_t, 0)),
        _const_spec((N_META, N_KV_HEADS * KV_SLOT)), _const_spec((D_KV, N_META)),
        _const_spec((N_KV_HEADS, 1, GROUP * BLOCK)),
    ]
    return pl.pallas_call(
        kern,
        grid=(n_tiles + 2,),
        in_specs=in_specs,
        out_specs=(pl.BlockSpec((1, tq, D_MODEL), lambda j: (out_tile(j) // n_t, out_tile(j) % n_t, 0)),
                   pl.BlockSpec((1, tq, D_MODEL), lambda j: (proj_tile(j) // n_t, proj_tile(j) % n_t, 0)),
                   pl.BlockSpec((1, tq, D_MODEL), lambda j: (proj_tile(j) // n_t, proj_tile(j) % n_t, 0))),
        out_shape=(jax.ShapeDtypeStruct((b, t_len, D_MODEL), BF16),
                   jax.ShapeDtypeStruct((b, t_len, D_MODEL), BF16),
                   jax.ShapeDtypeStruct((b, t_len, D_MODEL), F32)),
        scratch_shapes=[pltpu.VMEM((D_MODEL, tq), F32),
                        pltpu.VMEM((tq, N_KV_HEADS * KV_SLOT), F32),
                        pltpu.VMEM((D_KV, tq), F32),
                        pltpu.VMEM((D_MODEL, tq), F32),
                        pltpu.VMEM((BLOCK, N_KV_HEADS * KV_SLOT), F32),
                        pltpu.VMEM((D_KV, BLOCK), F32),
                        pltpu.VMEM((D_MODEL, tq), F32),
                        pltpu.VMEM((tq, D_MODEL), F32)],
        compiler_params=pltpu.CompilerParams(dimension_semantics=("arbitrary",),
                                             vmem_limit_bytes=V7X_VMEM_LIMIT),
        name="attn_branch",
    )(x, p["ln_g"], p["ln_b"], p["wqt"], p["bqt"], p["wk"], p["bk"], p["wvt"], p["bvt"],
      p["wgat"], p["bgat"], p["w_attn_out"], tabs["cos_t"], tabs["sin_t"], tabs["cos_r"], tabs["sin_r"],
      kmeta, vmetat, p["sink_rows"])


def _merge_kernel(h32_ref, ya_ref, yb_ref, *refs):
    wga_refs, wgb_refs = refs[:MERGE_BLOCKS], refs[MERGE_BLOCKS:2 * MERGE_BLOCKS]
    bmg_ref, wo_ref, bo_ref, g2_ref, b2_ref, out_ref = refs[2 * MERGE_BLOCKS:]
    h = h32_ref[0].astype(BF16)
    mixed = []
    for c in range(MERGE_BLOCKS):
        cs = slice(c * MERGE_COLS, (c + 1) * MERGE_COLS)
        gs = slice(D_MODEL + c * MERGE_COLS, D_MODEL + (c + 1) * MERGE_COLS)
        ga = _sigmoid(_dot(h, wga_refs[c][...]) + bmg_ref[:, cs])
        gb = _sigmoid(_dot(h, wgb_refs[c][...]) + bmg_ref[:, gs])
        mixed.append(ga * ya_ref[0, :, cs].astype(F32) + gb * yb_ref[0, :, cs].astype(F32))
    mixed = jnp.concatenate(mixed, axis=1)
    out = _dot(mixed.astype(BF16), wo_ref[...]) + bo_ref[...]
    out_ref[0] = _layer_norm(DEEPNORM_ALPHA * h32_ref[0] + out, g2_ref[...], b2_ref[...])


def _merge(h32, ya, yb, p, *, tm):
    b, t_len, _ = h32.shape
    tile = pl.BlockSpec((1, tm, D_MODEL), lambda bi, i: (bi, i, 0))
    w_in_block = lambda c: pl.BlockSpec((D_MODEL, MERGE_COLS), lambda bi, i: (0, c), pipeline_mode=pl.Buffered(1))
    in_specs = [tile, tile, tile,
                *[w_in_block(OFF_G // MERGE_COLS + c) for c in range(MERGE_BLOCKS)],
                *[w_in_block((OFF_G + D_MODEL) // MERGE_COLS + c) for c in range(MERGE_BLOCKS)],
                _const_spec((1, 2 * D_MODEL)),
                _const_spec((D_MODEL, D_MODEL)), _const_spec((1, D_MODEL)),
                _const_spec((1, D_MODEL)), _const_spec((1, D_MODEL))]
    return pl.pallas_call(
        _merge_kernel,
        grid=(b, t_len // tm),
        in_specs=in_specs,
        out_specs=tile,
        out_shape=jax.ShapeDtypeStruct((b, t_len, D_MODEL), F32),
        compiler_params=pltpu.CompilerParams(dimension_semantics=("arbitrary", "arbitrary"),
                                             vmem_limit_bytes=V7X_VMEM_LIMIT),
        name="merge",
    )(h32, ya, yb, *([p["w_in"]] * (2 * MERGE_BLOCKS)), p["bmg"], p["w_o"], p["b_o"],
      p["ln2_g"], p["ln2_b"])


def _rope_tables(t_total):
    inv = (ROPE_THETA ** (-np.arange(HALF, dtype=np.float32) / HALF)).astype(np.float32)
    ang = np.arange(t_total, dtype=np.float32)[:, None] * inv[None, :]
    cos, sin = np.cos(ang), np.sin(ang)
    zeros = np.zeros_like(cos)
    cos_r = np.concatenate([cos, cos, zeros, zeros], axis=1)
    sin_r = np.concatenate([-sin, sin, zeros, zeros], axis=1)
    return cos, sin, cos_r, sin_r


def _slotted(w):
    lead = w.shape[:-1]
    w4 = w.reshape(lead + (N_KV_HEADS, HEAD_DIM))
    w4 = jnp.concatenate([w4, jnp.zeros_like(w4)], axis=-1)
    return w4.reshape(lead + (N_KV_HEADS * KV_SLOT,))


def kernel(x, meta_tokens, ln_emb_g, ln_emb_b, w_in, b_in, conv_w, conv_b, w_ra, b_ra, w_ri, b_ri,
           lru_lambda, sinks, w_rnn_out, w_attn_out, w_o, b_o, ln_g, ln_b):
    b, seq, _ = x.shape
    assert b == V7X_SUBLANES and w_in.shape[0] == DEPTH
    nb, rb = N_RNN_BLOCKS, RNN_BLOCK
    w = w_in[0].astype(BF16)
    bi = b_in[0]
    row = lambda v: v.reshape(1, -1)
    blocks = lambda v: v.reshape(nb, 1, rb)
    lanes = lambda v: jnp.broadcast_to(v[:, None], (v.shape[0], V7X_LANES))

    common = {"ln_g": row(ln_emb_g), "ln_b": row(ln_emb_b)}
    p_rnn = dict(common, w_in=w,
                 bx=blocks(bi[:OFF_GR]), bg=blocks(bi[OFF_GR:OFF_Q]),
                 cw=conv_w[0].reshape(CONV_WIDTH, nb, rb).transpose(1, 0, 2), cb=blocks(conv_b[0]),
                 wra=w_ra[0].astype(BF16), bra=blocks(b_ra[0]),
                 wri=w_ri[0].astype(BF16), bri=blocks(b_ri[0]),
                 lam=blocks(lru_lambda[0]), w_rnn_out=w_rnn_out[0].astype(BF16))
    p_attn = dict(common,
                  wqt=w[:, OFF_Q:OFF_K].T, bqt=lanes(bi[OFF_Q:OFF_K]),
                  wk=_slotted(w[:, OFF_K:OFF_V]), bk=row(_slotted(bi[OFF_K:OFF_V])),
                  wvt=w[:, OFF_V:OFF_GA].T, bvt=lanes(bi[OFF_V:OFF_GA]),
                  wgat=w[:, OFF_GA:OFF_G].T, bgat=lanes(bi[OFF_GA:OFF_G]),
                  wv=w[:, OFF_V:OFF_GA], bv=row(bi[OFF_V:OFF_GA]),
                  w_attn_out=w_attn_out[0].astype(BF16),
                  sink_rows=jnp.repeat(sinks[0].astype(F32), BLOCK).reshape(N_KV_HEADS, 1, GROUP * BLOCK))
    p_merge = dict(common, w_in=w, bmg=row(bi[OFF_G:]),
                   w_o=w_o[0].astype(BF16), b_o=row(b_o[0]), ln2_g=row(ln_g[0]), ln2_b=row(ln_b[0]))

    cos, sin, cos_r, sin_r = _rope_tables(N_META + seq)
    tabs = {"cos_t": np.ascontiguousarray(cos[N_META:].T), "sin_t": np.ascontiguousarray(sin[N_META:].T),
            "cos_r": cos_r[N_META:], "sin_r": sin_r[N_META:]}

    meta = meta_tokens.astype(x.dtype)
    meta_b = jnp.broadcast_to(meta[None], (V7X_SUBLANES, N_META, D_MODEL))
    zero_hist = jnp.zeros((nb, HIST_ROWS, rb), F32)
    zero_state = jnp.zeros((nb, V7X_SUBLANES, rb), F32)
    _, hist0, state0 = _rnn_branch(meta_b, p_rnn, zero_hist, zero_state, tt=N_META, seq_start=True)
    kmeta, vmeta = _meta_kv(meta, p_attn, cos_r[:N_META], sin_r[:N_META])

    yb, h, h32 = _attn_branch(x, p_attn, tabs, kmeta, vmeta.T, tq=ATTN_TILE)
    ya, _, _ = _rnn_branch(h, p_rnn, hist0, state0, tt=RNN_TILE_STEPS, seq_start=False)
    return _merge(h32, ya, yb, p_merge, tm=MERGE_TILE)
```

```python
import functools

import jax
import jax.numpy as jnp
import numpy as np
from jax import lax
from jax.experimental import pallas as pl
from jax.experimental.pallas import tpu as pltpu

D_MODEL = 2048
N_META = 16
N_RNN_BLOCKS = 8
RNN_BLOCK = 256
CONV_WIDTH = 4
LRU_C = 8.0
HEAD_DIM = 64
HALF = HEAD_DIM // 2
N_Q_HEADS = 32
N_KV_HEADS = 4
GROUP = 8
D_KV = N_KV_HEADS * HEAD_DIM
BLOCK = 128
ROPE_THETA = 10000.0
NEG_INF = -1e30
LN_EPS = 1e-5
DEPTH = 1
DEEPNORM_ALPHA = (2.0 * DEPTH) ** 0.25
OFF_XR = 0
OFF_GR = D_MODEL
OFF_Q = 2 * D_MODEL
OFF_K = OFF_Q + D_MODEL
OFF_V = OFF_K + D_KV
OFF_GA = OFF_V + D_KV
OFF_G = OFF_GA + D_MODEL

V7X_SUBLANES = 8
V7X_LANES = 128
V7X_VMEM_LIMIT = 56 * 1024 * 1024

ATTN_TILE = 2 * BLOCK
RNN_TILE_STEPS = 64
MERGE_TILE = 256

KV_SLOT = V7X_LANES
HIST_ROWS = (CONV_WIDTH - 1) * V7X_SUBLANES
RNN_PERM_STEPS = 32
MERGE_COLS = 512
MERGE_BLOCKS = D_MODEL // MERGE_COLS

BF16 = jnp.bfloat16
F32 = jnp.float32


def _layer_norm(x, g, b):
    mu = jnp.mean(x, axis=-1, keepdims=True)
    xc = x - mu
    var = jnp.mean(xc * xc, axis=-1, keepdims=True)
    return xc * lax.rsqrt(var + LN_EPS) * g + b


def _sigmoid(x):
    return 1.0 / (1.0 + jnp.exp(-x))


def _dot(a, b):
    return jnp.dot(a, b, preferred_element_type=F32)


def _dot_nt(a, b):
    return lax.dot_general(a, b, (((1,), (1,)), ((), ())), preferred_element_type=F32)


def _const_spec(shape):
    nd = len(shape)
    return pl.BlockSpec(shape, lambda *_: (0,) * nd, pipeline_mode=pl.Buffered(1))


def _rnn_kernel(x_ref, lng_ref, lnb_ref, *refs, tt, seq_start, normalised_input):
    wx_refs, wg_refs = refs[:N_RNN_BLOCKS], refs[N_RNN_BLOCKS:2 * N_RNN_BLOCKS]
    (bx_ref, bg_ref, cw_ref, cb_ref, wra_ref, bra_ref, wri_ref, bri_ref, lam_ref, wout_ref, perm_ref,
     permt_ref, hist0_ref, state0_ref, ya_ref, hist_out_ref, state_out_ref,
     y_s, hist_s, state_s) = refs[2 * N_RNN_BLOCKS:]
    rows = tt * V7X_SUBLANES
    nb = N_RNN_BLOCKS
    j = pl.program_id(0)

    @pl.when(j == 0)
    def _():
        y_s[...] = jnp.zeros_like(y_s)
        hist_s[...] = hist0_ref[...]
        state_s[...] = state0_ref[...]

    ts = perm_ref.shape[0] // V7X_SUBLANES
    parts = []
    for s0 in range(0, tt, ts):
        hs_ = x_ref[:, s0:s0 + ts, :].reshape(ts * V7X_SUBLANES, D_MODEL)
        if not normalised_input:
            hs_ = _layer_norm(hs_, lng_ref[...], lnb_ref[...]).astype(BF16)
        parts.append(_dot(perm_ref[...], hs_).astype(BF16))
    h = jnp.concatenate(parts, axis=0)

    def project(n):
        return _dot(h, wx_refs[n][...]) + bx_ref[n], _dot(h, wg_refs[n][...]) + bg_ref[n]

    sub = ts * V7X_SUBLANES
    y_prev = y_s[...].astype(BF16)

    def project_out(n):
        cs = slice(n * RNN_BLOCK, (n + 1) * RNN_BLOCK)
        ya = _dot(y_prev, wout_ref[:, cs])
        for k, s0 in enumerate(range(0, tt, ts)):
            ya_ref[:, s0:s0 + ts, cs] = (ya[k * sub:(k + 1) * sub]
                                         .reshape(V7X_SUBLANES, ts, RNN_BLOCK).astype(ya_ref.dtype))

    ys = []
    nxt = project(0)
    for n in range(nb):
        xr, gr = nxt
        if n + 1 < nb:
            nxt = project(n + 1)
        project_out(n)
        xe = jnp.concatenate([hist_s[n], xr], axis=0)
        cw = cw_ref[n]
        conv = cb_ref[n] + cw[0:1] * xr
        for k in range(1, CONV_WIDTH):
            off = HIST_ROWS - k * V7X_SUBLANES
            conv = conv + cw[k:k + 1] * xe[off:off + rows]
        hist_s[n] = xe[rows:rows + HIST_ROWS]
        cb16 = conv.astype(BF16)
        gate_r = _sigmoid(_dot(cb16, wra_ref[n]) + bra_ref[n])
        gate_i = _sigmoid(_dot(cb16, wri_ref[n]) + bri_ref[n])
        lam = lam_ref[n]
        log_sig = jnp.minimum(lam, 0.0) - jnp.log1p(jnp.exp(-jnp.abs(lam)))
        a = jnp.exp((LRU_C * gate_r) * log_sig)
        mult = jnp.sqrt(1.0 - a * a)
        if seq_start:
            first = lax.broadcasted_iota(jnp.int32, (rows, RNN_BLOCK), 0) < V7X_SUBLANES
            mult = jnp.where(first & (j == 0), 1.0, mult)
        u = mult * gate_i * conv
        hcur = state_s[n]
        hs = []
        for t in range(tt):
            sl = slice(t * V7X_SUBLANES, (t + 1) * V7X_SUBLANES)
            hcur = a[sl] * hcur + u[sl]
            hs.append(hcur)
        state_s[n] = hcur
        hr = jnp.concatenate(hs, axis=0)
        ys.append((hr * (gr * _sigmoid(gr))).astype(BF16))
    y = jnp.concatenate(ys, axis=1)
    for r0 in range(0, rows, sub):
        y_s[r0:r0 + sub, :] = _dot(permt_ref[...], y[r0:r0 + sub])

    @pl.when(j == pl.num_programs(0) - 2)
    def _():
        hist_out_ref[...] = hist_s[...]
        state_out_ref[...] = state_s[...]


def _rnn_branch(x, p, hist0, state0, *, tt, seq_start):
    t_len = x.shape[1]
    nb, rb = N_RNN_BLOCKS, RNN_BLOCK
    rows = tt * V7X_SUBLANES
    ts = min(tt, RNN_PERM_STEPS)
    prow = ts * V7X_SUBLANES
    r = np.arange(prow)
    perm = np.zeros((prow, prow), np.float32)
    perm[r, (r % V7X_SUBLANES) * ts + r // V7X_SUBLANES] = 1.0
    perm, perm_t = jnp.asarray(perm, BF16), jnp.asarray(perm.T, BF16)
    assert x.dtype in (F32, BF16)
    kern = functools.partial(_rnn_kernel, tt=tt, seq_start=seq_start, normalised_input=x.dtype == BF16)
    n_tiles = t_len // tt
    w_in_block = lambda c: pl.BlockSpec((D_MODEL, rb), lambda j: (0, c), pipeline_mode=pl.Buffered(1))
    in_specs = [
        pl.BlockSpec((V7X_SUBLANES, tt, D_MODEL), lambda j: (0, jnp.minimum(j, n_tiles - 1), 0)),
        _const_spec((1, D_MODEL)), _const_spec((1, D_MODEL)),
        *[w_in_block(OFF_XR // rb + n) for n in range(nb)],
        *[w_in_block(OFF_GR // rb + n) for n in range(nb)],
        _const_spec((nb, 1, rb)), _const_spec((nb, 1, rb)),
        _const_spec((nb, CONV_WIDTH, rb)), _const_spec((nb, 1, rb)),
        _const_spec((nb, rb, rb)), _const_spec((nb, 1, rb)),
        _const_spec((nb, rb, rb)), _const_spec((nb, 1, rb)),
        _const_spec((nb, 1, rb)),
        _const_spec((D_MODEL, D_MODEL)),
        _const_spec((prow, prow)), _const_spec((prow, prow)),
        _const_spec((nb, HIST_ROWS, rb)), _const_spec((nb, V7X_SUBLANES, rb)),
    ]
    out_shape = (
        jax.ShapeDtypeStruct((V7X_SUBLANES, t_len, D_MODEL), BF16),
        jax.ShapeDtypeStruct((nb, HIST_ROWS, rb), F32),
        jax.ShapeDtypeStruct((nb, V7X_SUBLANES, rb), F32),
    )
    out_specs = (
        pl.BlockSpec((V7X_SUBLANES, tt, D_MODEL), lambda j: (0, jnp.maximum(j - 1, 0), 0)),
        pl.BlockSpec((nb, HIST_ROWS, rb), lambda j: (0, 0, 0)),
        pl.BlockSpec((nb, V7X_SUBLANES, rb), lambda j: (0, 0, 0)),
    )
    return pl.pallas_call(
        kern,
        grid=(n_tiles + 1,),
        in_specs=in_specs,
        out_specs=out_specs,
        out_shape=out_shape,
        scratch_shapes=[pltpu.VMEM((rows, D_MODEL), F32),
                        pltpu.VMEM((nb, HIST_ROWS, rb), F32), pltpu.VMEM((nb, V7X_SUBLANES, rb), F32)],
        compiler_params=pltpu.CompilerParams(dimension_semantics=("arbitrary",),
                                             vmem_limit_bytes=V7X_VMEM_LIMIT),
        name="rnn_branch_start" if seq_start else "rnn_branch",
    )(x, p["ln_g"], p["ln_b"], *([p["w_in"]] * (2 * nb)), p["bx"], p["bg"], p["cw"], p["cb"],
      p["wra"], p["bra"], p["wri"], p["bri"], p["lam"], p["w_rnn_out"], perm, perm_t, hist0, state0)


def _rope_rows(k, cos, sin_signed):
    lane = lax.broadcasted_iota(jnp.int32, k.shape, 1)
    partner = jnp.where((lane % HEAD_DIM) < HALF,
                        pltpu.roll(k, V7X_LANES - HALF, axis=1),
                        pltpu.roll(k, HALF, axis=1))
    return k * cos + partner * sin_signed


def _meta_kv_kernel(m_ref, lng_ref, lnb_ref, wk_ref, bk_ref, wv_ref, bv_ref, cos_ref, sin_ref,
                    k_ref, v_ref):
    h = _layer_norm(m_ref[...], lng_ref[...], lnb_ref[...]).astype(BF16)
    k = _dot(h, wk_ref[...]) + bk_ref[...]
    v = _dot(h, wv_ref[...]) + bv_ref[...]
    for g in range(N_KV_HEADS):
        sl = slice(g * KV_SLOT, (g + 1) * KV_SLOT)
        k_ref[:, sl] = _rope_rows(k[:, sl], cos_ref[...], sin_ref[...]).astype(BF16)
    v_ref[...] = v.astype(BF16)


def _meta_kv(meta, p, cos_rows, sin_rows):
    return pl.pallas_call(
        _meta_kv_kernel,
        out_shape=(jax.ShapeDtypeStruct((N_META, N_KV_HEADS * KV_SLOT), BF16),
                   jax.ShapeDtypeStruct((N_META, D_KV), BF16)),
        name="meta_kv",
    )(meta, p["ln_g"], p["ln_b"], p["wk"], p["bk"], p["wv"], p["bv"], cos_rows, sin_rows)


def _attn_kernel(x_ref, lng_ref, lnb_ref, *refs, tq, n_t):
    wq_refs, wga_refs = refs[:N_KV_HEADS], refs[N_KV_HEADS:2 * N_KV_HEADS]
    (bq_ref, bga_ref, wk_ref, bk_ref, wvt_ref, bvt_ref, wout_ref, cost_ref, sint_ref, cosr_ref, sinr_ref,
     kmeta_ref, vmetat_ref, sink_ref, yb_ref, h_ref, h32_ref,
     qt_s, k_s, vt_s, gat_s, kprev_s, vprev_s, ot_s, act_s) = refs[2 * N_KV_HEADS:]
    nblk = tq // BLOCK
    rows_g = GROUP * HEAD_DIM
    j = pl.program_id(0)

    @pl.when(j == 0)
    def _():
        for ref in (qt_s, k_s, vt_s, gat_s, kprev_s, vprev_s, act_s):
            ref[...] = jnp.zeros_like(ref)

    h32 = _layer_norm(x_ref[0], lng_ref[...], lnb_ref[...])
    h = h32.astype(BF16)
    h32_ref[0] = h32
    h_ref[0] = h

    act_prev = act_s[...].astype(BF16)
    out_cols = D_MODEL // (N_KV_HEADS * nblk)

    def project_out(c):
        cs = slice(c * out_cols, (c + 1) * out_cols)
        yb_ref[0, :, cs] = _dot(act_prev, wout_ref[:, cs]).astype(yb_ref.dtype)

    def lanes_bias(b):
        return jnp.concatenate([b] * (tq // V7X_LANES), axis=1)

    scale = HEAD_DIM ** -0.5
    cos = cost_ref[...] * scale
    sin = sint_ref[...] * scale

    def project_q(g):
        rs = slice(g * rows_g, (g + 1) * rows_g)
        qt = (_dot(h, wq_refs[g][...]) + bq_ref[:, rs]).T
        out = []
        for hh in range(GROUP):
            r0 = hh * HEAD_DIM
            q1, q2 = qt[r0:r0 + HALF], qt[r0 + HALF:r0 + HEAD_DIM]
            out.append((q1 * cos - q2 * sin, q2 * cos + q1 * sin))
        return out

    def store_q(g, roped):
        for hh, (lo, hi) in enumerate(roped):
            r0 = g * rows_g + hh * HEAD_DIM
            qt_s[r0:r0 + HALF, :] = lo
            qt_s[r0 + HALF:r0 + HEAD_DIM, :] = hi

    def project_gate(g):
        rs = slice(g * rows_g, (g + 1) * rows_g)
        gat = _dot(h, wga_refs[g][...]) + bga_ref[:, rs]
        return gat * _sigmoid(gat)

    i = jnp.maximum(j - 1, 0) % n_t
    k = k_s[...].astype(BF16)
    vt = vt_s[...].astype(BF16)
    k_carry, vt_carry = kprev_s[...].astype(BF16), vprev_s[...].astype(BF16)
    kmeta = kmeta_ref[...]
    vmetat = vmetat_ref[...]

    key_row = lax.broadcasted_iota(jnp.int32, (BLOCK, GROUP * BLOCK), 0)
    qry_col = lax.broadcasted_iota(jnp.int32, (BLOCK, GROUP * BLOCK), 1) % BLOCK
    cur_ok = key_row <= qry_col
    prev_in_window = key_row > qry_col
    first_prev_ok = prev_in_window & (i > 0)

    def scores(g, jb):
        tok = slice(jb * BLOCK, (jb + 1) * BLOCK)
        ks = slice(g * KV_SLOT, g * KV_SLOT + HEAD_DIM)
        k_prev = k_carry if jb == 0 else k[(jb - 1) * BLOCK:jb * BLOCK]
        q_g = jnp.concatenate(
            [qt_s[(g * GROUP + hh) * HEAD_DIM:(g * GROUP + hh + 1) * HEAD_DIM, tok].astype(BF16)
             for hh in range(GROUP)], axis=1)
        s_prev = _dot(k_prev[:, ks], q_g)
        if jb == 0:
            s_prev = jnp.where(first_prev_ok, s_prev, NEG_INF)
        s_cm = _dot(jnp.concatenate([k[tok, ks], kmeta[:, ks]], axis=0), q_g)
        s_band = jnp.where(cur_ok, s_cm[:BLOCK], s_prev)
        return s_band, s_cm[BLOCK:]

    def finish(g, jb, s):
        s_band, s_meta = s
        tok = slice(jb * BLOCK, (jb + 1) * BLOCK)
        vs = slice(g * HEAD_DIM, (g + 1) * HEAD_DIM)
        vt_prev = vt_carry if jb == 0 else vt[:, (jb - 1) * BLOCK:jb * BLOCK]
        sink = sink_ref[g]
        m = jnp.maximum(jnp.maximum(jnp.max(s_band, axis=0, keepdims=True),
                                    jnp.max(s_meta, axis=0, keepdims=True)), sink)
        p_band = jnp.exp(s_band - m)
        p_meta = jnp.exp(s_meta - m)
        denom = (jnp.sum(p_band, axis=0, keepdims=True) + jnp.sum(p_meta, axis=0, keepdims=True)
                 + jnp.exp(sink - m))
        p_cur = jnp.where(cur_ok, p_band, 0.0)
        p_prev = jnp.where(cur_ok, 0.0, p_band)
        p_cm = jnp.concatenate([p_cur, p_meta], axis=0).astype(BF16)
        v_cm = jnp.concatenate([vt[vs, tok], vmetat[vs]], axis=1)
        o = _dot(vt_prev[vs], p_prev.astype(BF16)) + _dot(v_cm, p_cm)
        o = o * (1.0 / denom)
        for hh in range(GROUP):
            r0 = (g * GROUP + hh) * HEAD_DIM
            ot_s[r0:r0 + HEAD_DIM, tok] = o[:, hh * BLOCK:(hh + 1) * BLOCK]

    gates = [None] * N_KV_HEADS
    held_q = None
    for g in range(N_KV_HEADS):
        pg = (g - 1) % N_KV_HEADS
        for jb in range(nblk):
            s = scores(g, jb)
            if jb == 0:
                roped = project_q(pg)
                if pg < g:
                    store_q(pg, roped)
                else:
                    held_q = roped
            elif jb == 1:
                gates[pg] = project_gate(pg)
            project_out(g * nblk + jb)
            finish(g, jb, s)
    store_q(N_KV_HEADS - 1, held_q)

    k_new = _dot(h, wk_ref[...]) + bk_ref[...]
    vt_new = _dot_nt(wvt_ref[...], h) + lanes_bias(bvt_ref[...])

    act_s[...] = ot_s[...].T * gat_s[...]

    kprev_s[...] = k_s[tq - BLOCK:, :]
    vprev_s[...] = vt_s[:, tq - BLOCK:]
    for g in range(N_KV_HEADS):
        sl = slice(g * KV_SLOT, (g + 1) * KV_SLOT)
        k_s[:, sl] = _rope_rows(k_new[:, sl], cosr_ref[...], sinr_ref[...])
        gat_s[:, g * rows_g:(g + 1) * rows_g] = gates[g]
    vt_s[...] = vt_new


def _attn_branch(x, p, tabs, kmeta, vmetat, *, tq):
    b, t_len, _ = x.shape
    assert tq == 2 * BLOCK
    n_t = t_len // tq
    n_tiles = b * n_t
    kern = functools.partial(_attn_kernel, tq=tq, n_t=n_t)
    proj_tile = lambda j: jnp.minimum(j, n_tiles - 1)
    out_tile = lambda j: jnp.maximum(j - 2, 0)
    cols_g = GROUP * HEAD_DIM
    w_in_block = lambda c: pl.BlockSpec((D_MODEL, cols_g), lambda j: (0, c), pipeline_mode=pl.Buffered(1))
    in_specs = [
        pl.BlockSpec((1, tq, D_MODEL), lambda j: (proj_tile(j) // n_t, proj_tile(j) % n_t, 0)),
        _const_spec((1, D_MODEL)), _const_spec((1, D_MODEL)),
        *[w_in_block(OFF_Q // cols_g + g) for g in range(N_KV_HEADS)],
        *[w_in_block(OFF_GA // cols_g + g) for g in range(N_KV_HEADS)],
        _const_spec((1, D_MODEL)), _const_spec((1, D_MODEL)),
        _const_spec((D_MODEL, N_KV_HEADS * KV_SLOT)), _const_spec((1, N_KV_HEADS * KV_SLOT)),
        _const_spec((D_KV, D_MODEL)), _const_spec((D_KV, V7X_LANES)),
        _const_spec((D_MODEL, D_MODEL)),
        pl.BlockSpec((HALF, tq), lambda j: (0, proj_tile(j) % n_t)),
        pl.BlockSpec((HALF, tq), lambda j: (0, proj_tile(j) % n_t)),
        pl.BlockSpec((tq, KV_SLOT), lambda j: (proj_tile(j) % n_t, 0)),
        pl.BlockSpec((tq, KV_SLOT), lambda j: (proj_tile(j) % n_t, 0)),
        _const_spec((N_META, N_KV_HEADS * KV_SLOT)), _const_spec((D_KV, N_META)),
        _const_spec((N_KV_HEADS, 1, GROUP * BLOCK)),
    ]
    return pl.pallas_call(
        kern,
        grid=(n_tiles + 2,),
        in_specs=in_specs,
        out_specs=(pl.BlockSpec((1, tq, D_MODEL), lambda j: (out_tile(j) // n_t, out_tile(j) % n_t, 0)),
                   pl.BlockSpec((1, tq, D_MODEL), lambda j: (proj_tile(j) // n_t, proj_tile(j) % n_t, 0)),
                   pl.BlockSpec((1, tq, D_MODEL), lambda j: (proj_tile(j) // n_t, proj_tile(j) % n_t, 0))),
        out_shape=(jax.ShapeDtypeStruct((b, t_len, D_MODEL), BF16),
                   jax.ShapeDtypeStruct((b, t_len, D_MODEL), BF16),
                   jax.ShapeDtypeStruct((b, t_len, D_MODEL), F32)),
        scratch_shapes=[pltpu.VMEM((D_MODEL, tq), F32),
                        pltpu.VMEM((tq, N_KV_HEADS * KV_SLOT), F32),
                        pltpu.VMEM((D_KV, tq), F32),
                        pltpu.VMEM((tq, D_MODEL), F32),
                        pltpu.VMEM((BLOCK, N_KV_HEADS * KV_SLOT), F32),
                        pltpu.VMEM((D_KV, BLOCK), F32),
                        pltpu.VMEM((D_MODEL, tq), F32),
                        pltpu.VMEM((tq, D_MODEL), F32)],
        compiler_params=pltpu.CompilerParams(dimension_semantics=("arbitrary",),
                                             vmem_limit_bytes=V7X_VMEM_LIMIT),
        name="attn_branch",
    )(x, p["ln_g"], p["ln_b"], *([p["w_in"]] * (2 * N_KV_HEADS)), p["bq"], p["bga"],
      p["wk"], p["bk"], p["wvt"], p["bvt"], p["w_attn_out"], tabs["cos_t"], tabs["sin_t"], tabs["cos_r"], tabs["sin_r"],
      kmeta, vmetat, p["sink_rows"])


def _merge_kernel(h32_ref, ya_ref, yb_ref, *refs):
    wga_refs, wgb_refs = refs[:MERGE_BLOCKS], refs[MERGE_BLOCKS:2 * MERGE_BLOCKS]
    bmg_ref, wo_ref, bo_ref, g2_ref, b2_ref, out_ref = refs[2 * MERGE_BLOCKS:]
    h = h32_ref[0].astype(BF16)
    mixed = []
    for c in range(MERGE_BLOCKS):
        cs = slice(c * MERGE_COLS, (c + 1) * MERGE_COLS)
        gs = slice(D_MODEL + c * MERGE_COLS, D_MODEL + (c + 1) * MERGE_COLS)
        ga = _sigmoid(_dot(h, wga_refs[c][...]) + bmg_ref[:, cs])
        gb = _sigmoid(_dot(h, wgb_refs[c][...]) + bmg_ref[:, gs])
        mixed.append(ga * ya_ref[0, :, cs].astype(F32) + gb * yb_ref[0, :, cs].astype(F32))
    mixed = jnp.concatenate(mixed, axis=1)
    out = _dot(mixed.astype(BF16), wo_ref[...]) + bo_ref[...]
    out_ref[0] = _layer_norm(DEEPNORM_ALPHA * h32_ref[0] + out, g2_ref[...], b2_ref[...])


def _merge(h32, ya, yb, p, *, tm):
    b, t_len, _ = h32.shape
    tile = pl.BlockSpec((1, tm, D_MODEL), lambda bi, i: (bi, i, 0))
    w_in_block = lambda c: pl.BlockSpec((D_MODEL, MERGE_COLS), lambda bi, i: (0, c), pipeline_mode=pl.Buffered(1))
    in_specs = [tile, tile, tile,
                *[w_in_block(OFF_G // MERGE_COLS + c) for c in range(MERGE_BLOCKS)],
                *[w_in_block((OFF_G + D_MODEL) // MERGE_COLS + c) for c in range(MERGE_BLOCKS)],
                _const_spec((1, 2 * D_MODEL)),
                _const_spec((D_MODEL, D_MODEL)), _const_spec((1, D_MODEL)),
                _const_spec((1, D_MODEL)), _const_spec((1, D_MODEL))]
    return pl.pallas_call(
        _merge_kernel,
        grid=(b, t_len // tm),
        in_specs=in_specs,
        out_specs=tile,
        out_shape=jax.ShapeDtypeStruct((b, t_len, D_MODEL), F32),
        compiler_params=pltpu.CompilerParams(dimension_semantics=("arbitrary", "arbitrary"),
                                             vmem_limit_bytes=V7X_VMEM_LIMIT),
        name="merge",
    )(h32, ya, yb, *([p["w_in"]] * (2 * MERGE_BLOCKS)), p["bmg"], p["w_o"], p["b_o"],
      p["ln2_g"], p["ln2_b"])


def _rope_tables(t_total):
    inv = (ROPE_THETA ** (-np.arange(HALF, dtype=np.float32) / HALF)).astype(np.float32)
    ang = np.arange(t_total, dtype=np.float32)[:, None] * inv[None, :]
    cos, sin = np.cos(ang), np.sin(ang)
    zeros = np.zeros_like(cos)
    cos_r = np.concatenate([cos, cos, zeros, zeros], axis=1)
    sin_r = np.concatenate([-sin, sin, zeros, zeros], axis=1)
    return cos, sin, cos_r, sin_r


def _slotted(w):
    lead = w.shape[:-1]
    w4 = w.reshape(lead + (N_KV_HEADS, HEAD_DIM))
    w4 = jnp.concatenate([w4, jnp.zeros_like(w4)], axis=-1)
    return w4.reshape(lead + (N_KV_HEADS * KV_SLOT,))


def kernel(x, meta_tokens, ln_emb_g, ln_emb_b, w_in, b_in, conv_w, conv_b, w_ra, b_ra, w_ri, b_ri,
           lru_lambda, sinks, w_rnn_out, w_attn_out, w_o, b_o, ln_g, ln_b):
    b, seq, _ = x.shape
    assert b == V7X_SUBLANES and w_in.shape[0] == DEPTH
    nb, rb = N_RNN_BLOCKS, RNN_BLOCK
    w = w_in[0].astype(BF16)
    bi = b_in[0]
    row = lambda v: v.reshape(1, -1)
    blocks = lambda v: v.reshape(nb, 1, rb)
    lanes = lambda v: jnp.broadcast_to(v[:, None], (v.shape[0], V7X_LANES))

    common = {"ln_g": row(ln_emb_g), "ln_b": row(ln_emb_b)}
    p_rnn = dict(common, w_in=w,
                 bx=blocks(bi[:OFF_GR]), bg=blocks(bi[OFF_GR:OFF_Q]),
                 cw=conv_w[0].reshape(CONV_WIDTH, nb, rb).transpose(1, 0, 2), cb=blocks(conv_b[0]),
                 wra=w_ra[0].astype(BF16), bra=blocks(b_ra[0]),
                 wri=w_ri[0].astype(BF16), bri=blocks(b_ri[0]),
                 lam=blocks(lru_lambda[0]), w_rnn_out=w_rnn_out[0].astype(BF16))
    p_attn = dict(common,
                  w_in=w, bq=row(bi[OFF_Q:OFF_K]), bga=row(bi[OFF_GA:OFF_G]),
                  wk=_slotted(w[:, OFF_K:OFF_V]), bk=row(_slotted(bi[OFF_K:OFF_V])),
                  wvt=w[:, OFF_V:OFF_GA].T, bvt=lanes(bi[OFF_V:OFF_GA]),
                  wv=w[:, OFF_V:OFF_GA], bv=row(bi[OFF_V:OFF_GA]),
                  w_attn_out=w_attn_out[0].astype(BF16),
                  sink_rows=jnp.repeat(sinks[0].astype(F32), BLOCK).reshape(N_KV_HEADS, 1, GROUP * BLOCK))
    p_merge = dict(common, w_in=w, bmg=row(bi[OFF_G:]),
                   w_o=w_o[0].astype(BF16), b_o=row(b_o[0]), ln2_g=row(ln_g[0]), ln2_b=row(ln_b[0]))

    cos, sin, cos_r, sin_r = _rope_tables(N_META + seq)
    tabs = {"cos_t": np.ascontiguousarray(cos[N_META:].T), "sin_t": np.ascontiguousarray(sin[N_META:].T),
            "cos_r": cos_r[N_META:], "sin_r": sin_r[N_META:]}

    meta = meta_tokens.astype(x.dtype)
    meta_b = jnp.broadcast_to(meta[None], (V7X_SUBLANES, N_META, D_MODEL))
    zero_hist = jnp.zeros((nb, HIST_ROWS, rb), F32)
    zero_state = jnp.zeros((nb, V7X_SUBLANES, rb), F32)
    _, hist0, state0 = _rnn_branch(meta_b, p_rnn, zero_hist, zero_state, tt=N_META, seq_start=True)
    kmeta, vmeta = _meta_kv(meta, p_attn, cos_r[:N_META], sin_r[:N_META])

    yb, h, h32 = _attn_branch(x, p_attn, tabs, kmeta, vmeta.T, tq=ATTN_TILE)
    ya, _, _ = _rnn_branch(h, p_rnn, hist0, state0, tt=RNN_TILE_STEPS, seq_start=False)
    return _merge(h32, ya, yb, p_merge, tm=MERGE_TILE)
```

```python
import functools

import jax
import jax.numpy as jnp
import numpy as np
from jax import lax
from jax.experimental import pallas as pl
from jax.experimental.pallas import tpu as pltpu

D_MODEL = 2048
N_META = 16
N_RNN_BLOCKS = 8
RNN_BLOCK = 256
CONV_WIDTH = 4
LRU_C = 8.0
HEAD_DIM = 64
HALF = HEAD_DIM // 2
N_Q_HEADS = 32
N_KV_HEADS = 4
GROUP = 8
D_KV = N_KV_HEADS * HEAD_DIM
BLOCK = 128
ROPE_THETA = 10000.0
NEG_INF = -1e30
LN_EPS = 1e-5
DEPTH = 1
DEEPNORM_ALPHA = (2.0 * DEPTH) ** 0.25
OFF_XR = 0
OFF_GR = D_MODEL
OFF_Q = 2 * D_MODEL
OFF_K = OFF_Q + D_MODEL
OFF_V = OFF_K + D_KV
OFF_GA = OFF_V + D_KV
OFF_G = OFF_GA + D_MODEL

V7X_SUBLANES = 8
V7X_LANES = 128
V7X_VMEM_LIMIT = 56 * 1024 * 1024

ATTN_TILE = 2 * BLOCK
RNN_TILE_STEPS = 64
MERGE_TILE = 256

KV_SLOT = V7X_LANES
HIST_ROWS = (CONV_WIDTH - 1) * V7X_SUBLANES
RNN_PERM_STEPS = 32
MERGE_COLS = 512
MERGE_BLOCKS = D_MODEL // MERGE_COLS

BF16 = jnp.bfloat16
F32 = jnp.float32


def _layer_norm(x, g, b):
    mu = jnp.mean(x, axis=-1, keepdims=True)
    xc = x - mu
    var = jnp.mean(xc * xc, axis=-1, keepdims=True)
    return xc * lax.rsqrt(var + LN_EPS) * g + b


def _sigmoid(x):
    return 1.0 / (1.0 + jnp.exp(-x))


def _dot(a, b):
    return jnp.dot(a, b, preferred_element_type=F32)


def _dot_nt(a, b):
    return lax.dot_general(a, b, (((1,), (1,)), ((), ())), preferred_element_type=F32)


def _const_spec(shape):
    nd = len(shape)
    return pl.BlockSpec(shape, lambda *_: (0,) * nd, pipeline_mode=pl.Buffered(1))


def _rnn_kernel(x_ref, lng_ref, lnb_ref, *refs, tt, seq_start, normalised_input):
    wx_refs, wg_refs = refs[:N_RNN_BLOCKS], refs[N_RNN_BLOCKS:2 * N_RNN_BLOCKS]
    (bx_ref, bg_ref, cw_ref, cb_ref, wra_ref, bra_ref, wri_ref, bri_ref, lam_ref, wout_ref, perm_ref,
     permt_ref, hist0_ref, state0_ref, ya_ref, hist_out_ref, state_out_ref,
     y_s, hist_s, state_s) = refs[2 * N_RNN_BLOCKS:]
    rows = tt * V7X_SUBLANES
    nb = N_RNN_BLOCKS
    j = pl.program_id(0)

    @pl.when(j == 0)
    def _():
        y_s[...] = jnp.zeros_like(y_s)
        hist_s[...] = hist0_ref[...]
        state_s[...] = state0_ref[...]

    ts = perm_ref.shape[0] // V7X_SUBLANES
    parts = []
    for s0 in range(0, tt, ts):
        hs_ = x_ref[:, s0:s0 + ts, :].reshape(ts * V7X_SUBLANES, D_MODEL)
        if not normalised_input:
            hs_ = _layer_norm(hs_, lng_ref[...], lnb_ref[...]).astype(BF16)
        parts.append(_dot(perm_ref[...], hs_).astype(BF16))
    h = jnp.concatenate(parts, axis=0)

    def project(n):
        return _dot(h, wx_refs[n][...]) + bx_ref[n], _dot(h, wg_refs[n][...]) + bg_ref[n]

    sub = ts * V7X_SUBLANES
    y_prev = y_s[...].astype(BF16)

    def project_out(n):
        cs = slice(n * RNN_BLOCK, (n + 1) * RNN_BLOCK)
        ya = _dot(y_prev, wout_ref[:, cs])
        for k, s0 in enumerate(range(0, tt, ts)):
            ya_ref[:, s0:s0 + ts, cs] = (ya[k * sub:(k + 1) * sub]
                                         .reshape(V7X_SUBLANES, ts, RNN_BLOCK).astype(ya_ref.dtype))

    ys = []
    nxt = project(0)
    for n in range(nb):
        xr, gr = nxt
        if n + 1 < nb:
            nxt = project(n + 1)
        project_out(n)
        xe = jnp.concatenate([hist_s[n], xr], axis=0)
        cw = cw_ref[n]
        conv = cb_ref[n] + cw[0:1] * xr
        for k in range(1, CONV_WIDTH):
            off = HIST_ROWS - k * V7X_SUBLANES
            conv = conv + cw[k:k + 1] * xe[off:off + rows]
        hist_s[n] = xe[rows:rows + HIST_ROWS]
        cb16 = conv.astype(BF16)
        gate_r = _sigmoid(_dot(cb16, wra_ref[n]) + bra_ref[n])
        gate_i = _sigmoid(_dot(cb16, wri_ref[n]) + bri_ref[n])
        lam = lam_ref[n]
        log_sig = jnp.minimum(lam, 0.0) - jnp.log1p(jnp.exp(-jnp.abs(lam)))
        a = jnp.exp((LRU_C * gate_r) * log_sig)
        mult = jnp.sqrt(1.0 - a * a)
        if seq_start:
            first = lax.broadcasted_iota(jnp.int32, (rows, RNN_BLOCK), 0) < V7X_SUBLANES
            mult = jnp.where(first & (j == 0), 1.0, mult)
        u = mult * gate_i * conv
        hcur = state_s[n]
        hs = []
        for t in range(tt):
            sl = slice(t * V7X_SUBLANES, (t + 1) * V7X_SUBLANES)
            hcur = a[sl] * hcur + u[sl]
            hs.append(hcur)
        state_s[n] = hcur
        hr = jnp.concatenate(hs, axis=0)
        ys.append((hr * (gr * _sigmoid(gr))).astype(BF16))
    y = jnp.concatenate(ys, axis=1)
    for r0 in range(0, rows, sub):
        y_s[r0:r0 + sub, :] = _dot(permt_ref[...], y[r0:r0 + sub])

    @pl.when(j == pl.num_programs(0) - 2)
    def _():
        hist_out_ref[...] = hist_s[...]
        state_out_ref[...] = state_s[...]


def _rnn_branch(x, p, hist0, state0, *, tt, seq_start):
    t_len = x.shape[1]
    nb, rb = N_RNN_BLOCKS, RNN_BLOCK
    rows = tt * V7X_SUBLANES
    ts = min(tt, RNN_PERM_STEPS)
    prow = ts * V7X_SUBLANES
    r = np.arange(prow)
    perm = np.zeros((prow, prow), np.float32)
    perm[r, (r % V7X_SUBLANES) * ts + r // V7X_SUBLANES] = 1.0
    perm, perm_t = jnp.asarray(perm, BF16), jnp.asarray(perm.T, BF16)
    assert x.dtype in (F32, BF16)
    kern = functools.partial(_rnn_kernel, tt=tt, seq_start=seq_start, normalised_input=x.dtype == BF16)
    n_tiles = t_len // tt
    w_in_block = lambda c: pl.BlockSpec((D_MODEL, rb), lambda j: (0, c), pipeline_mode=pl.Buffered(1))
    in_specs = [
        pl.BlockSpec((V7X_SUBLANES, tt, D_MODEL), lambda j: (0, jnp.minimum(j, n_tiles - 1), 0)),
        _const_spec((1, D_MODEL)), _const_spec((1, D_MODEL)),
        *[w_in_block(OFF_XR // rb + n) for n in range(nb)],
        *[w_in_block(OFF_GR // rb + n) for n in range(nb)],
        _const_spec((nb, 1, rb)), _const_spec((nb, 1, rb)),
        _const_spec((nb, CONV_WIDTH, rb)), _const_spec((nb, 1, rb)),
        _const_spec((nb, rb, rb)), _const_spec((nb, 1, rb)),
        _const_spec((nb, rb, rb)), _const_spec((nb, 1, rb)),
        _const_spec((nb, 1, rb)),
        _const_spec((D_MODEL, D_MODEL)),
        _const_spec((prow, prow)), _const_spec((prow, prow)),
        _const_spec((nb, HIST_ROWS, rb)), _const_spec((nb, V7X_SUBLANES, rb)),
    ]
    out_shape = (
        jax.ShapeDtypeStruct((V7X_SUBLANES, t_len, D_MODEL), BF16),
        jax.ShapeDtypeStruct((nb, HIST_ROWS, rb), F32),
        jax.ShapeDtypeStruct((nb, V7X_SUBLANES, rb), F32),
    )
    out_specs = (
        pl.BlockSpec((V7X_SUBLANES, tt, D_MODEL), lambda j: (0, jnp.maximum(j - 1, 0), 0)),
        pl.BlockSpec((nb, HIST_ROWS, rb), lambda j: (0, 0, 0)),
        pl.BlockSpec((nb, V7X_SUBLANES, rb), lambda j: (0, 0, 0)),
    )
    return pl.pallas_call(
        kern,
        grid=(n_tiles + 1,),
        in_specs=in_specs,
        out_specs=out_specs,
        out_shape=out_shape,
        scratch_shapes=[pltpu.VMEM((rows, D_MODEL), F32),
                        pltpu.VMEM((nb, HIST_ROWS, rb), F32), pltpu.VMEM((nb, V7X_SUBLANES, rb), F32)],
        compiler_params=pltpu.CompilerParams(dimension_semantics=("arbitrary",),
                                             vmem_limit_bytes=V7X_VMEM_LIMIT),
        name="rnn_branch_start" if seq_start else "rnn_branch",
    )(x, p["ln_g"], p["ln_b"], *([p["w_in"]] * (2 * nb)), p["bx"], p["bg"], p["cw"], p["cb"],
      p["wra"], p["bra"], p["wri"], p["bri"], p["lam"], p["w_rnn_out"], perm, perm_t, hist0, state0)


def _rope_rows(k, cos, sin_signed):
    lane = lax.broadcasted_iota(jnp.int32, k.shape, 1)
    partner = jnp.where((lane % HEAD_DIM) < HALF,
                        pltpu.roll(k, V7X_LANES - HALF, axis=1),
                        pltpu.roll(k, HALF, axis=1))
    return k * cos + partner * sin_signed


def _meta_kv_kernel(m_ref, lng_ref, lnb_ref, wk_ref, bk_ref, wv_ref, bv_ref, cos_ref, sin_ref,
                    k_ref, v_ref):
    h = _layer_norm(m_ref[...], lng_ref[...], lnb_ref[...]).astype(BF16)
    k = _dot(h, wk_ref[...]) + bk_ref[...]
    v = _dot(h, wv_ref[...]) + bv_ref[...]
    for g in range(N_KV_HEADS):
        sl = slice(g * KV_SLOT, (g + 1) * KV_SLOT)
        k_ref[:, sl] = _rope_rows(k[:, sl], cos_ref[...], sin_ref[...]).astype(BF16)
    v_ref[...] = v.astype(BF16)


def _meta_kv(meta, p, cos_rows, sin_rows):
    return pl.pallas_call(
        _meta_kv_kernel,
        out_shape=(jax.ShapeDtypeStruct((N_META, N_KV_HEADS * KV_SLOT), BF16),
                   jax.ShapeDtypeStruct((N_META, D_KV), BF16)),
        name="meta_kv",
    )(meta, p["ln_g"], p["ln_b"], p["wk"], p["bk"], p["wv"], p["bv"], cos_rows, sin_rows)


def _attn_kernel(x_ref, lng_ref, lnb_ref, *refs, tq, n_t):
    wq_refs, wga_refs = refs[:N_KV_HEADS], refs[N_KV_HEADS:2 * N_KV_HEADS]
    (bq_ref, bga_ref, wk_ref, bk_ref, wvt_ref, bvt_ref, wout_ref, cost_ref, sint_ref, cosr_ref, sinr_ref,
     kmeta_ref, vmetat_ref, sink_ref, yb_ref, h_ref, h32_ref,
     qt_s, k_s, vt_s, gat_s, kprev_s, vprev_s, ot_s, act_s) = refs[2 * N_KV_HEADS:]
    nblk = tq // BLOCK
    rows_g = GROUP * HEAD_DIM
    j = pl.program_id(0)

    @pl.when(j == 0)
    def _():
        for ref in (qt_s, k_s, vt_s, gat_s, kprev_s, vprev_s, act_s):
            ref[...] = jnp.zeros_like(ref)

    h32 = _layer_norm(x_ref[0], lng_ref[...], lnb_ref[...])
    h = h32.astype(BF16)
    h32_ref[0] = h32
    h_ref[0] = h

    act_prev = act_s[...].astype(BF16)
    out_cols = D_MODEL // (N_KV_HEADS * nblk)

    def project_out(c):
        cs = slice(c * out_cols, (c + 1) * out_cols)
        yb_ref[0, :, cs] = _dot(act_prev, wout_ref[:, cs]).astype(yb_ref.dtype)

    scale = HEAD_DIM ** -0.5
    cos = cost_ref[...] * scale
    sin = sint_ref[...] * scale

    def project_q(g):
        rs = slice(g * rows_g, (g + 1) * rows_g)
        qt = (_dot(h, wq_refs[g][...]) + bq_ref[:, rs]).T
        out = []
        for hh in range(GROUP):
            r0 = hh * HEAD_DIM
            q1, q2 = qt[r0:r0 + HALF], qt[r0 + HALF:r0 + HEAD_DIM]
            out.append((q1 * cos - q2 * sin, q2 * cos + q1 * sin))
        return out

    def store_q(g, roped):
        for hh, (lo, hi) in enumerate(roped):
            r0 = g * rows_g + hh * HEAD_DIM
            qt_s[r0:r0 + HALF, :] = lo
            qt_s[r0 + HALF:r0 + HEAD_DIM, :] = hi

    def project_gate(g):
        rs = slice(g * rows_g, (g + 1) * rows_g)
        gat = _dot(h, wga_refs[g][...]) + bga_ref[:, rs]
        return gat * _sigmoid(gat)

    i = jnp.maximum(j - 1, 0) % n_t
    k = k_s[...].astype(BF16)
    vt = vt_s[...].astype(BF16)
    k_carry, vt_carry = kprev_s[...].astype(BF16), vprev_s[...].astype(BF16)
    kmeta = kmeta_ref[...]
    vmetat = vmetat_ref[...]

    key_row = lax.broadcasted_iota(jnp.int32, (BLOCK, GROUP * BLOCK), 0)
    qry_col = lax.broadcasted_iota(jnp.int32, (BLOCK, GROUP * BLOCK), 1) % BLOCK
    cur_ok = key_row <= qry_col
    first_prev_ok = (key_row > qry_col) & (i > 0)

    def scores(g, jb):
        tok = slice(jb * BLOCK, (jb + 1) * BLOCK)
        ks = slice(g * KV_SLOT, g * KV_SLOT + HEAD_DIM)
        k_prev = k_carry if jb == 0 else k[(jb - 1) * BLOCK:jb * BLOCK]
        q_g = jnp.concatenate(
            [qt_s[(g * GROUP + hh) * HEAD_DIM:(g * GROUP + hh + 1) * HEAD_DIM, tok].astype(BF16)
             for hh in range(GROUP)], axis=1)
        s_prev = _dot(k_prev[:, ks], q_g)
        if jb == 0:
            s_prev = jnp.where(first_prev_ok, s_prev, NEG_INF)
        s_cm = _dot(jnp.concatenate([k[tok, ks], kmeta[:, ks]], axis=0), q_g)
        s_band = jnp.where(cur_ok, s_cm[:BLOCK], s_prev)
        return s_band, s_cm[BLOCK:]

    def finish(g, jb, s):
        s_band, s_meta = s
        tok = slice(jb * BLOCK, (jb + 1) * BLOCK)
        vs = slice(g * HEAD_DIM, (g + 1) * HEAD_DIM)
        vt_prev = vt_carry if jb == 0 else vt[:, (jb - 1) * BLOCK:jb * BLOCK]
        sink = sink_ref[g]
        m = jnp.maximum(jnp.maximum(jnp.max(s_band, axis=0, keepdims=True),
                                    jnp.max(s_meta, axis=0, keepdims=True)), sink)
        p_band = jnp.exp(s_band - m)
        p_meta = jnp.exp(s_meta - m)
        denom = (jnp.sum(p_band, axis=0, keepdims=True) + jnp.sum(p_meta, axis=0, keepdims=True)
                 + jnp.exp(sink - m))
        p_cur = jnp.where(cur_ok, p_band, 0.0)
        p_prev = jnp.where(cur_ok, 0.0, p_band)
        p_cm = jnp.concatenate([p_cur, p_meta], axis=0).astype(BF16)
        v_cm = jnp.concatenate([vt[vs, tok], vmetat[vs]], axis=1)
        o = _dot(vt_prev[vs], p_prev.astype(BF16)) + _dot(v_cm, p_cm)
        o = o * (1.0 / denom)
        for hh in range(GROUP):
            r0 = (g * GROUP + hh) * HEAD_DIM
            ot_s[r0:r0 + HEAD_DIM, tok] = o[:, hh * BLOCK:(hh + 1) * BLOCK]

    gates = [None] * N_KV_HEADS
    held_q = None
    for g in range(N_KV_HEADS):
        pg = (g - 1) % N_KV_HEADS
        for jb in range(nblk):
            s = scores(g, jb)
            if jb == 0:
                roped = project_q(pg)
                if pg < g:
                    store_q(pg, roped)
                else:
                    held_q = roped
            elif jb == 1:
                gates[pg] = project_gate(pg)
            project_out(g * nblk + jb)
            finish(g, jb, s)
    store_q(N_KV_HEADS - 1, held_q)

    k_new = _dot(h, wk_ref[...]) + bk_ref[...]
    vt_new = _dot_nt(wvt_ref[...], h) + jnp.concatenate([bvt_ref[...]] * (tq // V7X_LANES), axis=1)

    act_s[...] = ot_s[...].T * gat_s[...]

    kprev_s[...] = k_s[tq - BLOCK:, :]
    vprev_s[...] = vt_s[:, tq - BLOCK:]
    for g in range(N_KV_HEADS):
        sl = slice(g * KV_SLOT, (g + 1) * KV_SLOT)
        k_s[:, sl] = _rope_rows(k_new[:, sl], cosr_ref[...], sinr_ref[...])
        gat_s[:, g * rows_g:(g + 1) * rows_g] = gates[g]
    vt_s[...] = vt_new


def _attn_branch(x, p, tabs, kmeta, vmetat, *, tq):
    b, t_len, _ = x.shape
    assert tq == 2 * BLOCK
    n_t = t_len // tq
    n_tiles = b * n_t
    kern = functools.partial(_attn_kernel, tq=tq, n_t=n_t)
    proj_tile = lambda j: jnp.minimum(j, n_tiles - 1)
    out_tile = lambda j: jnp.maximum(j - 2, 0)
    cols_g = GROUP * HEAD_DIM
    w_in_block = lambda c: pl.BlockSpec((D_MODEL, cols_g), lambda j: (0, c), pipeline_mode=pl.Buffered(1))
    in_specs = [
        pl.BlockSpec((1, tq, D_MODEL), lambda j: (proj_tile(j) // n_t, proj_tile(j) % n_t, 0)),
        _const_spec((1, D_MODEL)), _const_spec((1, D_MODEL)),
        *[w_in_block(OFF_Q // cols_g + g) for g in range(N_KV_HEADS)],
        *[w_in_block(OFF_GA // cols_g + g) for g in range(N_KV_HEADS)],
        _const_spec((1, D_MODEL)), _const_spec((1, D_MODEL)),
        _const_spec((D_MODEL, N_KV_HEADS * KV_SLOT)), _const_spec((1, N_KV_HEADS * KV_SLOT)),
        _const_spec((D_KV, D_MODEL)), _const_spec((D_KV, V7X_LANES)),
        _const_spec((D_MODEL, D_MODEL)),
        pl.BlockSpec((HALF, tq), lambda j: (0, proj_tile(j) % n_t)),
        pl.BlockSpec((HALF, tq), lambda j: (0, proj_tile(j) % n_t)),
        pl.BlockSpec((tq, KV_SLOT), lambda j: (proj_tile(j) % n_t, 0)),
        pl.BlockSpec((tq, KV_SLOT), lambda j: (proj_tile(j) % n_t, 0)),
        _const_spec((N_META, N_KV_HEADS * KV_SLOT)), _const_spec((D_KV, N_META)),
        _const_spec((N_KV_HEADS, 1, GROUP * BLOCK)),
    ]
    return pl.pallas_call(
        kern,
        grid=(n_tiles + 2,),
        in_specs=in_specs,
        out_specs=(pl.BlockSpec((1, tq, D_MODEL), lambda j: (out_tile(j) // n_t, out_tile(j) % n_t, 0)),
                   pl.BlockSpec((1, tq, D_MODEL), lambda j: (proj_tile(j) // n_t, proj_tile(j) % n_t, 0)),
                   pl.BlockSpec((1, tq, D_MODEL), lambda j: (proj_tile(j) // n_t, proj_tile(j) % n_t, 0))),
        out_shape=(jax.ShapeDtypeStruct((b, t_len, D_MODEL), BF16),
                   jax.ShapeDtypeStruct((b, t_len, D_MODEL), BF16),
                   jax.ShapeDtypeStruct((b, t_len, D_MODEL), F32)),
        scratch_shapes=[pltpu.VMEM((D_MODEL, tq), F32),
                        pltpu.VMEM((tq, N_KV_HEADS * KV_SLOT), F32),
                        pltpu.VMEM((D_KV, tq), F32),
                        pltpu.VMEM((tq, D_MODEL), F32),
                        pltpu.VMEM((BLOCK, N_KV_HEADS * KV_SLOT), F32),
                        pltpu.VMEM((D_KV, BLOCK), F32),
                        pltpu.VMEM((D_MODEL, tq), F32),
                        pltpu.VMEM((tq, D_MODEL), F32)],
        compiler_params=pltpu.CompilerParams(dimension_semantics=("arbitrary",),
                                             vmem_limit_bytes=V7X_VMEM_LIMIT),
        name="attn_branch",
    )(x, p["ln_g"], p["ln_b"], *([p["w_in"]] * (2 * N_KV_HEADS)), p["bq"], p["bga"],
      p["wk"], p["bk"], p["wvt"], p["bvt"], p["w_attn_out"], tabs["cos_t"], tabs["sin_t"], tabs["cos_r"], tabs["sin_r"],
      kmeta, vmetat, p["sink_rows"])


def _merge_kernel(h32_ref, ya_ref, yb_ref, *refs):
    wga_refs, wgb_refs = refs[:MERGE_BLOCKS], refs[MERGE_BLOCKS:2 * MERGE_BLOCKS]
    bmg_ref, wo_ref, bo_ref, g2_ref, b2_ref, out_ref = refs[2 * MERGE_BLOCKS:]
    h = h32_ref[0].astype(BF16)
    mixed = []
    for c in range(MERGE_BLOCKS):
        cs = slice(c * MERGE_COLS, (c + 1) * MERGE_COLS)
        gs = slice(D_MODEL + c * MERGE_COLS, D_MODEL + (c + 1) * MERGE_COLS)
        ga = _sigmoid(_dot(h, wga_refs[c][...]) + bmg_ref[:, cs])
        gb = _sigmoid(_dot(h, wgb_refs[c][...]) + bmg_ref[:, gs])
        mixed.append(ga * ya_ref[0, :, cs].astype(F32) + gb * yb_ref[0, :, cs].astype(F32))
    mixed = jnp.concatenate(mixed, axis=1)
    out = _dot(mixed.astype(BF16), wo_ref[...]) + bo_ref[...]
    out_ref[0] = _layer_norm(DEEPNORM_ALPHA * h32_ref[0] + out, g2_ref[...], b2_ref[...])


def _merge(h32, ya, yb, p, *, tm):
    b, t_len, _ = h32.shape
    tile = pl.BlockSpec((1, tm, D_MODEL), lambda bi, i: (bi, i, 0))
    w_in_block = lambda c: pl.BlockSpec((D_MODEL, MERGE_COLS), lambda bi, i: (0, c), pipeline_mode=pl.Buffered(1))
    in_specs = [tile, tile, tile,
                *[w_in_block(OFF_G // MERGE_COLS + c) for c in range(MERGE_BLOCKS)],
                *[w_in_block((OFF_G + D_MODEL) // MERGE_COLS + c) for c in range(MERGE_BLOCKS)],
                _const_spec((1, 2 * D_MODEL)),
                _const_spec((D_MODEL, D_MODEL)), _const_spec((1, D_MODEL)),
                _const_spec((1, D_MODEL)), _const_spec((1, D_MODEL))]
    return pl.pallas_call(
        _merge_kernel,
        grid=(b, t_len // tm),
        in_specs=in_specs,
        out_specs=tile,
        out_shape=jax.ShapeDtypeStruct((b, t_len, D_MODEL), F32),
        compiler_params=pltpu.CompilerParams(dimension_semantics=("arbitrary", "arbitrary"),
                                             vmem_limit_bytes=V7X_VMEM_LIMIT),
        name="merge",
    )(h32, ya, yb, *([p["w_in"]] * (2 * MERGE_BLOCKS)), p["bmg"], p["w_o"], p["b_o"],
      p["ln2_g"], p["ln2_b"])


def _rope_tables(t_total):
    inv = (ROPE_THETA ** (-np.arange(HALF, dtype=np.float32) / HALF)).astype(np.float32)
    ang = np.arange(t_total, dtype=np.float32)[:, None] * inv[None, :]
    cos, sin = np.cos(ang), np.sin(ang)
    zeros = np.zeros_like(cos)
    cos_r = np.concatenate([cos, cos, zeros, zeros], axis=1)
    sin_r = np.concatenate([-sin, sin, zeros, zeros], axis=1)
    return cos, sin, cos_r, sin_r


def _slotted(w):
    lead = w.shape[:-1]
    w4 = w.reshape(lead + (N_KV_HEADS, HEAD_DIM))
    w4 = jnp.concatenate([w4, jnp.zeros_like(w4)], axis=-1)
    return w4.reshape(lead + (N_KV_HEADS * KV_SLOT,))


def kernel(x, meta_tokens, ln_emb_g, ln_emb_b, w_in, b_in, conv_w, conv_b, w_ra, b_ra, w_ri, b_ri,
           lru_lambda, sinks, w_rnn_out, w_attn_out, w_o, b_o, ln_g, ln_b):
    b, seq, _ = x.shape
    assert b == V7X_SUBLANES and w_in.shape[0] == DEPTH
    nb, rb = N_RNN_BLOCKS, RNN_BLOCK
    w = w_in[0].astype(BF16)
    bi = b_in[0]
    row = lambda v: v.reshape(1, -1)
    blocks = lambda v: v.reshape(nb, 1, rb)

    common = {"ln_g": row(ln_emb_g), "ln_b": row(ln_emb_b)}
    p_rnn = dict(common, w_in=w,
                 bx=blocks(bi[:OFF_GR]), bg=blocks(bi[OFF_GR:OFF_Q]),
                 cw=conv_w[0].reshape(CONV_WIDTH, nb, rb).transpose(1, 0, 2), cb=blocks(conv_b[0]),
                 wra=w_ra[0].astype(BF16), bra=blocks(b_ra[0]),
                 wri=w_ri[0].astype(BF16), bri=blocks(b_ri[0]),
                 lam=blocks(lru_lambda[0]), w_rnn_out=w_rnn_out[0].astype(BF16))
    p_attn = dict(common,
                  w_in=w, bq=row(bi[OFF_Q:OFF_K]), bga=row(bi[OFF_GA:OFF_G]),
                  wk=_slotted(w[:, OFF_K:OFF_V]), bk=row(_slotted(bi[OFF_K:OFF_V])),
                  wvt=w[:, OFF_V:OFF_GA].T,
                  bvt=jnp.broadcast_to(bi[OFF_V:OFF_GA, None], (D_KV, V7X_LANES)),
                  wv=w[:, OFF_V:OFF_GA], bv=row(bi[OFF_V:OFF_GA]),
                  w_attn_out=w_attn_out[0].astype(BF16),
                  sink_rows=jnp.repeat(sinks[0].astype(F32), BLOCK).reshape(N_KV_HEADS, 1, GROUP * BLOCK))
    p_merge = dict(common, w_in=w, bmg=row(bi[OFF_G:]),
                   w_o=w_o[0].astype(BF16), b_o=row(b_o[0]), ln2_g=row(ln_g[0]), ln2_b=row(ln_b[0]))

    cos, sin, cos_r, sin_r = _rope_tables(N_META + seq)
    tabs = {"cos_t": np.ascontiguousarray(cos[N_META:].T), "sin_t": np.ascontiguousarray(sin[N_META:].T),
            "cos_r": cos_r[N_META:], "sin_r": sin_r[N_META:]}

    meta = meta_tokens.astype(x.dtype)
    meta_b = jnp.broadcast_to(meta[None], (V7X_SUBLANES, N_META, D_MODEL))
    zero_hist = jnp.zeros((nb, HIST_ROWS, rb), F32)
    zero_state = jnp.zeros((nb, V7X_SUBLANES, rb), F32)
    _, hist0, state0 = _rnn_branch(meta_b, p_rnn, zero_hist, zero_state, tt=N_META, seq_start=True)
    kmeta, vmeta = _meta_kv(meta, p_attn, cos_r[:N_META], sin_r[:N_META])

    yb, h, h32 = _attn_branch(x, p_attn, tabs, kmeta, vmeta.T, tq=ATTN_TILE)
    ya, _, _ = _rnn_branch(h, p_rnn, hist0, state0, tt=RNN_TILE_STEPS, seq_start=False)
    return _merge(h32, ya, yb, p_merge, tm=MERGE_TILE)
```

```python
import functools

import jax
import jax.numpy as jnp
import numpy as np
from jax import lax
from jax.experimental import pallas as pl
from jax.experimental.pallas import tpu as pltpu

D_MODEL = 2048
N_META = 16
N_RNN_BLOCKS = 8
RNN_BLOCK = 256
CONV_WIDTH = 4
LRU_C = 8.0
HEAD_DIM = 64
HALF = HEAD_DIM // 2
N_Q_HEADS = 32
N_KV_HEADS = 4
GROUP = 8
D_KV = N_KV_HEADS * HEAD_DIM
BLOCK = 128
ROPE_THETA = 10000.0
NEG_INF = -1e30
LN_EPS = 1e-5
DEPTH = 1
DEEPNORM_ALPHA = (2.0 * DEPTH) ** 0.25
OFF_XR = 0
OFF_GR = D_MODEL
OFF_Q = 2 * D_MODEL
OFF_K = OFF_Q + D_MODEL
OFF_V = OFF_K + D_KV
OFF_GA = OFF_V + D_KV
OFF_G = OFF_GA + D_MODEL
W_REST_OFF = OFF_Q

V7X_SUBLANES = 8
V7X_LANES = 128
V7X_VMEM_LIMIT = 56 * 1024 * 1024

ATTN_TILE = 2 * BLOCK
RNN_TILE_STEPS = 64
MERGE_TILE = 256

KV_SLOT = V7X_LANES
HIST_ROWS = (CONV_WIDTH - 1) * V7X_SUBLANES
RNN_PERM_STEPS = 32
MERGE_COLS = 512
MERGE_BLOCKS = D_MODEL // MERGE_COLS

BF16 = jnp.bfloat16
F32 = jnp.float32


def _layer_norm(x, g, b):
    mu = jnp.mean(x, axis=-1, keepdims=True)
    xc = x - mu
    var = jnp.mean(xc * xc, axis=-1, keepdims=True)
    return xc * lax.rsqrt(var + LN_EPS) * g + b


def _sigmoid(x):
    return 1.0 / (1.0 + jnp.exp(-x))


def _dot(a, b):
    return jnp.dot(a, b, preferred_element_type=F32)


def _dot_nt(a, b):
    return lax.dot_general(a, b, (((1,), (1,)), ((), ())), preferred_element_type=F32)


def _const_spec(shape):
    nd = len(shape)
    return pl.BlockSpec(shape, lambda *_: (0,) * nd, pipeline_mode=pl.Buffered(1))


def _rnn_kernel(x_ref, lng_ref, lnb_ref, *refs, tt, seq_start, normalised_input):
    wx_refs, wg_refs = refs[:N_RNN_BLOCKS], refs[N_RNN_BLOCKS:2 * N_RNN_BLOCKS]
    (bx_ref, bg_ref, cw_ref, cb_ref, wra_ref, bra_ref, wri_ref, bri_ref, lam_ref, wout_ref, perm_ref,
     permt_ref, hist0_ref, state0_ref, ya_ref, hist_out_ref, state_out_ref,
     y_s, hist_s, state_s) = refs[2 * N_RNN_BLOCKS:]
    rows = tt * V7X_SUBLANES
    nb = N_RNN_BLOCKS
    j = pl.program_id(0)

    @pl.when(j == 0)
    def _():
        y_s[...] = jnp.zeros_like(y_s)
        hist_s[...] = hist0_ref[...]
        state_s[...] = state0_ref[...]

    ts = perm_ref.shape[0] // V7X_SUBLANES
    parts = []
    for s0 in range(0, tt, ts):
        hs_ = x_ref[:, s0:s0 + ts, :].reshape(ts * V7X_SUBLANES, D_MODEL)
        if not normalised_input:
            hs_ = _layer_norm(hs_, lng_ref[...], lnb_ref[...]).astype(BF16)
        parts.append(_dot(perm_ref[...], hs_).astype(BF16))
    h = jnp.concatenate(parts, axis=0)

    def project(n):
        return _dot(h, wx_refs[n][...]) + bx_ref[n], _dot(h, wg_refs[n][...]) + bg_ref[n]

    sub = ts * V7X_SUBLANES
    y_prev = y_s[...].astype(BF16)

    def project_out(n):
        cs = slice(n * RNN_BLOCK, (n + 1) * RNN_BLOCK)
        ya = _dot(y_prev, wout_ref[:, cs])
        for k, s0 in enumerate(range(0, tt, ts)):
            ya_ref[:, s0:s0 + ts, cs] = (ya[k * sub:(k + 1) * sub]
                                         .reshape(V7X_SUBLANES, ts, RNN_BLOCK).astype(ya_ref.dtype))

    ys = []
    nxt = project(0)
    for n in range(nb):
        xr, gr = nxt
        if n + 1 < nb:
            nxt = project(n + 1)
        project_out(n)
        xe = jnp.concatenate([hist_s[n], xr], axis=0)
        cw = cw_ref[n]
        conv = cb_ref[n] + cw[0:1] * xr
        for k in range(1, CONV_WIDTH):
            off = HIST_ROWS - k * V7X_SUBLANES
            conv = conv + cw[k:k + 1] * xe[off:off + rows]
        hist_s[n] = xe[rows:rows + HIST_ROWS]
        cb16 = conv.astype(BF16)
        gate_r = _sigmoid(_dot(cb16, wra_ref[n]) + bra_ref[n])
        gate_i = _sigmoid(_dot(cb16, wri_ref[n]) + bri_ref[n])
        lam = lam_ref[n]
        log_sig = jnp.minimum(lam, 0.0) - jnp.log1p(jnp.exp(-jnp.abs(lam)))
        a = jnp.exp((LRU_C * gate_r) * log_sig)
        mult = jnp.sqrt(1.0 - a * a)
        if seq_start:
            first = lax.broadcasted_iota(jnp.int32, (rows, RNN_BLOCK), 0) < V7X_SUBLANES
            mult = jnp.where(first & (j == 0), 1.0, mult)
        u = mult * gate_i * conv
        hcur = state_s[n]
        hs = []
        for t in range(tt):
            sl = slice(t * V7X_SUBLANES, (t + 1) * V7X_SUBLANES)
            hcur = a[sl] * hcur + u[sl]
            hs.append(hcur)
        state_s[n] = hcur
        hr = jnp.concatenate(hs, axis=0)
        ys.append((hr * (gr * _sigmoid(gr))).astype(BF16))
    y = jnp.concatenate(ys, axis=1)
    for r0 in range(0, rows, sub):
        y_s[r0:r0 + sub, :] = _dot(permt_ref[...], y[r0:r0 + sub])

    @pl.when(j == pl.num_programs(0) - 2)
    def _():
        hist_out_ref[...] = hist_s[...]
        state_out_ref[...] = state_s[...]


def _rnn_branch(x, p, hist0, state0, *, tt, seq_start):
    t_len = x.shape[1]
    nb, rb = N_RNN_BLOCKS, RNN_BLOCK
    rows = tt * V7X_SUBLANES
    ts = min(tt, RNN_PERM_STEPS)
    prow = ts * V7X_SUBLANES
    r = np.arange(prow)
    perm = np.zeros((prow, prow), np.float32)
    perm[r, (r % V7X_SUBLANES) * ts + r // V7X_SUBLANES] = 1.0
    perm, perm_t = jnp.asarray(perm, BF16), jnp.asarray(perm.T, BF16)
    assert x.dtype in (F32, BF16)
    kern = functools.partial(_rnn_kernel, tt=tt, seq_start=seq_start, normalised_input=x.dtype == BF16)
    n_tiles = t_len // tt
    w_in_block = lambda c: pl.BlockSpec((D_MODEL, rb), lambda j: (0, c), pipeline_mode=pl.Buffered(1))
    in_specs = [
        pl.BlockSpec((V7X_SUBLANES, tt, D_MODEL), lambda j: (0, jnp.minimum(j, n_tiles - 1), 0)),
        _const_spec((1, D_MODEL)), _const_spec((1, D_MODEL)),
        *[w_in_block(OFF_XR // rb + n) for n in range(nb)],
        *[w_in_block(OFF_GR // rb + n) for n in range(nb)],
        _const_spec((nb, 1, rb)), _const_spec((nb, 1, rb)),
        _const_spec((nb, CONV_WIDTH, rb)), _const_spec((nb, 1, rb)),
        _const_spec((nb, rb, rb)), _const_spec((nb, 1, rb)),
        _const_spec((nb, rb, rb)), _const_spec((nb, 1, rb)),
        _const_spec((nb, 1, rb)),
        _const_spec((D_MODEL, D_MODEL)),
        _const_spec((prow, prow)), _const_spec((prow, prow)),
        _const_spec((nb, HIST_ROWS, rb)), _const_spec((nb, V7X_SUBLANES, rb)),
    ]
    out_shape = (
        jax.ShapeDtypeStruct((V7X_SUBLANES, t_len, D_MODEL), BF16),
        jax.ShapeDtypeStruct((nb, HIST_ROWS, rb), F32),
        jax.ShapeDtypeStruct((nb, V7X_SUBLANES, rb), F32),
    )
    out_specs = (
        pl.BlockSpec((V7X_SUBLANES, tt, D_MODEL), lambda j: (0, jnp.maximum(j - 1, 0), 0)),
        pl.BlockSpec((nb, HIST_ROWS, rb), lambda j: (0, 0, 0)),
        pl.BlockSpec((nb, V7X_SUBLANES, rb), lambda j: (0, 0, 0)),
    )
    return pl.pallas_call(
        kern,
        grid=(n_tiles + 1,),
        in_specs=in_specs,
        out_specs=out_specs,
        out_shape=out_shape,
        scratch_shapes=[pltpu.VMEM((rows, D_MODEL), F32),
                        pltpu.VMEM((nb, HIST_ROWS, rb), F32), pltpu.VMEM((nb, V7X_SUBLANES, rb), F32)],
        compiler_params=pltpu.CompilerParams(dimension_semantics=("arbitrary",),
                                             vmem_limit_bytes=V7X_VMEM_LIMIT),
        name="rnn_branch_start" if seq_start else "rnn_branch",
    )(x, p["ln_g"], p["ln_b"], *([p["w_in"]] * (2 * nb)), p["bx"], p["bg"], p["cw"], p["cb"],
      p["wra"], p["bra"], p["wri"], p["bri"], p["lam"], p["w_rnn_out"], perm, perm_t, hist0, state0)


def _rope_rows(k, cos, sin_signed):
    lane = lax.broadcasted_iota(jnp.int32, k.shape, 1)
    partner = jnp.where((lane % HEAD_DIM) < HALF,
                        pltpu.roll(k, V7X_LANES - HALF, axis=1),
                        pltpu.roll(k, HALF, axis=1))
    return k * cos + partner * sin_signed


def _meta_kv_kernel(m_ref, lng_ref, lnb_ref, wk_ref, bk_ref, wv_ref, bv_ref, cos_ref, sin_ref,
                    k_ref, v_ref):
    h = _layer_norm(m_ref[...], lng_ref[...], lnb_ref[...]).astype(BF16)
    k = _dot(h, wk_ref[...]) + bk_ref[...]
    v = _dot(h, wv_ref[...]) + bv_ref[...]
    for g in range(N_KV_HEADS):
        sl = slice(g * KV_SLOT, (g + 1) * KV_SLOT)
        k_ref[:, sl] = _rope_rows(k[:, sl], cos_ref[...], sin_ref[...]).astype(BF16)
    v_ref[...] = v.astype(BF16)


def _meta_kv(meta, p, cos_rows, sin_rows):
    return pl.pallas_call(
        _meta_kv_kernel,
        out_shape=(jax.ShapeDtypeStruct((N_META, N_KV_HEADS * KV_SLOT), BF16),
                   jax.ShapeDtypeStruct((N_META, D_KV), BF16)),
        name="meta_kv",
    )(meta, p["ln_g"], p["ln_b"], p["wk"], p["bk"], p["wv"], p["bv"], cos_rows, sin_rows)


def _attn_kernel(x_ref, lng_ref, lnb_ref, *refs, tq, n_t):
    wq_refs, wga_refs = refs[:N_KV_HEADS], refs[N_KV_HEADS:2 * N_KV_HEADS]
    (bq_ref, bga_ref, wk_ref, bk_ref, wvt_ref, bvt_ref, wout_ref, cost_ref, sint_ref, cosr_ref, sinr_ref,
     kmeta_ref, vmetat_ref, sink_ref, wxg32_ref, wro32_ref, wo32_ref,
     yb_ref, h_ref, h32_ref, wxg_ref, wro_ref, wo_ref,
     qt_s, k_s, vt_s, gat_s, kprev_s, vprev_s, ot_s, act_s) = refs[2 * N_KV_HEADS:]
    nblk = tq // BLOCK
    rows_g = GROUP * HEAD_DIM
    j = pl.program_id(0)

    @pl.when(j == 0)
    def _():
        for ref in (qt_s, k_s, vt_s, gat_s, kprev_s, vprev_s, act_s):
            ref[...] = jnp.zeros_like(ref)

    h32 = _layer_norm(x_ref[0], lng_ref[...], lnb_ref[...])
    h = h32.astype(BF16)
    h32_ref[0] = h32
    h_ref[0] = h

    wxg_ref[...] = wxg32_ref[...].astype(BF16)
    wro_ref[...] = wro32_ref[...].astype(BF16)
    wo_ref[...] = wo32_ref[...].astype(BF16)

    act_prev = act_s[...].astype(BF16)
    out_cols = D_MODEL // (N_KV_HEADS * nblk)

    def project_out(c):
        cs = slice(c * out_cols, (c + 1) * out_cols)
        yb_ref[0, :, cs] = _dot(act_prev, wout_ref[:, cs]).astype(yb_ref.dtype)

    scale = HEAD_DIM ** -0.5
    cos = cost_ref[...] * scale
    sin = sint_ref[...] * scale

    def project_q(g):
        rs = slice(g * rows_g, (g + 1) * rows_g)
        qt = (_dot(h, wq_refs[g][...]) + bq_ref[:, rs]).T
        out = []
        for hh in range(GROUP):
            r0 = hh * HEAD_DIM
            q1, q2 = qt[r0:r0 + HALF], qt[r0 + HALF:r0 + HEAD_DIM]
            out.append((q1 * cos - q2 * sin, q2 * cos + q1 * sin))
        return out

    def store_q(g, roped):
        for hh, (lo, hi) in enumerate(roped):
            r0 = g * rows_g + hh * HEAD_DIM
            qt_s[r0:r0 + HALF, :] = lo
            qt_s[r0 + HALF:r0 + HEAD_DIM, :] = hi

    def project_gate(g):
        rs = slice(g * rows_g, (g + 1) * rows_g)
        gat = _dot(h, wga_refs[g][...]) + bga_ref[:, rs]
        return gat * _sigmoid(gat)

    i = jnp.maximum(j - 1, 0) % n_t
    k = k_s[...].astype(BF16)
    vt = vt_s[...].astype(BF16)
    k_carry, vt_carry = kprev_s[...].astype(BF16), vprev_s[...].astype(BF16)
    kmeta = kmeta_ref[...]
    vmetat = vmetat_ref[...]

    key_row = lax.broadcasted_iota(jnp.int32, (BLOCK, GROUP * BLOCK), 0)
    qry_col = lax.broadcasted_iota(jnp.int32, (BLOCK, GROUP * BLOCK), 1) % BLOCK
    cur_ok = key_row <= qry_col
    first_prev_ok = (key_row > qry_col) & (i > 0)

    def scores(g, jb):
        tok = slice(jb * BLOCK, (jb + 1) * BLOCK)
        ks = slice(g * KV_SLOT, g * KV_SLOT + HEAD_DIM)
        k_prev = k_carry if jb == 0 else k[(jb - 1) * BLOCK:jb * BLOCK]
        q_g = jnp.concatenate(
            [qt_s[(g * GROUP + hh) * HEAD_DIM:(g * GROUP + hh + 1) * HEAD_DIM, tok].astype(BF16)
             for hh in range(GROUP)], axis=1)
        s_prev = _dot(k_prev[:, ks], q_g)
        if jb == 0:
            s_prev = jnp.where(first_prev_ok, s_prev, NEG_INF)
        s_cm = _dot(jnp.concatenate([k[tok, ks], kmeta[:, ks]], axis=0), q_g)
        s_band = jnp.where(cur_ok, s_cm[:BLOCK], s_prev)
        return s_band, s_cm[BLOCK:]

    def finish(g, jb, s):
        s_band, s_meta = s
        tok = slice(jb * BLOCK, (jb + 1) * BLOCK)
        vs = slice(g * HEAD_DIM, (g + 1) * HEAD_DIM)
        vt_prev = vt_carry if jb == 0 else vt[:, (jb - 1) * BLOCK:jb * BLOCK]
        sink = sink_ref[g]
        m = jnp.maximum(jnp.maximum(jnp.max(s_band, axis=0, keepdims=True),
                                    jnp.max(s_meta, axis=0, keepdims=True)), sink)
        p_band = jnp.exp(s_band - m)
        p_meta = jnp.exp(s_meta - m)
        denom = (jnp.sum(p_band, axis=0, keepdims=True) + jnp.sum(p_meta, axis=0, keepdims=True)
                 + jnp.exp(sink - m))
        p_cur = jnp.where(cur_ok, p_band, 0.0)
        p_prev = jnp.where(cur_ok, 0.0, p_band)
        p_cm = jnp.concatenate([p_cur, p_meta], axis=0).astype(BF16)
        v_cm = jnp.concatenate([vt[vs, tok], vmetat[vs]], axis=1)
        o = _dot(vt_prev[vs], p_prev.astype(BF16)) + _dot(v_cm, p_cm)
        o = o * (1.0 / denom)
        for hh in range(GROUP):
            r0 = (g * GROUP + hh) * HEAD_DIM
            ot_s[r0:r0 + HEAD_DIM, tok] = o[:, hh * BLOCK:(hh + 1) * BLOCK]

    gates = [None] * N_KV_HEADS
    held_q = None
    for g in range(N_KV_HEADS):
        pg = (g - 1) % N_KV_HEADS
        for jb in range(nblk):
            s = scores(g, jb)
            if jb == 0:
                roped = project_q(pg)
                if pg < g:
                    store_q(pg, roped)
                else:
                    held_q = roped
            elif jb == 1:
                gates[pg] = project_gate(pg)
            project_out(g * nblk + jb)
            finish(g, jb, s)
    store_q(N_KV_HEADS - 1, held_q)

    k_new = _dot(h, wk_ref[...]) + bk_ref[...]
    vt_new = _dot_nt(wvt_ref[...], h) + jnp.concatenate([bvt_ref[...]] * (tq // V7X_LANES), axis=1)

    act_s[...] = ot_s[...].T * gat_s[...]

    kprev_s[...] = k_s[tq - BLOCK:, :]
    vprev_s[...] = vt_s[:, tq - BLOCK:]
    for g in range(N_KV_HEADS):
        sl = slice(g * KV_SLOT, (g + 1) * KV_SLOT)
        k_s[:, sl] = _rope_rows(k_new[:, sl], cosr_ref[...], sinr_ref[...])
        gat_s[:, g * rows_g:(g + 1) * rows_g] = gates[g]
    vt_s[...] = vt_new


def _attn_branch(x, p, tabs, kmeta, vmetat, later_weights, *, tq):
    b, t_len, _ = x.shape
    assert tq == 2 * BLOCK
    n_t = t_len // tq
    n_tiles = b * n_t
    kern = functools.partial(_attn_kernel, tq=tq, n_t=n_t)
    proj_tile = lambda j: jnp.minimum(j, n_tiles - 1)
    out_tile = lambda j: jnp.maximum(j - 2, 0)
    cols_g = GROUP * HEAD_DIM
    w_in_block = lambda c: pl.BlockSpec((D_MODEL, cols_g), lambda j: (0, c), pipeline_mode=pl.Buffered(1))
    cast_rows = D_MODEL // n_tiles
    assert cast_rows * n_tiles == D_MODEL and cast_rows % 16 == 0
    cast_spec = lambda cols: pl.BlockSpec((cast_rows, cols), lambda j: (proj_tile(j), 0))
    in_specs = [
        pl.BlockSpec((1, tq, D_MODEL), lambda j: (proj_tile(j) // n_t, proj_tile(j) % n_t, 0)),
        _const_spec((1, D_MODEL)), _const_spec((1, D_MODEL)),
        *[w_in_block((OFF_Q - W_REST_OFF) // cols_g + g) for g in range(N_KV_HEADS)],
        *[w_in_block((OFF_GA - W_REST_OFF) // cols_g + g) for g in range(N_KV_HEADS)],
        _const_spec((1, D_MODEL)), _const_spec((1, D_MODEL)),
        _const_spec((D_MODEL, N_KV_HEADS * KV_SLOT)), _const_spec((1, N_KV_HEADS * KV_SLOT)),
        _const_spec((D_KV, D_MODEL)), _const_spec((D_KV, V7X_LANES)),
        _const_spec((D_MODEL, D_MODEL)),
        pl.BlockSpec((HALF, tq), lambda j: (0, proj_tile(j) % n_t)),
        pl.BlockSpec((HALF, tq), lambda j: (0, proj_tile(j) % n_t)),
        pl.BlockSpec((tq, KV_SLOT), lambda j: (proj_tile(j) % n_t, 0)),
        pl.BlockSpec((tq, KV_SLOT), lambda j: (proj_tile(j) % n_t, 0)),
        _const_spec((N_META, N_KV_HEADS * KV_SLOT)), _const_spec((D_KV, N_META)),
        _const_spec((N_KV_HEADS, 1, GROUP * BLOCK)),
        cast_spec(W_REST_OFF), cast_spec(D_MODEL), cast_spec(D_MODEL),
    ]
    return pl.pallas_call(
        kern,
        grid=(n_tiles + 2,),
        in_specs=in_specs,
        out_specs=(pl.BlockSpec((1, tq, D_MODEL), lambda j: (out_tile(j) // n_t, out_tile(j) % n_t, 0)),
                   pl.BlockSpec((1, tq, D_MODEL), lambda j: (proj_tile(j) // n_t, proj_tile(j) % n_t, 0)),
                   pl.BlockSpec((1, tq, D_MODEL), lambda j: (proj_tile(j) // n_t, proj_tile(j) % n_t, 0)),
                   cast_spec(W_REST_OFF), cast_spec(D_MODEL), cast_spec(D_MODEL)),
        out_shape=(jax.ShapeDtypeStruct((b, t_len, D_MODEL), BF16),
                   jax.ShapeDtypeStruct((b, t_len, D_MODEL), BF16),
                   jax.ShapeDtypeStruct((b, t_len, D_MODEL), F32),
                   jax.ShapeDtypeStruct((D_MODEL, W_REST_OFF), BF16),
                   jax.ShapeDtypeStruct((D_MODEL, D_MODEL), BF16),
                   jax.ShapeDtypeStruct((D_MODEL, D_MODEL), BF16)),
        scratch_shapes=[pltpu.VMEM((D_MODEL, tq), F32),
                        pltpu.VMEM((tq, N_KV_HEADS * KV_SLOT), F32),
                        pltpu.VMEM((D_KV, tq), F32),
                        pltpu.VMEM((tq, D_MODEL), F32),
                        pltpu.VMEM((BLOCK, N_KV_HEADS * KV_SLOT), F32),
                        pltpu.VMEM((D_KV, BLOCK), F32),
                        pltpu.VMEM((D_MODEL, tq), F32),
                        pltpu.VMEM((tq, D_MODEL), F32)],
        compiler_params=pltpu.CompilerParams(dimension_semantics=("arbitrary",),
                                             vmem_limit_bytes=V7X_VMEM_LIMIT),
        name="attn_branch",
    )(x, p["ln_g"], p["ln_b"], *([p["w_in"]] * (2 * N_KV_HEADS)), p["bq"], p["bga"],
      p["wk"], p["bk"], p["wvt"], p["bvt"], p["w_attn_out"], tabs["cos_t"], tabs["sin_t"], tabs["cos_r"], tabs["sin_r"],
      kmeta, vmetat, p["sink_rows"], *later_weights)


def _merge_kernel(h32_ref, ya_ref, yb_ref, *refs):
    wga_refs, wgb_refs = refs[:MERGE_BLOCKS], refs[MERGE_BLOCKS:2 * MERGE_BLOCKS]
    bmg_ref, wo_ref, bo_ref, g2_ref, b2_ref, out_ref = refs[2 * MERGE_BLOCKS:]
    h = h32_ref[0].astype(BF16)
    mixed = []
    for c in range(MERGE_BLOCKS):
        cs = slice(c * MERGE_COLS, (c + 1) * MERGE_COLS)
        gs = slice(D_MODEL + c * MERGE_COLS, D_MODEL + (c + 1) * MERGE_COLS)
        ga = _sigmoid(_dot(h, wga_refs[c][...]) + bmg_ref[:, cs])
        gb = _sigmoid(_dot(h, wgb_refs[c][...]) + bmg_ref[:, gs])
        mixed.append(ga * ya_ref[0, :, cs].astype(F32) + gb * yb_ref[0, :, cs].astype(F32))
    mixed = jnp.concatenate(mixed, axis=1)
    out = _dot(mixed.astype(BF16), wo_ref[...]) + bo_ref[...]
    out_ref[0] = _layer_norm(DEEPNORM_ALPHA * h32_ref[0] + out, g2_ref[...], b2_ref[...])


def _merge(h32, ya, yb, p, *, tm):
    b, t_len, _ = h32.shape
    tile = pl.BlockSpec((1, tm, D_MODEL), lambda bi, i: (bi, i, 0))
    w_in_block = lambda c: pl.BlockSpec((D_MODEL, MERGE_COLS), lambda bi, i: (0, c), pipeline_mode=pl.Buffered(1))
    in_specs = [tile, tile, tile,
                *[w_in_block((OFF_G - W_REST_OFF) // MERGE_COLS + c) for c in range(MERGE_BLOCKS)],
                *[w_in_block((OFF_G - W_REST_OFF + D_MODEL) // MERGE_COLS + c) for c in range(MERGE_BLOCKS)],
                _const_spec((1, 2 * D_MODEL)),
                _const_spec((D_MODEL, D_MODEL)), _const_spec((1, D_MODEL)),
                _const_spec((1, D_MODEL)), _const_spec((1, D_MODEL))]
    return pl.pallas_call(
        _merge_kernel,
        grid=(b, t_len // tm),
        in_specs=in_specs,
        out_specs=tile,
        out_shape=jax.ShapeDtypeStruct((b, t_len, D_MODEL), F32),
        compiler_params=pltpu.CompilerParams(dimension_semantics=("arbitrary", "arbitrary"),
                                             vmem_limit_bytes=V7X_VMEM_LIMIT),
        name="merge",
    )(h32, ya, yb, *([p["w_in"]] * (2 * MERGE_BLOCKS)), p["bmg"], p["w_o"], p["b_o"],
      p["ln2_g"], p["ln2_b"])


def _rope_tables(t_total):
    inv = (ROPE_THETA ** (-np.arange(HALF, dtype=np.float32) / HALF)).astype(np.float32)
    ang = np.arange(t_total, dtype=np.float32)[:, None] * inv[None, :]
    cos, sin = np.cos(ang), np.sin(ang)
    zeros = np.zeros_like(cos)
    cos_r = np.concatenate([cos, cos, zeros, zeros], axis=1)
    sin_r = np.concatenate([-sin, sin, zeros, zeros], axis=1)
    return cos, sin, cos_r, sin_r


def _slotted(w):
    lead = w.shape[:-1]
    w4 = w.reshape(lead + (N_KV_HEADS, HEAD_DIM))
    w4 = jnp.concatenate([w4, jnp.zeros_like(w4)], axis=-1)
    return w4.reshape(lead + (N_KV_HEADS * KV_SLOT,))


def kernel(x, meta_tokens, ln_emb_g, ln_emb_b, w_in, b_in, conv_w, conv_b, w_ra, b_ra, w_ri, b_ri,
           lru_lambda, sinks, w_rnn_out, w_attn_out, w_o, b_o, ln_g, ln_b):
    b, seq, _ = x.shape
    assert b == V7X_SUBLANES and w_in.shape[0] == DEPTH
    nb, rb = N_RNN_BLOCKS, RNN_BLOCK
    w = w_in[0][:, W_REST_OFF:].astype(BF16)
    rel = lambda off: off - W_REST_OFF
    bi = b_in[0]
    row = lambda v: v.reshape(1, -1)
    blocks = lambda v: v.reshape(nb, 1, rb)

    common = {"ln_g": row(ln_emb_g), "ln_b": row(ln_emb_b)}
    p_rnn = dict(common,
                 bx=blocks(bi[:OFF_GR]), bg=blocks(bi[OFF_GR:OFF_Q]),
                 cw=conv_w[0].reshape(CONV_WIDTH, nb, rb).transpose(1, 0, 2), cb=blocks(conv_b[0]),
                 wra=w_ra[0].astype(BF16), bra=blocks(b_ra[0]),
                 wri=w_ri[0].astype(BF16), bri=blocks(b_ri[0]),
                 lam=blocks(lru_lambda[0]))
    p_attn = dict(common,
                  w_in=w, bq=row(bi[OFF_Q:OFF_K]), bga=row(bi[OFF_GA:OFF_G]),
                  wk=_slotted(w[:, rel(OFF_K):rel(OFF_V)]), bk=row(_slotted(bi[OFF_K:OFF_V])),
                  wvt=w[:, rel(OFF_V):rel(OFF_GA)].T,
                  bvt=jnp.broadcast_to(bi[OFF_V:OFF_GA, None], (D_KV, V7X_LANES)),
                  wv=w[:, rel(OFF_V):rel(OFF_GA)], bv=row(bi[OFF_V:OFF_GA]),
                  w_attn_out=w_attn_out[0].astype(BF16),
                  sink_rows=jnp.repeat(sinks[0].astype(F32), BLOCK).reshape(N_KV_HEADS, 1, GROUP * BLOCK))
    p_merge = dict(common, w_in=w, bmg=row(bi[OFF_G:]),
                   b_o=row(b_o[0]), ln2_g=row(ln_g[0]), ln2_b=row(ln_b[0]))

    cos, sin, cos_r, sin_r = _rope_tables(N_META + seq)
    tabs = {"cos_t": np.ascontiguousarray(cos[N_META:].T), "sin_t": np.ascontiguousarray(sin[N_META:].T),
            "cos_r": cos_r[N_META:], "sin_r": sin_r[N_META:]}

    meta = meta_tokens.astype(x.dtype)
    kmeta, vmeta = _meta_kv(meta, p_attn, cos_r[:N_META], sin_r[:N_META])

    yb, h, h32, w_xg, w_ro, w_ob = _attn_branch(x, p_attn, tabs, kmeta, vmeta.T,
                                                (w_in[0], w_rnn_out[0], w_o[0]), tq=ATTN_TILE)
    p_rnn = dict(p_rnn, w_in=w_xg, w_rnn_out=w_ro)
    p_merge = dict(p_merge, w_o=w_ob)

    meta_b = jnp.broadcast_to(meta[None], (V7X_SUBLANES, N_META, D_MODEL))
    zero_hist = jnp.zeros((nb, HIST_ROWS, rb), F32)
    zero_state = jnp.zeros((nb, V7X_SUBLANES, rb), F32)
    _, hist0, state0 = _rnn_branch(meta_b, p_rnn, zero_hist, zero_state, tt=N_META, seq_start=True)
    ya, _, _ = _rnn_branch(h, p_rnn, hist0, state0, tt=RNN_TILE_STEPS, seq_start=False)
    return _merge(h32, ya, yb, p_merge, tm=MERGE_TILE)
```

```python
import functools

import jax
import jax.numpy as jnp
import numpy as np
from jax import lax
from jax.experimental import pallas as pl
from jax.experimental.pallas import tpu as pltpu

D_MODEL = 2048
N_META = 16
N_RNN_BLOCKS = 8
RNN_BLOCK = 256
CONV_WIDTH = 4
LRU_C = 8.0
HEAD_DIM = 64
HALF = HEAD_DIM // 2
N_Q_HEADS = 32
N_KV_HEADS = 4
GROUP = 8
D_KV = N_KV_HEADS * HEAD_DIM
BLOCK = 128
ROPE_THETA = 10000.0
NEG_INF = -1e30
LN_EPS = 1e-5
DEPTH = 1
DEEPNORM_ALPHA = (2.0 * DEPTH) ** 0.25
OFF_XR = 0
OFF_GR = D_MODEL
OFF_Q = 2 * D_MODEL
OFF_K = OFF_Q + D_MODEL
OFF_V = OFF_K + D_KV
OFF_GA = OFF_V + D_KV
OFF_G = OFF_GA + D_MODEL

V7X_SUBLANES = 8
V7X_LANES = 128
V7X_VMEM_LIMIT = 56 * 1024 * 1024

ATTN_TILE = 2 * BLOCK
RNN_TILE_STEPS = 64
MERGE_TILE = 256

HIST_ROWS = (CONV_WIDTH - 1) * V7X_SUBLANES
RNN_PERM_STEPS = 32
MERGE_COLS = 512
MERGE_BLOCKS = D_MODEL // MERGE_COLS

BF16 = jnp.bfloat16
F32 = jnp.float32


def _layer_norm(x, g, b):
    mu = jnp.mean(x, axis=-1, keepdims=True)
    xc = x - mu
    var = jnp.mean(xc * xc, axis=-1, keepdims=True)
    return xc * lax.rsqrt(var + LN_EPS) * g + b


def _sigmoid(x):
    return 1.0 / (1.0 + jnp.exp(-x))


def _dot(a, b):
    return jnp.dot(a, b, preferred_element_type=F32)


def _dot_nt(a, b):
    return lax.dot_general(a, b, (((1,), (1,)), ((), ())), preferred_element_type=F32)


def _const_spec(shape):
    nd = len(shape)
    return pl.BlockSpec(shape, lambda *_: (0,) * nd, pipeline_mode=pl.Buffered(1))


def _rnn_kernel(x_ref, lng_ref, lnb_ref, *refs, tt, seq_start, normalised_input):
    wx_refs, wg_refs = refs[:N_RNN_BLOCKS], refs[N_RNN_BLOCKS:2 * N_RNN_BLOCKS]
    (bx_ref, bg_ref, cw_ref, cb_ref, wra_ref, bra_ref, wri_ref, bri_ref, lam_ref, wout_ref, perm_ref,
     permt_ref, hist0_ref, state0_ref, ya_ref, hist_out_ref, state_out_ref,
     y_s, hist_s, state_s) = refs[2 * N_RNN_BLOCKS:]
    rows = tt * V7X_SUBLANES
    nb = N_RNN_BLOCKS
    j = pl.program_id(0)

    @pl.when(j == 0)
    def _():
        y_s[...] = jnp.zeros_like(y_s)
        hist_s[...] = hist0_ref[...]
        state_s[...] = state0_ref[...]

    ts = perm_ref.shape[0] // V7X_SUBLANES
    parts = []
    for s0 in range(0, tt, ts):
        hs_ = x_ref[:, s0:s0 + ts, :].reshape(ts * V7X_SUBLANES, D_MODEL)
        if not normalised_input:
            hs_ = _layer_norm(hs_, lng_ref[...], lnb_ref[...]).astype(BF16)
        parts.append(_dot(perm_ref[...], hs_).astype(BF16))
    h = jnp.concatenate(parts, axis=0)

    def project(n):
        return _dot(h, wx_refs[n][...]) + bx_ref[n], _dot(h, wg_refs[n][...]) + bg_ref[n]

    sub = ts * V7X_SUBLANES
    y_prev = y_s[...].astype(BF16)

    def project_out(n):
        cs = slice(n * RNN_BLOCK, (n + 1) * RNN_BLOCK)
        ya = _dot(y_prev, wout_ref[:, cs])
        for k, s0 in enumerate(range(0, tt, ts)):
            ya_ref[:, s0:s0 + ts, cs] = (ya[k * sub:(k + 1) * sub]
                                         .reshape(V7X_SUBLANES, ts, RNN_BLOCK).astype(ya_ref.dtype))

    ys = []
    nxt = project(0)
    for n in range(nb):
        xr, gr = nxt
        if n + 1 < nb:
            nxt = project(n + 1)
        project_out(n)
        xe = jnp.concatenate([hist_s[n], xr], axis=0)
        cw = cw_ref[n]
        conv = cb_ref[n] + cw[0:1] * xr
        for k in range(1, CONV_WIDTH):
            off = HIST_ROWS - k * V7X_SUBLANES
            conv = conv + cw[k:k + 1] * xe[off:off + rows]
        hist_s[n] = xe[rows:rows + HIST_ROWS]
        cb16 = conv.astype(BF16)
        gate_r = _sigmoid(_dot(cb16, wra_ref[n]) + bra_ref[n])
        gate_i = _sigmoid(_dot(cb16, wri_ref[n]) + bri_ref[n])
        lam = lam_ref[n]
        log_sig = jnp.minimum(lam, 0.0) - jnp.log1p(jnp.exp(-jnp.abs(lam)))
        a = jnp.exp((LRU_C * gate_r) * log_sig)
        mult = jnp.sqrt(1.0 - a * a)
        if seq_start:
            first = lax.broadcasted_iota(jnp.int32, (rows, RNN_BLOCK), 0) < V7X_SUBLANES
            mult = jnp.where(first & (j == 0), 1.0, mult)
        u = mult * gate_i * conv
        hcur = state_s[n]
        hs = []
        for t in range(tt):
            sl = slice(t * V7X_SUBLANES, (t + 1) * V7X_SUBLANES)
            hcur = a[sl] * hcur + u[sl]
            hs.append(hcur)
        state_s[n] = hcur
        hr = jnp.concatenate(hs, axis=0)
        ys.append((hr * (gr * _sigmoid(gr))).astype(BF16))
    y = jnp.concatenate(ys, axis=1)
    for r0 in range(0, rows, sub):
        y_s[r0:r0 + sub, :] = _dot(permt_ref[...], y[r0:r0 + sub])

    @pl.when(j == pl.num_programs(0) - 2)
    def _():
        hist_out_ref[...] = hist_s[...]
        state_out_ref[...] = state_s[...]


def _rnn_branch(x, p, hist0, state0, *, tt, seq_start):
    t_len = x.shape[1]
    nb, rb = N_RNN_BLOCKS, RNN_BLOCK
    rows = tt * V7X_SUBLANES
    ts = min(tt, RNN_PERM_STEPS)
    prow = ts * V7X_SUBLANES
    r = np.arange(prow)
    perm = np.zeros((prow, prow), np.float32)
    perm[r, (r % V7X_SUBLANES) * ts + r // V7X_SUBLANES] = 1.0
    perm, perm_t = jnp.asarray(perm, BF16), jnp.asarray(perm.T, BF16)
    assert x.dtype in (F32, BF16)
    kern = functools.partial(_rnn_kernel, tt=tt, seq_start=seq_start, normalised_input=x.dtype == BF16)
    n_tiles = t_len // tt
    w_in_block = lambda c: pl.BlockSpec((D_MODEL, rb), lambda j: (0, c), pipeline_mode=pl.Buffered(1))
    in_specs = [
        pl.BlockSpec((V7X_SUBLANES, tt, D_MODEL), lambda j: (0, jnp.minimum(j, n_tiles - 1), 0)),
        _const_spec((1, D_MODEL)), _const_spec((1, D_MODEL)),
        *[w_in_block(OFF_XR // rb + n) for n in range(nb)],
        *[w_in_block(OFF_GR // rb + n) for n in range(nb)],
        _const_spec((nb, 1, rb)), _const_spec((nb, 1, rb)),
        _const_spec((nb, CONV_WIDTH, rb)), _const_spec((nb, 1, rb)),
        _const_spec((nb, rb, rb)), _const_spec((nb, 1, rb)),
        _const_spec((nb, rb, rb)), _const_spec((nb, 1, rb)),
        _const_spec((nb, 1, rb)),
        _const_spec((D_MODEL, D_MODEL)),
        _const_spec((prow, prow)), _const_spec((prow, prow)),
        _const_spec((nb, HIST_ROWS, rb)), _const_spec((nb, V7X_SUBLANES, rb)),
    ]
    out_shape = (
        jax.ShapeDtypeStruct((V7X_SUBLANES, t_len, D_MODEL), BF16),
        jax.ShapeDtypeStruct((nb, HIST_ROWS, rb), F32),
        jax.ShapeDtypeStruct((nb, V7X_SUBLANES, rb), F32),
    )
    out_specs = (
        pl.BlockSpec((V7X_SUBLANES, tt, D_MODEL), lambda j: (0, jnp.maximum(j - 1, 0), 0)),
        pl.BlockSpec((nb, HIST_ROWS, rb), lambda j: (0, 0, 0)),
        pl.BlockSpec((nb, V7X_SUBLANES, rb), lambda j: (0, 0, 0)),
    )
    return pl.pallas_call(
        kern,
        grid=(n_tiles + 1,),
        in_specs=in_specs,
        out_specs=out_specs,
        out_shape=out_shape,
        scratch_shapes=[pltpu.VMEM((rows, D_MODEL), F32),
                        pltpu.VMEM((nb, HIST_ROWS, rb), F32), pltpu.VMEM((nb, V7X_SUBLANES, rb), F32)],
        compiler_params=pltpu.CompilerParams(dimension_semantics=("arbitrary",),
                                             vmem_limit_bytes=V7X_VMEM_LIMIT),
        name="rnn_branch_start" if seq_start else "rnn_branch",
    )(x, p["ln_g"], p["ln_b"], *([p["w_in"]] * (2 * nb)), p["bx"], p["bg"], p["cw"], p["cb"],
      p["wra"], p["bra"], p["wri"], p["bri"], p["lam"], p["w_rnn_out"], perm, perm_t, hist0, state0)


def _rope_rows(k, cos, sin_signed):
    lane = lax.broadcasted_iota(jnp.int32, k.shape, 1)
    partner = jnp.where((lane % HEAD_DIM) < HALF,
                        pltpu.roll(k, V7X_LANES - HALF, axis=1),
                        pltpu.roll(k, HALF, axis=1))
    return k * cos + partner * sin_signed


def _meta_kv_kernel(m_ref, lng_ref, lnb_ref, wk_ref, bk_ref, wv_ref, bv_ref, cos_ref, sin_ref,
                    k_ref, v_ref):
    h = _layer_norm(m_ref[...], lng_ref[...], lnb_ref[...]).astype(BF16)
    k = _dot(h, wk_ref[...]) + bk_ref[...]
    v = _dot(h, wv_ref[...]) + bv_ref[...]
    for c in range(D_KV // V7X_LANES):
        sl = slice(c * V7X_LANES, (c + 1) * V7X_LANES)
        k_ref[:, sl] = _rope_rows(k[:, sl], cos_ref[...], sin_ref[...]).astype(BF16)
    v_ref[...] = v.astype(BF16)


def _meta_kv(meta, p, cos_rows, sin_rows):
    return pl.pallas_call(
        _meta_kv_kernel,
        out_shape=(jax.ShapeDtypeStruct((N_META, D_KV), BF16),
                   jax.ShapeDtypeStruct((N_META, D_KV), BF16)),
        name="meta_kv",
    )(meta, p["ln_g"], p["ln_b"], p["wk"], p["bk"], p["wv"], p["bv"], cos_rows, sin_rows)


def _attn_kernel(x_ref, lng_ref, lnb_ref, *refs, tq, n_t):
    wq_refs, wga_refs = refs[:N_KV_HEADS], refs[N_KV_HEADS:2 * N_KV_HEADS]
    (bq_ref, bga_ref, wk_ref, bk_ref, wvt_ref, bvt_ref, wout_ref, cost_ref, sint_ref, cosr_ref, sinr_ref,
     kmeta_ref, vmetat_ref, sink_ref, yb_ref, h_ref, h32_ref,
     qt_s, k_s, vt_s, gat_s, kprev_s, vprev_s, ot_s, act_s) = refs[2 * N_KV_HEADS:]
    nblk = tq // BLOCK
    rows_g = GROUP * HEAD_DIM
    j = pl.program_id(0)

    @pl.when(j == 0)
    def _():
        for ref in (qt_s, k_s, vt_s, gat_s, kprev_s, vprev_s, act_s):
            ref[...] = jnp.zeros_like(ref)

    h32 = _layer_norm(x_ref[0], lng_ref[...], lnb_ref[...])
    h = h32.astype(BF16)
    h32_ref[0] = h32
    h_ref[0] = h

    act_prev = act_s[...].astype(BF16)
    out_cols = D_MODEL // (N_KV_HEADS * nblk)

    def project_out(c):
        cs = slice(c * out_cols, (c + 1) * out_cols)
        yb_ref[0, :, cs] = _dot(act_prev, wout_ref[:, cs]).astype(yb_ref.dtype)

    scale = HEAD_DIM ** -0.5
    cos = cost_ref[...] * scale
    sin = sint_ref[...] * scale

    def project_q(g):
        rs = slice(g * rows_g, (g + 1) * rows_g)
        qt = (_dot(h, wq_refs[g][...]) + bq_ref[:, rs]).T
        out = []
        for hh in range(GROUP):
            r0 = hh * HEAD_DIM
            q1, q2 = qt[r0:r0 + HALF], qt[r0 + HALF:r0 + HEAD_DIM]
            out.append((q1 * cos - q2 * sin, q2 * cos + q1 * sin))
        return out

    def store_q(g, roped):
        for hh, (lo, hi) in enumerate(roped):
            r0 = g * rows_g + hh * HEAD_DIM
            qt_s[r0:r0 + HALF, :] = lo
            qt_s[r0 + HALF:r0 + HEAD_DIM, :] = hi

    def project_gate(g):
        rs = slice(g * rows_g, (g + 1) * rows_g)
        gat = _dot(h, wga_refs[g][...]) + bga_ref[:, rs]
        return gat * _sigmoid(gat)

    i = jnp.maximum(j - 1, 0) % n_t
    k = k_s[...].astype(BF16)
    vt = vt_s[...].astype(BF16)
    k_carry, vt_carry = kprev_s[...].astype(BF16), vprev_s[...].astype(BF16)
    kmeta = kmeta_ref[...]
    vmetat = vmetat_ref[...]

    key_row = lax.broadcasted_iota(jnp.int32, (BLOCK, GROUP * BLOCK), 0)
    qry_col = lax.broadcasted_iota(jnp.int32, (BLOCK, GROUP * BLOCK), 1) % BLOCK
    cur_ok = key_row <= qry_col
    first_prev_ok = (key_row > qry_col) & (i > 0)

    def scores(g, jb):
        tok = slice(jb * BLOCK, (jb + 1) * BLOCK)
        ks = slice(g * HEAD_DIM, (g + 1) * HEAD_DIM)
        k_prev = k_carry if jb == 0 else k[(jb - 1) * BLOCK:jb * BLOCK]
        q_g = jnp.concatenate(
            [qt_s[(g * GROUP + hh) * HEAD_DIM:(g * GROUP + hh + 1) * HEAD_DIM, tok].astype(BF16)
             for hh in range(GROUP)], axis=1)
        s_prev = _dot(k_prev[:, ks], q_g)
        if jb == 0:
            s_prev = jnp.where(first_prev_ok, s_prev, NEG_INF)
        s_cm = _dot(jnp.concatenate([k[tok, ks], kmeta[:, ks]], axis=0), q_g)
        s_band = jnp.where(cur_ok, s_cm[:BLOCK], s_prev)
        return s_band, s_cm[BLOCK:]

    def finish(g, jb, s):
        s_band, s_meta = s
        tok = slice(jb * BLOCK, (jb + 1) * BLOCK)
        vs = slice(g * HEAD_DIM, (g + 1) * HEAD_DIM)
        vt_prev = vt_carry if jb == 0 else vt[:, (jb - 1) * BLOCK:jb * BLOCK]
        sink = sink_ref[g]
        m = jnp.maximum(jnp.maximum(jnp.max(s_band, axis=0, keepdims=True),
                                    jnp.max(s_meta, axis=0, keepdims=True)), sink)
        p_band = jnp.exp(s_band - m)
        p_meta = jnp.exp(s_meta - m)
        denom = (jnp.sum(p_band, axis=0, keepdims=True) + jnp.sum(p_meta, axis=0, keepdims=True)
                 + jnp.exp(sink - m))
        p_cur = jnp.where(cur_ok, p_band, 0.0)
        p_prev = jnp.where(cur_ok, 0.0, p_band)
        p_cm = jnp.concatenate([p_cur, p_meta], axis=0).astype(BF16)
        v_cm = jnp.concatenate([vt[vs, tok], vmetat[vs]], axis=1)
        o = _dot(vt_prev[vs], p_prev.astype(BF16)) + _dot(v_cm, p_cm)
        o = o * (1.0 / denom)
        for hh in range(GROUP):
            r0 = (g * GROUP + hh) * HEAD_DIM
            ot_s[r0:r0 + HEAD_DIM, tok] = o[:, hh * BLOCK:(hh + 1) * BLOCK]

    gates = [None] * N_KV_HEADS
    held_q = None
    for g in range(N_KV_HEADS):
        pg = (g - 1) % N_KV_HEADS
        for jb in range(nblk):
            s = scores(g, jb)
            if jb == 0:
                roped = project_q(pg)
                if pg < g:
                    store_q(pg, roped)
                else:
                    held_q = roped
            elif jb == 1:
                gates[pg] = project_gate(pg)
            project_out(g * nblk + jb)
            finish(g, jb, s)
    store_q(N_KV_HEADS - 1, held_q)

    k_new = _dot(h, wk_ref[...]) + bk_ref[...]
    vt_new = _dot_nt(wvt_ref[...], h) + jnp.concatenate([bvt_ref[...]] * (tq // V7X_LANES), axis=1)

    act_s[...] = ot_s[...].T * gat_s[...]

    kprev_s[...] = k_s[tq - BLOCK:, :]
    vprev_s[...] = vt_s[:, tq - BLOCK:]
    for c in range(D_KV // V7X_LANES):
        sl = slice(c * V7X_LANES, (c + 1) * V7X_LANES)
        k_s[:, sl] = _rope_rows(k_new[:, sl], cosr_ref[...], sinr_ref[...])
    for g in range(N_KV_HEADS):
        gat_s[:, g * rows_g:(g + 1) * rows_g] = gates[g]
    vt_s[...] = vt_new


def _attn_branch(x, p, tabs, kmeta, vmetat, *, tq):
    b, t_len, _ = x.shape
    assert tq == 2 * BLOCK
    n_t = t_len // tq
    n_tiles = b * n_t
    kern = functools.partial(_attn_kernel, tq=tq, n_t=n_t)
    proj_tile = lambda j: jnp.minimum(j, n_tiles - 1)
    out_tile = lambda j: jnp.maximum(j - 2, 0)
    cols_g = GROUP * HEAD_DIM
    w_in_block = lambda c: pl.BlockSpec((D_MODEL, cols_g), lambda j: (0, c), pipeline_mode=pl.Buffered(1))
    in_specs = [
        pl.BlockSpec((1, tq, D_MODEL), lambda j: (proj_tile(j) // n_t, proj_tile(j) % n_t, 0)),
        _const_spec((1, D_MODEL)), _const_spec((1, D_MODEL)),
        *[w_in_block(OFF_Q // cols_g + g) for g in range(N_KV_HEADS)],
        *[w_in_block(OFF_GA // cols_g + g) for g in range(N_KV_HEADS)],
        _const_spec((1, D_MODEL)), _const_spec((1, D_MODEL)),
        pl.BlockSpec((D_MODEL, D_KV), lambda j: (0, OFF_K // D_KV), pipeline_mode=pl.Buffered(1)),
        _const_spec((1, D_KV)),
        _const_spec((D_KV, D_MODEL)), _const_spec((D_KV, V7X_LANES)),
        _const_spec((D_MODEL, D_MODEL)),
        pl.BlockSpec((HALF, tq), lambda j: (0, proj_tile(j) % n_t)),
        pl.BlockSpec((HALF, tq), lambda j: (0, proj_tile(j) % n_t)),
        pl.BlockSpec((tq, V7X_LANES), lambda j: (proj_tile(j) % n_t, 0)),
        pl.BlockSpec((tq, V7X_LANES), lambda j: (proj_tile(j) % n_t, 0)),
        _const_spec((N_META, D_KV)), _const_spec((D_KV, N_META)),
        _const_spec((N_KV_HEADS, 1, GROUP * BLOCK)),
    ]
    return pl.pallas_call(
        kern,
        grid=(n_tiles + 2,),
        in_specs=in_specs,
        out_specs=(pl.BlockSpec((1, tq, D_MODEL), lambda j: (out_tile(j) // n_t, out_tile(j) % n_t, 0)),
                   pl.BlockSpec((1, tq, D_MODEL), lambda j: (proj_tile(j) // n_t, proj_tile(j) % n_t, 0)),
                   pl.BlockSpec((1, tq, D_MODEL), lambda j: (proj_tile(j) // n_t, proj_tile(j) % n_t, 0))),
        out_shape=(jax.ShapeDtypeStruct((b, t_len, D_MODEL), BF16),
                   jax.ShapeDtypeStruct((b, t_len, D_MODEL), BF16),
                   jax.ShapeDtypeStruct((b, t_len, D_MODEL), F32)),
        scratch_shapes=[pltpu.VMEM((D_MODEL, tq), F32),
                        pltpu.VMEM((tq, D_KV), F32),
                        pltpu.VMEM((D_KV, tq), F32),
                        pltpu.VMEM((tq, D_MODEL), F32),
                        pltpu.VMEM((BLOCK, D_KV), F32),
                        pltpu.VMEM((D_KV, BLOCK), F32),
                        pltpu.VMEM((D_MODEL, tq), F32),
                        pltpu.VMEM((tq, D_MODEL), F32)],
        compiler_params=pltpu.CompilerParams(dimension_semantics=("arbitrary",),
                                             vmem_limit_bytes=V7X_VMEM_LIMIT),
        name="attn_branch",
    )(x, p["ln_g"], p["ln_b"], *([p["w_in"]] * (2 * N_KV_HEADS)), p["bq"], p["bga"],
      p["w_in"], p["bk"], p["wvt"], p["bvt"], p["w_attn_out"], tabs["cos_t"], tabs["sin_t"], tabs["cos_r"], tabs["sin_r"],
      kmeta, vmetat, p["sink_rows"])


def _merge_kernel(h32_ref, ya_ref, yb_ref, *refs):
    wga_refs, wgb_refs = refs[:MERGE_BLOCKS], refs[MERGE_BLOCKS:2 * MERGE_BLOCKS]
    bmg_ref, wo_ref, bo_ref, g2_ref, b2_ref, out_ref = refs[2 * MERGE_BLOCKS:]
    h = h32_ref[0].astype(BF16)
    mixed = []
    for c in range(MERGE_BLOCKS):
        cs = slice(c * MERGE_COLS, (c + 1) * MERGE_COLS)
        gs = slice(D_MODEL + c * MERGE_COLS, D_MODEL + (c + 1) * MERGE_COLS)
        ga = _sigmoid(_dot(h, wga_refs[c][...]) + bmg_ref[:, cs])
        gb = _sigmoid(_dot(h, wgb_refs[c][...]) + bmg_ref[:, gs])
        mixed.append(ga * ya_ref[0, :, cs].astype(F32) + gb * yb_ref[0, :, cs].astype(F32))
    mixed = jnp.concatenate(mixed, axis=1)
    out = _dot(mixed.astype(BF16), wo_ref[...]) + bo_ref[...]
    out_ref[0] = _layer_norm(DEEPNORM_ALPHA * h32_ref[0] + out, g2_ref[...], b2_ref[...])


def _merge(h32, ya, yb, p, *, tm):
    b, t_len, _ = h32.shape
    tile = pl.BlockSpec((1, tm, D_MODEL), lambda bi, i: (bi, i, 0))
    w_in_block = lambda c: pl.BlockSpec((D_MODEL, MERGE_COLS), lambda bi, i: (0, c), pipeline_mode=pl.Buffered(1))
    in_specs = [tile, tile, tile,
                *[w_in_block(OFF_G // MERGE_COLS + c) for c in range(MERGE_BLOCKS)],
                *[w_in_block((OFF_G + D_MODEL) // MERGE_COLS + c) for c in range(MERGE_BLOCKS)],
                _const_spec((1, 2 * D_MODEL)),
                _const_spec((D_MODEL, D_MODEL)), _const_spec((1, D_MODEL)),
                _const_spec((1, D_MODEL)), _const_spec((1, D_MODEL))]
    return pl.pallas_call(
        _merge_kernel,
        grid=(b, t_len // tm),
        in_specs=in_specs,
        out_specs=tile,
        out_shape=jax.ShapeDtypeStruct((b, t_len, D_MODEL), F32),
        compiler_params=pltpu.CompilerParams(dimension_semantics=("arbitrary", "arbitrary"),
                                             vmem_limit_bytes=V7X_VMEM_LIMIT),
        name="merge",
    )(h32, ya, yb, *([p["w_in"]] * (2 * MERGE_BLOCKS)), p["bmg"], p["w_o"], p["b_o"],
      p["ln2_g"], p["ln2_b"])


def _rope_tables(t_total):
    inv = (ROPE_THETA ** (-np.arange(HALF, dtype=np.float32) / HALF)).astype(np.float32)
    ang = np.arange(t_total, dtype=np.float32)[:, None] * inv[None, :]
    cos, sin = np.cos(ang), np.sin(ang)
    cos_r = np.concatenate([cos, cos, cos, cos], axis=1)
    sin_r = np.concatenate([-sin, sin, -sin, sin], axis=1)
    return cos, sin, cos_r, sin_r


def kernel(x, meta_tokens, ln_emb_g, ln_emb_b, w_in, b_in, conv_w, conv_b, w_ra, b_ra, w_ri, b_ri,
           lru_lambda, sinks, w_rnn_out, w_attn_out, w_o, b_o, ln_g, ln_b):
    b, seq, _ = x.shape
    assert b == V7X_SUBLANES and w_in.shape[0] == DEPTH
    nb, rb = N_RNN_BLOCKS, RNN_BLOCK
    w = w_in[0].astype(BF16)
    bi = b_in[0]
    row = lambda v: v.reshape(1, -1)
    blocks = lambda v: v.reshape(nb, 1, rb)

    common = {"ln_g": row(ln_emb_g), "ln_b": row(ln_emb_b)}
    p_rnn = dict(common, w_in=w,
                 bx=blocks(bi[:OFF_GR]), bg=blocks(bi[OFF_GR:OFF_Q]),
                 cw=conv_w[0].reshape(CONV_WIDTH, nb, rb).transpose(1, 0, 2), cb=blocks(conv_b[0]),
                 wra=w_ra[0].astype(BF16), bra=blocks(b_ra[0]),
                 wri=w_ri[0].astype(BF16), bri=blocks(b_ri[0]),
                 lam=blocks(lru_lambda[0]), w_rnn_out=w_rnn_out[0].astype(BF16))
    p_attn = dict(common,
                  w_in=w, bq=row(bi[OFF_Q:OFF_K]), bga=row(bi[OFF_GA:OFF_G]),
                  wk=w[:, OFF_K:OFF_V], bk=row(bi[OFF_K:OFF_V]),
                  wvt=w[:, OFF_V:OFF_GA].T,
                  bvt=jnp.broadcast_to(bi[OFF_V:OFF_GA, None], (D_KV, V7X_LANES)),
                  wv=w[:, OFF_V:OFF_GA], bv=row(bi[OFF_V:OFF_GA]),
                  w_attn_out=w_attn_out[0].astype(BF16),
                  sink_rows=jnp.repeat(sinks[0].astype(F32), BLOCK).reshape(N_KV_HEADS, 1, GROUP * BLOCK))
    p_merge = dict(common, w_in=w, bmg=row(bi[OFF_G:]),
                   w_o=w_o[0].astype(BF16), b_o=row(b_o[0]), ln2_g=row(ln_g[0]), ln2_b=row(ln_b[0]))

    cos, sin, cos_r, sin_r = _rope_tables(N_META + seq)
    tabs = {"cos_t": np.ascontiguousarray(cos[N_META:].T), "sin_t": np.ascontiguousarray(sin[N_META:].T),
            "cos_r": cos_r[N_META:], "sin_r": sin_r[N_META:]}

    meta = meta_tokens.astype(x.dtype)
    meta_b = jnp.broadcast_to(meta[None], (V7X_SUBLANES, N_META, D_MODEL))
    zero_hist = jnp.zeros((nb, HIST_ROWS, rb), F32)
    zero_state = jnp.zeros((nb, V7X_SUBLANES, rb), F32)
    _, hist0, state0 = _rnn_branch(meta_b, p_rnn, zero_hist, zero_state, tt=N_META, seq_start=True)
    kmeta, vmeta = _meta_kv(meta, p_attn, cos_r[:N_META], sin_r[:N_META])

    yb, h, h32 = _attn_branch(x, p_attn, tabs, kmeta, vmeta.T, tq=ATTN_TILE)
    ya, _, _ = _rnn_branch(h, p_rnn, hist0, state0, tt=RNN_TILE_STEPS, seq_start=False)
    return _merge(h32, ya, yb, p_merge, tm=MERGE_TILE)
```

```python
import functools

import jax
import jax.numpy as jnp
import numpy as np
from jax import lax
from jax.experimental import pallas as pl
from jax.experimental.pallas import tpu as pltpu

D_MODEL = 2048
N_META = 16
N_RNN_BLOCKS = 8
RNN_BLOCK = 256
CONV_WIDTH = 4
LRU_C = 8.0
HEAD_DIM = 64
HALF = HEAD_DIM // 2
N_Q_HEADS = 32
N_KV_HEADS = 4
GROUP = 8
D_KV = N_KV_HEADS * HEAD_DIM
BLOCK = 128
ROPE_THETA = 10000.0
NEG_INF = -1e30
LN_EPS = 1e-5
DEPTH = 1
DEEPNORM_ALPHA = (2.0 * DEPTH) ** 0.25
OFF_XR = 0
OFF_GR = D_MODEL
OFF_Q = 2 * D_MODEL
OFF_K = OFF_Q + D_MODEL
OFF_V = OFF_K + D_KV
OFF_GA = OFF_V + D_KV
OFF_G = OFF_GA + D_MODEL

V7X_SUBLANES = 8
V7X_LANES = 128
V7X_VMEM_LIMIT = 56 * 1024 * 1024

ATTN_TILE = 2 * BLOCK
RNN_TILE_STEPS = 64
MERGE_TILE = 256

HIST_ROWS = (CONV_WIDTH - 1) * V7X_SUBLANES
RNN_PERM_STEPS = 32
MERGE_COLS = 512
MERGE_BLOCKS = D_MODEL // MERGE_COLS

BF16 = jnp.bfloat16
F32 = jnp.float32


def _layer_norm(x, g, b):
    mu = jnp.mean(x, axis=-1, keepdims=True)
    xc = x - mu
    var = jnp.mean(xc * xc, axis=-1, keepdims=True)
    return xc * lax.rsqrt(var + LN_EPS) * g + b


def _sigmoid(x):
    return 1.0 / (1.0 + jnp.exp(-x))


def _dot(a, b):
    return jnp.dot(a, b, preferred_element_type=F32)


def _dot_nt(a, b):
    return lax.dot_general(a, b, (((1,), (1,)), ((), ())), preferred_element_type=F32)


def _const_spec(shape):
    nd = len(shape)
    return pl.BlockSpec(shape, lambda *_: (0,) * nd, pipeline_mode=pl.Buffered(1))


def _rnn_kernel(x_ref, lng_ref, lnb_ref, *refs, tt, seq_start, normalised_input):
    wx_refs, wg_refs = refs[:N_RNN_BLOCKS], refs[N_RNN_BLOCKS:2 * N_RNN_BLOCKS]
    (bx_ref, bg_ref, cw_ref, cb_ref, wra_ref, bra_ref, wri_ref, bri_ref, lam_ref, wout_ref, perm_ref,
     permt_ref, hist0_ref, state0_ref, ya_ref, hist_out_ref, state_out_ref,
     y_s, hist_s, state_s) = refs[2 * N_RNN_BLOCKS:]
    rows = tt * V7X_SUBLANES
    nb = N_RNN_BLOCKS
    j = pl.program_id(0)

    @pl.when(j == 0)
    def _():
        y_s[...] = jnp.zeros_like(y_s)
        hist_s[...] = hist0_ref[...]
        state_s[...] = state0_ref[...]

    ts = perm_ref.shape[0] // V7X_SUBLANES
    sub = ts * V7X_SUBLANES

    def project_out(y_prev, n):
        cs = slice(n * RNN_BLOCK, (n + 1) * RNN_BLOCK)
        ya = _dot(y_prev, wout_ref[:, cs])
        for k, s0 in enumerate(range(0, tt, ts)):
            ya_ref[:, s0:s0 + ts, cs] = (ya[k * sub:(k + 1) * sub]
                                         .reshape(V7X_SUBLANES, ts, RNN_BLOCK).astype(ya_ref.dtype))

    def scan_step():
        parts = []
        for s0 in range(0, tt, ts):
            hs_ = x_ref[:, s0:s0 + ts, :].reshape(ts * V7X_SUBLANES, D_MODEL)
            if not normalised_input:
                hs_ = _layer_norm(hs_, lng_ref[...], lnb_ref[...]).astype(BF16)
            parts.append(_dot(perm_ref[...], hs_).astype(BF16))
        h = jnp.concatenate(parts, axis=0)

        def project(n):
            return _dot(h, wx_refs[n][...]) + bx_ref[n], _dot(h, wg_refs[n][...]) + bg_ref[n]

        y_prev = y_s[...].astype(BF16)
        ys = []
        nxt = project(0)
        for n in range(nb):
            xr, gr = nxt
            if n + 1 < nb:
                nxt = project(n + 1)
            project_out(y_prev, n)
            xe = jnp.concatenate([hist_s[n], xr], axis=0)
            cw = cw_ref[n]
            conv = cb_ref[n] + cw[0:1] * xr
            for k in range(1, CONV_WIDTH):
                off = HIST_ROWS - k * V7X_SUBLANES
                conv = conv + cw[k:k + 1] * xe[off:off + rows]
            hist_s[n] = xe[rows:rows + HIST_ROWS]
            cb16 = conv.astype(BF16)
            gate_r = _sigmoid(_dot(cb16, wra_ref[n]) + bra_ref[n])
            gate_i = _sigmoid(_dot(cb16, wri_ref[n]) + bri_ref[n])
            lam = lam_ref[n]
            log_sig = jnp.minimum(lam, 0.0) - jnp.log1p(jnp.exp(-jnp.abs(lam)))
            a = jnp.exp((LRU_C * gate_r) * log_sig)
            mult = jnp.sqrt(1.0 - a * a)
            if seq_start:
                first = lax.broadcasted_iota(jnp.int32, (rows, RNN_BLOCK), 0) < V7X_SUBLANES
                mult = jnp.where(first & (j == 0), 1.0, mult)
            u = mult * gate_i * conv
            hcur = state_s[n]
            hs = []
            for t in range(tt):
                sl = slice(t * V7X_SUBLANES, (t + 1) * V7X_SUBLANES)
                hcur = a[sl] * hcur + u[sl]
                hs.append(hcur)
            state_s[n] = hcur
            hr = jnp.concatenate(hs, axis=0)
            ys.append((hr * (gr * _sigmoid(gr))).astype(BF16))
        y = jnp.concatenate(ys, axis=1)
        for r0 in range(0, rows, sub):
            y_s[r0:r0 + sub, :] = _dot(permt_ref[...], y[r0:r0 + sub])

    def drain_step():
        y_prev = y_s[...].astype(BF16)
        for n in range(nb):
            project_out(y_prev, n)

    last = pl.num_programs(0) - 1
    pl.when(j < last)(scan_step)
    pl.when(j == last)(drain_step)

    @pl.when(j == pl.num_programs(0) - 2)
    def _():
        hist_out_ref[...] = hist_s[...]
        state_out_ref[...] = state_s[...]


def _rnn_branch(x, p, hist0, state0, *, tt, seq_start):
    t_len = x.shape[1]
    nb, rb = N_RNN_BLOCKS, RNN_BLOCK
    rows = tt * V7X_SUBLANES
    ts = min(tt, RNN_PERM_STEPS)
    prow = ts * V7X_SUBLANES
    r = np.arange(prow)
    perm = np.zeros((prow, prow), np.float32)
    perm[r, (r % V7X_SUBLANES) * ts + r // V7X_SUBLANES] = 1.0
    perm, perm_t = jnp.asarray(perm, BF16), jnp.asarray(perm.T, BF16)
    assert x.dtype in (F32, BF16)
    kern = functools.partial(_rnn_kernel, tt=tt, seq_start=seq_start, normalised_input=x.dtype == BF16)
    n_tiles = t_len // tt
    w_in_block = lambda c: pl.BlockSpec((D_MODEL, rb), lambda j: (0, c), pipeline_mode=pl.Buffered(1))
    in_specs = [
        pl.BlockSpec((V7X_SUBLANES, tt, D_MODEL), lambda j: (0, jnp.minimum(j, n_tiles - 1), 0)),
        _const_spec((1, D_MODEL)), _const_spec((1, D_MODEL)),
        *[w_in_block(OFF_XR // rb + n) for n in range(nb)],
        *[w_in_block(OFF_GR // rb + n) for n in range(nb)],
        _const_spec((nb, 1, rb)), _const_spec((nb, 1, rb)),
        _const_spec((nb, CONV_WIDTH, rb)), _const_spec((nb, 1, rb)),
        _const_spec((nb, rb, rb)), _const_spec((nb, 1, rb)),
        _const_spec((nb, rb, rb)), _const_spec((nb, 1, rb)),
        _const_spec((nb, 1, rb)),
        _const_spec((D_MODEL, D_MODEL)),
        _const_spec((prow, prow)), _const_spec((prow, prow)),
        _const_spec((nb, HIST_ROWS, rb)), _const_spec((nb, V7X_SUBLANES, rb)),
    ]
    out_shape = (
        jax.ShapeDtypeStruct((V7X_SUBLANES, t_len, D_MODEL), BF16),
        jax.ShapeDtypeStruct((nb, HIST_ROWS, rb), F32),
        jax.ShapeDtypeStruct((nb, V7X_SUBLANES, rb), F32),
    )
    out_specs = (
        pl.BlockSpec((V7X_SUBLANES, tt, D_MODEL), lambda j: (0, jnp.maximum(j - 1, 0), 0)),
        pl.BlockSpec((nb, HIST_ROWS, rb), lambda j: (0, 0, 0)),
        pl.BlockSpec((nb, V7X_SUBLANES, rb), lambda j: (0, 0, 0)),
    )
    return pl.pallas_call(
        kern,
        grid=(n_tiles + 1,),
        in_specs=in_specs,
        out_specs=out_specs,
        out_shape=out_shape,
        scratch_shapes=[pltpu.VMEM((rows, D_MODEL), F32),
                        pltpu.VMEM((nb, HIST_ROWS, rb), F32), pltpu.VMEM((nb, V7X_SUBLANES, rb), F32)],
        compiler_params=pltpu.CompilerParams(dimension_semantics=("arbitrary",),
                                             vmem_limit_bytes=V7X_VMEM_LIMIT),
        name="rnn_branch_start" if seq_start else "rnn_branch",
    )(x, p["ln_g"], p["ln_b"], *([p["w_in"]] * (2 * nb)), p["bx"], p["bg"], p["cw"], p["cb"],
      p["wra"], p["bra"], p["wri"], p["bri"], p["lam"], p["w_rnn_out"], perm, perm_t, hist0, state0)


def _rope_rows(k, cos, sin_signed):
    lane = lax.broadcasted_iota(jnp.int32, k.shape, 1)
    partner = jnp.where((lane % HEAD_DIM) < HALF,
                        pltpu.roll(k, V7X_LANES - HALF, axis=1),
                        pltpu.roll(k, HALF, axis=1))
    return k * cos + partner * sin_signed


def _meta_kv_kernel(m_ref, lng_ref, lnb_ref, wk_ref, bk_ref, wv_ref, bv_ref, cos_ref, sin_ref,
                    k_ref, v_ref):
    h = _layer_norm(m_ref[...], lng_ref[...], lnb_ref[...]).astype(BF16)
    k = _dot(h, wk_ref[...]) + bk_ref[...]
    v = _dot(h, wv_ref[...]) + bv_ref[...]
    for c in range(D_KV // V7X_LANES):
        sl = slice(c * V7X_LANES, (c + 1) * V7X_LANES)
        k_ref[:, sl] = _rope_rows(k[:, sl], cos_ref[...], sin_ref[...]).astype(BF16)
    v_ref[...] = v.astype(BF16)


def _meta_kv(meta, p, cos_rows, sin_rows):
    return pl.pallas_call(
        _meta_kv_kernel,
        out_shape=(jax.ShapeDtypeStruct((N_META, D_KV), BF16),
                   jax.ShapeDtypeStruct((N_META, D_KV), BF16)),
        name="meta_kv",
    )(meta, p["ln_g"], p["ln_b"], p["wk"], p["bk"], p["wv"], p["bv"], cos_rows, sin_rows)


def _attn_kernel(x_ref, lng_ref, lnb_ref, *refs, tq, n_t):
    wq_refs, wga_refs = refs[:N_KV_HEADS], refs[N_KV_HEADS:2 * N_KV_HEADS]
    (bq_ref, bga_ref, wk_ref, bk_ref, wvt_ref, bvt_ref, wout_ref, cost_ref, sint_ref, cosr_ref, sinr_ref,
     kmeta_ref, vmetat_ref, sink_ref, yb_ref, h_ref, h32_ref,
     qt_s, k_s, vt_s, gat_s, kprev_s, vprev_s, ot_s, act_s) = refs[2 * N_KV_HEADS:]
    nblk = tq // BLOCK
    rows_g = GROUP * HEAD_DIM
    j = pl.program_id(0)

    @pl.when(j == 0)
    def _():
        for ref in (qt_s, k_s, vt_s, gat_s, kprev_s, vprev_s, act_s):
            ref[...] = jnp.zeros_like(ref)

    h32 = _layer_norm(x_ref[0], lng_ref[...], lnb_ref[...])
    h = h32.astype(BF16)
    h32_ref[0] = h32
    h_ref[0] = h

    act_prev = act_s[...].astype(BF16)
    out_cols = D_MODEL // (N_KV_HEADS * nblk)

    def project_out(c):
        cs = slice(c * out_cols, (c + 1) * out_cols)
        yb_ref[0, :, cs] = _dot(act_prev, wout_ref[:, cs]).astype(yb_ref.dtype)

    scale = HEAD_DIM ** -0.5
    cos = cost_ref[...] * scale
    sin = sint_ref[...] * scale

    def project_q(g):
        rs = slice(g * rows_g, (g + 1) * rows_g)
        qt = (_dot(h, wq_refs[g][...]) + bq_ref[:, rs]).T
        out = []
        for hh in range(GROUP):
            r0 = hh * HEAD_DIM
            q1, q2 = qt[r0:r0 + HALF], qt[r0 + HALF:r0 + HEAD_DIM]
            out.append((q1 * cos - q2 * sin, q2 * cos + q1 * sin))
        return out

    def store_q(g, roped):
        for hh, (lo, hi) in enumerate(roped):
            r0 = g * rows_g + hh * HEAD_DIM
            qt_s[r0:r0 + HALF, :] = lo
            qt_s[r0 + HALF:r0 + HEAD_DIM, :] = hi

    def project_gate(g):
        rs = slice(g * rows_g, (g + 1) * rows_g)
        gat = _dot(h, wga_refs[g][...]) + bga_ref[:, rs]
        return gat * _sigmoid(gat)

    i = jnp.maximum(j - 1, 0) % n_t
    k = k_s[...].astype(BF16)
    vt = vt_s[...].astype(BF16)
    k_carry, vt_carry = kprev_s[...].astype(BF16), vprev_s[...].astype(BF16)
    kmeta = kmeta_ref[...]
    vmetat = vmetat_ref[...]

    key_row = lax.broadcasted_iota(jnp.int32, (BLOCK, GROUP * BLOCK), 0)
    qry_col = lax.broadcasted_iota(jnp.int32, (BLOCK, GROUP * BLOCK), 1) % BLOCK
    cur_ok = key_row <= qry_col
    first_prev_ok = (key_row > qry_col) & (i > 0)

    def scores(g, jb):
        tok = slice(jb * BLOCK, (jb + 1) * BLOCK)
        ks = slice(g * HEAD_DIM, (g + 1) * HEAD_DIM)
        k_prev = k_carry if jb == 0 else k[(jb - 1) * BLOCK:jb * BLOCK]
        q_g = jnp.concatenate(
            [qt_s[(g * GROUP + hh) * HEAD_DIM:(g * GROUP + hh + 1) * HEAD_DIM, tok].astype(BF16)
             for hh in range(GROUP)], axis=1)
        s_prev = _dot(k_prev[:, ks], q_g)
        if jb == 0:
            s_prev = jnp.where(first_prev_ok, s_prev, NEG_INF)
        s_cm = _dot(jnp.concatenate([k[tok, ks], kmeta[:, ks]], axis=0), q_g)
        s_band = jnp.where(cur_ok, s_cm[:BLOCK], s_prev)
        return s_band, s_cm[BLOCK:]

    def finish(g, jb, s):
        s_band, s_meta = s
        tok = slice(jb * BLOCK, (jb + 1) * BLOCK)
        vs = slice(g * HEAD_DIM, (g + 1) * HEAD_DIM)
        vt_prev = vt_carry if jb == 0 else vt[:, (jb - 1) * BLOCK:jb * BLOCK]
        sink = sink_ref[g]
        m = jnp.maximum(jnp.maximum(jnp.max(s_band, axis=0, keepdims=True),
                                    jnp.max(s_meta, axis=0, keepdims=True)), sink)
        p_band = jnp.exp(s_band - m)
        p_meta = jnp.exp(s_meta - m)
        denom = (jnp.sum(p_band, axis=0, keepdims=True) + jnp.sum(p_meta, axis=0, keepdims=True)
                 + jnp.exp(sink - m))
        p_cur = jnp.where(cur_ok, p_band, 0.0)
        p_prev = jnp.where(cur_ok, 0.0, p_band)
        p_cm = jnp.concatenate([p_cur, p_meta], axis=0).astype(BF16)
        v_cm = jnp.concatenate([vt[vs, tok], vmetat[vs]], axis=1)
        o = _dot(vt_prev[vs], p_prev.astype(BF16)) + _dot(v_cm, p_cm)
        o = o * (1.0 / denom)
        for hh in range(GROUP):
            r0 = (g * GROUP + hh) * HEAD_DIM
            ot_s[r0:r0 + HEAD_DIM, tok] = o[:, hh * BLOCK:(hh + 1) * BLOCK]

    gates = [None] * N_KV_HEADS
    held_q = None
    for g in range(N_KV_HEADS):
        pg = (g - 1) % N_KV_HEADS
        for jb in range(nblk):
            s = scores(g, jb)
            if jb == 0:
                roped = project_q(pg)
                if pg < g:
                    store_q(pg, roped)
                else:
                    held_q = roped
            elif jb == 1:
                gates[pg] = project_gate(pg)
            project_out(g * nblk + jb)
            finish(g, jb, s)
    store_q(N_KV_HEADS - 1, held_q)

    k_new = _dot(h, wk_ref[...]) + bk_ref[...]
    vt_new = _dot_nt(wvt_ref[...], h) + jnp.concatenate([bvt_ref[...]] * (tq // V7X_LANES), axis=1)

    act_s[...] = ot_s[...].T * gat_s[...]

    kprev_s[...] = k_s[tq - BLOCK:, :]
    vprev_s[...] = vt_s[:, tq - BLOCK:]
    for c in range(D_KV // V7X_LANES):
        sl = slice(c * V7X_LANES, (c + 1) * V7X_LANES)
        k_s[:, sl] = _rope_rows(k_new[:, sl], cosr_ref[...], sinr_ref[...])
    for g in range(N_KV_HEADS):
        gat_s[:, g * rows_g:(g + 1) * rows_g] = gates[g]
    vt_s[...] = vt_new


def _attn_branch(x, p, tabs, kmeta, vmetat, *, tq):
    b, t_len, _ = x.shape
    assert tq == 2 * BLOCK
    n_t = t_len // tq
    n_tiles = b * n_t
    kern = functools.partial(_attn_kernel, tq=tq, n_t=n_t)
    proj_tile = lambda j: jnp.minimum(j, n_tiles - 1)
    out_tile = lambda j: jnp.maximum(j - 2, 0)
    cols_g = GROUP * HEAD_DIM
    w_in_block = lambda c: pl.BlockSpec((D_MODEL, cols_g), lambda j: (0, c), pipeline_mode=pl.Buffered(1))
    in_specs = [
        pl.BlockSpec((1, tq, D_MODEL), lambda j: (proj_tile(j) // n_t, proj_tile(j) % n_t, 0)),
        _const_spec((1, D_MODEL)), _const_spec((1, D_MODEL)),
        *[w_in_block(OFF_Q // cols_g + g) for g in range(N_KV_HEADS)],
        *[w_in_block(OFF_GA // cols_g + g) for g in range(N_KV_HEADS)],
        _const_spec((1, D_MODEL)), _const_spec((1, D_MODEL)),
        pl.BlockSpec((D_MODEL, D_KV), lambda j: (0, OFF_K // D_KV), pipeline_mode=pl.Buffered(1)),
        _const_spec((1, D_KV)),
        _const_spec((D_KV, D_MODEL)), _const_spec((D_KV, V7X_LANES)),
        _const_spec((D_MODEL, D_MODEL)),
        pl.BlockSpec((HALF, tq), lambda j: (0, proj_tile(j) % n_t)),
        pl.BlockSpec((HALF, tq), lambda j: (0, proj_tile(j) % n_t)),
        pl.BlockSpec((tq, V7X_LANES), lambda j: (proj_tile(j) % n_t, 0)),
        pl.BlockSpec((tq, V7X_LANES), lambda j: (proj_tile(j) % n_t, 0)),
        _const_spec((N_META, D_KV)), _const_spec((D_KV, N_META)),
        _const_spec((N_KV_HEADS, 1, GROUP * BLOCK)),
    ]
    return pl.pallas_call(
        kern,
        grid=(n_tiles + 2,),
        in_specs=in_specs,
        out_specs=(pl.BlockSpec((1, tq, D_MODEL), lambda j: (out_tile(j) // n_t, out_tile(j) % n_t, 0)),
                   pl.BlockSpec((1, tq, D_MODEL), lambda j: (proj_tile(j) // n_t, proj_tile(j) % n_t, 0)),
                   pl.BlockSpec((1, tq, D_MODEL), lambda j: (proj_tile(j) // n_t, proj_tile(j) % n_t, 0))),
        out_shape=(jax.ShapeDtypeStruct((b, t_len, D_MODEL), BF16),
                   jax.ShapeDtypeStruct((b, t_len, D_MODEL), BF16),
                   jax.ShapeDtypeStruct((b, t_len, D_MODEL), F32)),
        scratch_shapes=[pltpu.VMEM((D_MODEL, tq), F32),
                        pltpu.VMEM((tq, D_KV), F32),
                        pltpu.VMEM((D_KV, tq), F32),
                        pltpu.VMEM((tq, D_MODEL), F32),
                        pltpu.VMEM((BLOCK, D_KV), F32),
                        pltpu.VMEM((D_KV, BLOCK), F32),
                        pltpu.VMEM((D_MODEL, tq), F32),
                        pltpu.VMEM((tq, D_MODEL), F32)],
        compiler_params=pltpu.CompilerParams(dimension_semantics=("arbitrary",),
                                             vmem_limit_bytes=V7X_VMEM_LIMIT),
        name="attn_branch",
    )(x, p["ln_g"], p["ln_b"], *([p["w_in"]] * (2 * N_KV_HEADS)), p["bq"], p["bga"],
      p["w_in"], p["bk"], p["wvt"], p["bvt"], p["w_attn_out"], tabs["cos_t"], tabs["sin_t"], tabs["cos_r"], tabs["sin_r"],
      kmeta, vmetat, p["sink_rows"])


def _merge_kernel(h32_ref, ya_ref, yb_ref, *refs):
    wga_refs, wgb_refs = refs[:MERGE_BLOCKS], refs[MERGE_BLOCKS:2 * MERGE_BLOCKS]
    bmg_ref, wo_ref, bo_ref, g2_ref, b2_ref, out_ref = refs[2 * MERGE_BLOCKS:]
    h = h32_ref[0].astype(BF16)
    mixed = []
    for c in range(MERGE_BLOCKS):
        cs = slice(c * MERGE_COLS, (c + 1) * MERGE_COLS)
        gs = slice(D_MODEL + c * MERGE_COLS, D_MODEL + (c + 1) * MERGE_COLS)
        ga = _sigmoid(_dot(h, wga_refs[c][...]) + bmg_ref[:, cs])
        gb = _sigmoid(_dot(h, wgb_refs[c][...]) + bmg_ref[:, gs])
        mixed.append(ga * ya_ref[0, :, cs].astype(F32) + gb * yb_ref[0, :, cs].astype(F32))
    mixed = jnp.concatenate(mixed, axis=1)
    out = _dot(mixed.astype(BF16), wo_ref[...]) + bo_ref[...]
    out_ref[0] = _layer_norm(DEEPNORM_ALPHA * h32_ref[0] + out, g2_ref[...], b2_ref[...])


def _merge(h32, ya, yb, p, *, tm):
    b, t_len, _ = h32.shape
    tile = pl.BlockSpec((1, tm, D_MODEL), lambda bi, i: (bi, i, 0))
    w_in_block = lambda c: pl.BlockSpec((D_MODEL, MERGE_COLS), lambda bi, i: (0, c), pipeline_mode=pl.Buffered(1))
    in_specs = [tile, tile, tile,
                *[w_in_block(OFF_G // MERGE_COLS + c) for c in range(MERGE_BLOCKS)],
                *[w_in_block((OFF_G + D_MODEL) // MERGE_COLS + c) for c in range(MERGE_BLOCKS)],
                _const_spec((1, 2 * D_MODEL)),
                _const_spec((D_MODEL, D_MODEL)), _const_spec((1, D_MODEL)),
                _const_spec((1, D_MODEL)), _const_spec((1, D_MODEL))]
    return pl.pallas_call(
        _merge_kernel,
        grid=(b, t_len // tm),
        in_specs=in_specs,
        out_specs=tile,
        out_shape=jax.ShapeDtypeStruct((b, t_len, D_MODEL), F32),
        compiler_params=pltpu.CompilerParams(dimension_semantics=("arbitrary", "arbitrary"),
                                             vmem_limit_bytes=V7X_VMEM_LIMIT),
        name="merge",
    )(h32, ya, yb, *([p["w_in"]] * (2 * MERGE_BLOCKS)), p["bmg"], p["w_o"], p["b_o"],
      p["ln2_g"], p["ln2_b"])


def _rope_tables(t_total):
    inv = (ROPE_THETA ** (-np.arange(HALF, dtype=np.float32) / HALF)).astype(np.float32)
    ang = np.arange(t_total, dtype=np.float32)[:, None] * inv[None, :]
    cos, sin = np.cos(ang), np.sin(ang)
    cos_r = np.concatenate([cos, cos, cos, cos], axis=1)
    sin_r = np.concatenate([-sin, sin, -sin, sin], axis=1)
    return cos, sin, cos_r, sin_r


def kernel(x, meta_tokens, ln_emb_g, ln_emb_b, w_in, b_in, conv_w, conv_b, w_ra, b_ra, w_ri, b_ri,
           lru_lambda, sinks, w_rnn_out, w_attn_out, w_o, b_o, ln_g, ln_b):
    b, seq, _ = x.shape
    assert b == V7X_SUBLANES and w_in.shape[0] == DEPTH
    nb, rb = N_RNN_BLOCKS, RNN_BLOCK
    w = w_in[0].astype(BF16)
    bi = b_in[0]
    row = lambda v: v.reshape(1, -1)
    blocks = lambda v: v.reshape(nb, 1, rb)

    common = {"ln_g": row(ln_emb_g), "ln_b": row(ln_emb_b)}
    p_rnn = dict(common, w_in=w,
                 bx=blocks(bi[:OFF_GR]), bg=blocks(bi[OFF_GR:OFF_Q]),
                 cw=conv_w[0].reshape(CONV_WIDTH, nb, rb).transpose(1, 0, 2), cb=blocks(conv_b[0]),
                 wra=w_ra[0].astype(BF16), bra=blocks(b_ra[0]),
                 wri=w_ri[0].astype(BF16), bri=blocks(b_ri[0]),
                 lam=blocks(lru_lambda[0]), w_rnn_out=w_rnn_out[0].astype(BF16))
    p_attn = dict(common,
                  w_in=w, bq=row(bi[OFF_Q:OFF_K]), bga=row(bi[OFF_GA:OFF_G]),
                  wk=w[:, OFF_K:OFF_V], bk=row(bi[OFF_K:OFF_V]),
                  wvt=w[:, OFF_V:OFF_GA].T,
                  bvt=jnp.broadcast_to(bi[OFF_V:OFF_GA, None], (D_KV, V7X_LANES)),
                  wv=w[:, OFF_V:OFF_GA], bv=row(bi[OFF_V:OFF_GA]),
                  w_attn_out=w_attn_out[0].astype(BF16),
                  sink_rows=jnp.repeat(sinks[0].astype(F32), BLOCK).reshape(N_KV_HEADS, 1, GROUP * BLOCK))
    p_merge = dict(common, w_in=w, bmg=row(bi[OFF_G:]),
                   w_o=w_o[0].astype(BF16), b_o=row(b_o[0]), ln2_g=row(ln_g[0]), ln2_b=row(ln_b[0]))

    cos, sin, cos_r, sin_r = _rope_tables(N_META + seq)
    tabs = {"cos_t": np.ascontiguousarray(cos[N_META:].T), "sin_t": np.ascontiguousarray(sin[N_META:].T),
            "cos_r": cos_r[N_META:], "sin_r": sin_r[N_META:]}

    meta = meta_tokens.astype(x.dtype)
    meta_b = jnp.broadcast_to(meta[None], (V7X_SUBLANES, N_META, D_MODEL))
    zero_hist = jnp.zeros((nb, HIST_ROWS, rb), F32)
    zero_state = jnp.zeros((nb, V7X_SUBLANES, rb), F32)
    _, hist0, state0 = _rnn_branch(meta_b, p_rnn, zero_hist, zero_state, tt=N_META, seq_start=True)
    kmeta, vmeta = _meta_kv(meta, p_attn, cos_r[:N_META], sin_r[:N_META])

    yb, h, h32 = _attn_branch(x, p_attn, tabs, kmeta, vmeta.T, tq=ATTN_TILE)
    ya, _, _ = _rnn_branch(h, p_rnn, hist0, state0, tt=RNN_TILE_STEPS, seq_start=False)
    return _merge(h32, ya, yb, p_merge, tm=MERGE_TILE)
```

```python
import functools

import jax
import jax.numpy as jnp
import numpy as np
from jax import lax
from jax.experimental import pallas as pl
from jax.experimental.pallas import tpu as pltpu

D_MODEL = 2048
N_META = 16
N_RNN_BLOCKS = 8
RNN_BLOCK = 256
CONV_WIDTH = 4
LRU_C = 8.0
HEAD_DIM = 64
HALF = HEAD_DIM // 2
N_Q_HEADS = 32
N_KV_HEADS = 4
GROUP = 8
D_KV = N_KV_HEADS * HEAD_DIM
BLOCK = 128
ROPE_THETA = 10000.0
NEG_INF = -1e30
LN_EPS = 1e-5
DEPTH = 1
DEEPNORM_ALPHA = (2.0 * DEPTH) ** 0.25
OFF_XR = 0
OFF_GR = D_MODEL
OFF_Q = 2 * D_MODEL
OFF_K = OFF_Q + D_MODEL
OFF_V = OFF_K + D_KV
OFF_GA = OFF_V + D_KV
OFF_G = OFF_GA + D_MODEL

V7X_SUBLANES = 8
V7X_LANES = 128
V7X_VMEM_LIMIT = 56 * 1024 * 1024

ATTN_TILE = 2 * BLOCK
RNN_TILE_STEPS = 64
MERGE_TILE = 256

HIST_ROWS = (CONV_WIDTH - 1) * V7X_SUBLANES
RNN_PERM_STEPS = 32
MERGE_COLS = 512
MERGE_BLOCKS = D_MODEL // MERGE_COLS

BF16 = jnp.bfloat16
F32 = jnp.float32


def _layer_norm(x, g, b):
    mu = jnp.mean(x, axis=-1, keepdims=True)
    xc = x - mu
    var = jnp.mean(xc * xc, axis=-1, keepdims=True)
    return xc * lax.rsqrt(var + LN_EPS) * g + b


def _sigmoid(x):
    return 1.0 / (1.0 + jnp.exp(-x))


def _dot(a, b):
    return jnp.dot(a, b, preferred_element_type=F32)


def _dot_nt(a, b):
    return lax.dot_general(a, b, (((1,), (1,)), ((), ())), preferred_element_type=F32)


def _const_spec(shape):
    nd = len(shape)
    return pl.BlockSpec(shape, lambda *_: (0,) * nd, pipeline_mode=pl.Buffered(1))


def _rnn_kernel(x_ref, lng_ref, lnb_ref, *refs, tt, seq_start, normalised_input, strided_input):
    wx_refs, wg_refs = refs[:N_RNN_BLOCKS], refs[N_RNN_BLOCKS:2 * N_RNN_BLOCKS]
    (bx_ref, bg_ref, cw_ref, cb_ref, wra_ref, bra_ref, wri_ref, bri_ref, lam_ref, wout_ref, perm_ref,
     permt_ref, hist0_ref, state0_ref, ya_ref, hist_out_ref, state_out_ref,
     y_s, hist_s, state_s) = refs[2 * N_RNN_BLOCKS:2 * N_RNN_BLOCKS + 20]
    rows = tt * V7X_SUBLANES
    nb = N_RNN_BLOCKS
    if strided_input:
        xbuf, xsem = refs[2 * N_RNN_BLOCKS + 20:]

        def tile_copy(tile, slot, bb):
            t0 = pl.multiple_of(tile * tt, tt)
            return pltpu.make_async_copy(x_ref.at[bb, pl.ds(t0, tt), :], xbuf.at[slot, :, bb, :], xsem.at[slot, bb])
    j = pl.program_id(0)

    @pl.when(j == 0)
    def _():
        y_s[...] = jnp.zeros_like(y_s)
        hist_s[...] = hist0_ref[...]
        state_s[...] = state0_ref[...]
        if strided_input:
            for bb in range(V7X_SUBLANES):
                tile_copy(0, 0, bb).start()

    if strided_input:
        @pl.when(j + 1 < pl.num_programs(0) - 1)
        def _():
            for bb in range(V7X_SUBLANES):
                tile_copy(j + 1, (j + 1) % 2, bb).start()

    ts = perm_ref.shape[0] // V7X_SUBLANES
    sub = ts * V7X_SUBLANES

    def project_out(y_prev, n):
        cs = slice(n * RNN_BLOCK, (n + 1) * RNN_BLOCK)
        ya = _dot(y_prev, wout_ref[:, cs])
        for k, s0 in enumerate(range(0, tt, ts)):
            ya_ref[:, s0:s0 + ts, cs] = (ya[k * sub:(k + 1) * sub]
                                         .reshape(V7X_SUBLANES, ts, RNN_BLOCK).astype(ya_ref.dtype))

    def scan_step():
        if strided_input:
            slot = j % 2
            for bb in range(V7X_SUBLANES):
                tile_copy(j, slot, bb).wait()
            h = xbuf[slot].reshape(rows, D_MODEL).astype(BF16)
        else:
            parts = []
            for s0 in range(0, tt, ts):
                hs_ = x_ref[:, s0:s0 + ts, :].reshape(ts * V7X_SUBLANES, D_MODEL)
                if not normalised_input:
                    hs_ = _layer_norm(hs_, lng_ref[...], lnb_ref[...]).astype(BF16)
                parts.append(_dot(perm_ref[...], hs_).astype(BF16))
            h = jnp.concatenate(parts, axis=0)

        def project(n):
            return _dot(h, wx_refs[n][...]) + bx_ref[n], _dot(h, wg_refs[n][...]) + bg_ref[n]

        y_prev = y_s[...].astype(BF16)
        ys = []
        nxt = project(0)
        for n in range(nb):
            xr, gr = nxt
            if n + 1 < nb:
                nxt = project(n + 1)
            project_out(y_prev, n)
            xe = jnp.concatenate([hist_s[n], xr], axis=0)
            cw = cw_ref[n]
            conv = cb_ref[n] + cw[0:1] * xr
            for k in range(1, CONV_WIDTH):
                off = HIST_ROWS - k * V7X_SUBLANES
                conv = conv + cw[k:k + 1] * xe[off:off + rows]
            hist_s[n] = xe[rows:rows + HIST_ROWS]
            cb16 = conv.astype(BF16)
            gate_r = _sigmoid(_dot(cb16, wra_ref[n]) + bra_ref[n])
            gate_i = _sigmoid(_dot(cb16, wri_ref[n]) + bri_ref[n])
            lam = lam_ref[n]
            log_sig = jnp.minimum(lam, 0.0) - jnp.log1p(jnp.exp(-jnp.abs(lam)))
            a = jnp.exp((LRU_C * gate_r) * log_sig)
            mult = jnp.sqrt(1.0 - a * a)
            if seq_start:
                first = lax.broadcasted_iota(jnp.int32, (rows, RNN_BLOCK), 0) < V7X_SUBLANES
                mult = jnp.where(first & (j == 0), 1.0, mult)
            u = mult * gate_i * conv
            hcur = state_s[n]
            hs = []
            for t in range(tt):
                sl = slice(t * V7X_SUBLANES, (t + 1) * V7X_SUBLANES)
                hcur = a[sl] * hcur + u[sl]
                hs.append(hcur)
            state_s[n] = hcur
            hr = jnp.concatenate(hs, axis=0)
            ys.append((hr * (gr * _sigmoid(gr))).astype(BF16))
        y = jnp.concatenate(ys, axis=1)
        for r0 in range(0, rows, sub):
            y_s[r0:r0 + sub, :] = _dot(permt_ref[...], y[r0:r0 + sub])

    def drain_step():
        y_prev = y_s[...].astype(BF16)
        for n in range(nb):
            project_out(y_prev, n)

    last = pl.num_programs(0) - 1
    pl.when(j < last)(scan_step)
    pl.when(j == last)(drain_step)

    @pl.when(j == pl.num_programs(0) - 2)
    def _():
        hist_out_ref[...] = hist_s[...]
        state_out_ref[...] = state_s[...]


def _rnn_branch(x, p, hist0, state0, *, tt, seq_start):
    t_len = x.shape[1]
    nb, rb = N_RNN_BLOCKS, RNN_BLOCK
    rows = tt * V7X_SUBLANES
    ts = min(tt, RNN_PERM_STEPS)
    prow = ts * V7X_SUBLANES
    r = np.arange(prow)
    perm = np.zeros((prow, prow), np.float32)
    perm[r, (r % V7X_SUBLANES) * ts + r // V7X_SUBLANES] = 1.0
    perm, perm_t = jnp.asarray(perm, BF16), jnp.asarray(perm.T, BF16)
    strided = not seq_start
    kern = functools.partial(_rnn_kernel, tt=tt, seq_start=seq_start, normalised_input=strided,
                             strided_input=strided)
    n_tiles = t_len // tt
    w_in_block = lambda c: pl.BlockSpec((D_MODEL, rb), lambda j: (0, c), pipeline_mode=pl.Buffered(1))
    x_spec = (pl.BlockSpec(memory_space=pl.ANY) if strided else
              pl.BlockSpec((V7X_SUBLANES, tt, D_MODEL), lambda j: (0, jnp.minimum(j, n_tiles - 1), 0)))
    extra_scratch = ([pltpu.VMEM((2, tt, V7X_SUBLANES, D_MODEL), F32), pltpu.SemaphoreType.DMA((2, V7X_SUBLANES))]
                     if strided else [])
    in_specs = [
        x_spec,
        _const_spec((1, D_MODEL)), _const_spec((1, D_MODEL)),
        *[w_in_block(OFF_XR // rb + n) for n in range(nb)],
        *[w_in_block(OFF_GR // rb + n) for n in range(nb)],
        _const_spec((nb, 1, rb)), _const_spec((nb, 1, rb)),
        _const_spec((nb, CONV_WIDTH, rb)), _const_spec((nb, 1, rb)),
        _const_spec((nb, rb, rb)), _const_spec((nb, 1, rb)),
        _const_spec((nb, rb, rb)), _const_spec((nb, 1, rb)),
        _const_spec((nb, 1, rb)),
        _const_spec((D_MODEL, D_MODEL)),
        _const_spec((prow, prow)), _const_spec((prow, prow)),
        _const_spec((nb, HIST_ROWS, rb)), _const_spec((nb, V7X_SUBLANES, rb)),
    ]
    out_shape = (
        jax.ShapeDtypeStruct((V7X_SUBLANES, t_len, D_MODEL), BF16),
        jax.ShapeDtypeStruct((nb, HIST_ROWS, rb), F32),
        jax.ShapeDtypeStruct((nb, V7X_SUBLANES, rb), F32),
    )
    out_specs = (
        pl.BlockSpec((V7X_SUBLANES, tt, D_MODEL), lambda j: (0, jnp.maximum(j - 1, 0), 0)),
        pl.BlockSpec((nb, HIST_ROWS, rb), lambda j: (0, 0, 0)),
        pl.BlockSpec((nb, V7X_SUBLANES, rb), lambda j: (0, 0, 0)),
    )
    return pl.pallas_call(
        kern,
        grid=(n_tiles + 1,),
        in_specs=in_specs,
        out_specs=out_specs,
        out_shape=out_shape,
        scratch_shapes=[pltpu.VMEM((rows, D_MODEL), F32),
                        pltpu.VMEM((nb, HIST_ROWS, rb), F32), pltpu.VMEM((nb, V7X_SUBLANES, rb), F32),
                        *extra_scratch],
        compiler_params=pltpu.CompilerParams(dimension_semantics=("arbitrary",),
                                             vmem_limit_bytes=V7X_VMEM_LIMIT),
        name="rnn_branch_start" if seq_start else "rnn_branch",
    )(x, p["ln_g"], p["ln_b"], *([p["w_in"]] * (2 * nb)), p["bx"], p["bg"], p["cw"], p["cb"],
      p["wra"], p["bra"], p["wri"], p["bri"], p["lam"], p["w_rnn_out"], perm, perm_t, hist0, state0)


def _rope_rows(k, cos, sin_signed):
    lane = lax.broadcasted_iota(jnp.int32, k.shape, 1)
    partner = jnp.where((lane % HEAD_DIM) < HALF,
                        pltpu.roll(k, V7X_LANES - HALF, axis=1),
                        pltpu.roll(k, HALF, axis=1))
    return k * cos + partner * sin_signed


def _meta_kv_kernel(m_ref, lng_ref, lnb_ref, wk_ref, bk_ref, wv_ref, bv_ref, cos_ref, sin_ref,
                    k_ref, v_ref):
    h = _layer_norm(m_ref[...], lng_ref[...], lnb_ref[...]).astype(BF16)
    k = _dot(h, wk_ref[...]) + bk_ref[...]
    v = _dot(h, wv_ref[...]) + bv_ref[...]
    for c in range(D_KV // V7X_LANES):
        sl = slice(c * V7X_LANES, (c + 1) * V7X_LANES)
        k_ref[:, sl] = _rope_rows(k[:, sl], cos_ref[...], sin_ref[...]).astype(BF16)
    v_ref[...] = v.astype(BF16)


def _meta_kv(meta, p, cos_rows, sin_rows):
    return pl.pallas_call(
        _meta_kv_kernel,
        out_shape=(jax.ShapeDtypeStruct((N_META, D_KV), BF16),
                   jax.ShapeDtypeStruct((N_META, D_KV), BF16)),
        name="meta_kv",
    )(meta, p["ln_g"], p["ln_b"], p["wk"], p["bk"], p["wv"], p["bv"], cos_rows, sin_rows)


def _attn_kernel(x_ref, lng_ref, lnb_ref, *refs, tq, n_t):
    wq_refs, wga_refs = refs[:N_KV_HEADS], refs[N_KV_HEADS:2 * N_KV_HEADS]
    (bq_ref, bga_ref, wk_ref, bk_ref, wvt_ref, bvt_ref, wout_ref, cost_ref, sint_ref, cosr_ref, sinr_ref,
     kmeta_ref, vmetat_ref, sink_ref, yb_ref, h_ref, h32_ref,
     qt_s, k_s, vt_s, gat_s, kprev_s, vprev_s, ot_s, act_s) = refs[2 * N_KV_HEADS:]
    nblk = tq // BLOCK
    rows_g = GROUP * HEAD_DIM
    j = pl.program_id(0)

    @pl.when(j == 0)
    def _():
        for ref in (qt_s, k_s, vt_s, gat_s, kprev_s, vprev_s, act_s):
            ref[...] = jnp.zeros_like(ref)

    h32 = _layer_norm(x_ref[0], lng_ref[...], lnb_ref[...])
    h = h32.astype(BF16)
    h32_ref[0] = h32
    h_ref[0] = h

    act_prev = act_s[...].astype(BF16)
    out_cols = D_MODEL // (N_KV_HEADS * nblk)

    def project_out(c):
        cs = slice(c * out_cols, (c + 1) * out_cols)
        yb_ref[0, :, cs] = _dot(act_prev, wout_ref[:, cs]).astype(yb_ref.dtype)

    scale = HEAD_DIM ** -0.5
    cos = cost_ref[...] * scale
    sin = sint_ref[...] * scale

    def project_q(g):
        rs = slice(g * rows_g, (g + 1) * rows_g)
        qt = (_dot(h, wq_refs[g][...]) + bq_ref[:, rs]).T
        out = []
        for hh in range(GROUP):
            r0 = hh * HEAD_DIM
            q1, q2 = qt[r0:r0 + HALF], qt[r0 + HALF:r0 + HEAD_DIM]
            out.append((q1 * cos - q2 * sin, q2 * cos + q1 * sin))
        return out

    def store_q(g, roped):
        for hh, (lo, hi) in enumerate(roped):
            r0 = g * rows_g + hh * HEAD_DIM
            qt_s[r0:r0 + HALF, :] = lo
            qt_s[r0 + HALF:r0 + HEAD_DIM, :] = hi

    def project_gate(g):
        rs = slice(g * rows_g, (g + 1) * rows_g)
        gat = _dot(h, wga_refs[g][...]) + bga_ref[:, rs]
        return gat * _sigmoid(gat)

    i = jnp.maximum(j - 1, 0) % n_t
    k = k_s[...].astype(BF16)
    vt = vt_s[...].astype(BF16)
    k_carry, vt_carry = kprev_s[...].astype(BF16), vprev_s[...].astype(BF16)
    kmeta = kmeta_ref[...]
    vmetat = vmetat_ref[...]

    key_row = lax.broadcasted_iota(jnp.int32, (BLOCK, GROUP * BLOCK), 0)
    qry_col = lax.broadcasted_iota(jnp.int32, (BLOCK, GROUP * BLOCK), 1) % BLOCK
    cur_ok = key_row <= qry_col
    first_prev_ok = (key_row > qry_col) & (i > 0)

    def scores(g, jb):
        tok = slice(jb * BLOCK, (jb + 1) * BLOCK)
        ks = slice(g * HEAD_DIM, (g + 1) * HEAD_DIM)
        k_prev = k_carry if jb == 0 else k[(jb - 1) * BLOCK:jb * BLOCK]
        q_g = jnp.concatenate(
            [qt_s[(g * GROUP + hh) * HEAD_DIM:(g * GROUP + hh + 1) * HEAD_DIM, tok].astype(BF16)
             for hh in range(GROUP)], axis=1)
        s_prev = _dot(k_prev[:, ks], q_g)
        if jb == 0:
            s_prev = jnp.where(first_prev_ok, s_prev, NEG_INF)
        s_cm = _dot(jnp.concatenate([k[tok, ks], kmeta[:, ks]], axis=0), q_g)
        s_band = jnp.where(cur_ok, s_cm[:BLOCK], s_prev)
        return s_band, s_cm[BLOCK:]

    def finish(g, jb, s):
        s_band, s_meta = s
        tok = slice(jb * BLOCK, (jb + 1) * BLOCK)
        vs = slice(g * HEAD_DIM, (g + 1) * HEAD_DIM)
        vt_prev = vt_carry if jb == 0 else vt[:, (jb - 1) * BLOCK:jb * BLOCK]
        sink = sink_ref[g]
        m = jnp.maximum(jnp.maximum(jnp.max(s_band, axis=0, keepdims=True),
                                    jnp.max(s_meta, axis=0, keepdims=True)), sink)
        p_band = jnp.exp(s_band - m)
        p_meta = jnp.exp(s_meta - m)
        denom = (jnp.sum(p_band, axis=0, keepdims=True) + jnp.sum(p_meta, axis=0, keepdims=True)
                 + jnp.exp(sink - m))
        p_cur = jnp.where(cur_ok, p_band, 0.0)
        p_prev = jnp.where(cur_ok, 0.0, p_band)
        p_cm = jnp.concatenate([p_cur, p_meta], axis=0).astype(BF16)
        v_cm = jnp.concatenate([vt[vs, tok], vmetat[vs]], axis=1)
        o = _dot(vt_prev[vs], p_prev.astype(BF16)) + _dot(v_cm, p_cm)
        o = o * (1.0 / denom)
        for hh in range(GROUP):
            r0 = (g * GROUP + hh) * HEAD_DIM
            ot_s[r0:r0 + HEAD_DIM, tok] = o[:, hh * BLOCK:(hh + 1) * BLOCK]

    gates = [None] * N_KV_HEADS
    held_q = None
    for g in range(N_KV_HEADS):
        pg = (g - 1) % N_KV_HEADS
        for jb in range(nblk):
            s = scores(g, jb)
            if jb == 0:
                roped = project_q(pg)
                if pg < g:
                    store_q(pg, roped)
                else:
                    held_q = roped
            elif jb == 1:
                gates[pg] = project_gate(pg)
            project_out(g * nblk + jb)
            finish(g, jb, s)
    store_q(N_KV_HEADS - 1, held_q)

    k_new = _dot(h, wk_ref[...]) + bk_ref[...]
    vt_new = _dot_nt(wvt_ref[...], h) + jnp.concatenate([bvt_ref[...]] * (tq // V7X_LANES), axis=1)

    act_s[...] = ot_s[...].T * gat_s[...]

    kprev_s[...] = k_s[tq - BLOCK:, :]
    vprev_s[...] = vt_s[:, tq - BLOCK:]
    for c in range(D_KV // V7X_LANES):
        sl = slice(c * V7X_LANES, (c + 1) * V7X_LANES)
        k_s[:, sl] = _rope_rows(k_new[:, sl], cosr_ref[...], sinr_ref[...])
    for g in range(N_KV_HEADS):
        gat_s[:, g * rows_g:(g + 1) * rows_g] = gates[g]
    vt_s[...] = vt_new


def _attn_branch(x, p, tabs, kmeta, vmetat, *, tq):
    b, t_len, _ = x.shape
    assert tq == 2 * BLOCK
    n_t = t_len // tq
    n_tiles = b * n_t
    kern = functools.partial(_attn_kernel, tq=tq, n_t=n_t)
    proj_tile = lambda j: jnp.minimum(j, n_tiles - 1)
    out_tile = lambda j: jnp.maximum(j - 2, 0)
    cols_g = GROUP * HEAD_DIM
    w_in_block = lambda c: pl.BlockSpec((D_MODEL, cols_g), lambda j: (0, c), pipeline_mode=pl.Buffered(1))
    in_specs = [
        pl.BlockSpec((1, tq, D_MODEL), lambda j: (proj_tile(j) // n_t, proj_tile(j) % n_t, 0)),
        _const_spec((1, D_MODEL)), _const_spec((1, D_MODEL)),
        *[w_in_block(OFF_Q // cols_g + g) for g in range(N_KV_HEADS)],
        *[w_in_block(OFF_GA // cols_g + g) for g in range(N_KV_HEADS)],
        _const_spec((1, D_MODEL)), _const_spec((1, D_MODEL)),
        pl.BlockSpec((D_MODEL, D_KV), lambda j: (0, OFF_K // D_KV), pipeline_mode=pl.Buffered(1)),
        _const_spec((1, D_KV)),
        _const_spec((D_KV, D_MODEL)), _const_spec((D_KV, V7X_LANES)),
        _const_spec((D_MODEL, D_MODEL)),
        pl.BlockSpec((HALF, tq), lambda j: (0, proj_tile(j) % n_t)),
        pl.BlockSpec((HALF, tq), lambda j: (0, proj_tile(j) % n_t)),
        pl.BlockSpec((tq, V7X_LANES), lambda j: (proj_tile(j) % n_t, 0)),
        pl.BlockSpec((tq, V7X_LANES), lambda j: (proj_tile(j) % n_t, 0)),
        _const_spec((N_META, D_KV)), _const_spec((D_KV, N_META)),
        _const_spec((N_KV_HEADS, 1, GROUP * BLOCK)),
    ]
    return pl.pallas_call(
        kern,
        grid=(n_tiles + 2,),
        in_specs=in_specs,
        out_specs=(pl.BlockSpec((1, tq, D_MODEL), lambda j: (out_tile(j) // n_t, out_tile(j) % n_t, 0)),
                   pl.BlockSpec((1, tq, D_MODEL), lambda j: (proj_tile(j) // n_t, proj_tile(j) % n_t, 0)),
                   pl.BlockSpec((1, tq, D_MODEL), lambda j: (proj_tile(j) // n_t, proj_tile(j) % n_t, 0))),
        out_shape=(jax.ShapeDtypeStruct((b, t_len, D_MODEL), BF16),
                   jax.ShapeDtypeStruct((b, t_len, D_MODEL), BF16),
                   jax.ShapeDtypeStruct((b, t_len, D_MODEL), F32)),
        scratch_shapes=[pltpu.VMEM((D_MODEL, tq), F32),
                        pltpu.VMEM((tq, D_KV), F32),
                        pltpu.VMEM((D_KV, tq), F32),
                        pltpu.VMEM((tq, D_MODEL), F32),
                        pltpu.VMEM((BLOCK, D_KV), F32),
                        pltpu.VMEM((D_KV, BLOCK), F32),
                        pltpu.VMEM((D_MODEL, tq), F32),
                        pltpu.VMEM((tq, D_MODEL), F32)],
        compiler_params=pltpu.CompilerParams(dimension_semantics=("arbitrary",),
                                             vmem_limit_bytes=V7X_VMEM_LIMIT),
        name="attn_branch",
    )(x, p["ln_g"], p["ln_b"], *([p["w_in"]] * (2 * N_KV_HEADS)), p["bq"], p["bga"],
      p["w_in"], p["bk"], p["wvt"], p["bvt"], p["w_attn_out"], tabs["cos_t"], tabs["sin_t"], tabs["cos_r"], tabs["sin_r"],
      kmeta, vmetat, p["sink_rows"])


def _merge_kernel(h32_ref, ya_ref, yb_ref, *refs):
    wga_refs, wgb_refs = refs[:MERGE_BLOCKS], refs[MERGE_BLOCKS:2 * MERGE_BLOCKS]
    bmg_ref, wo_ref, bo_ref, g2_ref, b2_ref, out_ref = refs[2 * MERGE_BLOCKS:]
    h = h32_ref[0].astype(BF16)
    mixed = []
    for c in range(MERGE_BLOCKS):
        cs = slice(c * MERGE_COLS, (c + 1) * MERGE_COLS)
        gs = slice(D_MODEL + c * MERGE_COLS, D_MODEL + (c + 1) * MERGE_COLS)
        ga = _sigmoid(_dot(h, wga_refs[c][...]) + bmg_ref[:, cs])
        gb = _sigmoid(_dot(h, wgb_refs[c][...]) + bmg_ref[:, gs])
        mixed.append(ga * ya_ref[0, :, cs].astype(F32) + gb * yb_ref[0, :, cs].astype(F32))
    mixed = jnp.concatenate(mixed, axis=1)
    out = _dot(mixed.astype(BF16), wo_ref[...]) + bo_ref[...]
    out_ref[0] = _layer_norm(DEEPNORM_ALPHA * h32_ref[0] + out, g2_ref[...], b2_ref[...])


def _merge(h32, ya, yb, p, *, tm):
    b, t_len, _ = h32.shape
    tile = pl.BlockSpec((1, tm, D_MODEL), lambda bi, i: (bi, i, 0))
    w_in_block = lambda c: pl.BlockSpec((D_MODEL, MERGE_COLS), lambda bi, i: (0, c), pipeline_mode=pl.Buffered(1))
    in_specs = [tile, tile, tile,
                *[w_in_block(OFF_G // MERGE_COLS + c) for c in range(MERGE_BLOCKS)],
                *[w_in_block((OFF_G + D_MODEL) // MERGE_COLS + c) for c in range(MERGE_BLOCKS)],
                _const_spec((1, 2 * D_MODEL)),
                _const_spec((D_MODEL, D_MODEL)), _const_spec((1, D_MODEL)),
                _const_spec((1, D_MODEL)), _const_spec((1, D_MODEL))]
    return pl.pallas_call(
        _merge_kernel,
        grid=(b, t_len // tm),
        in_specs=in_specs,
        out_specs=tile,
        out_shape=jax.ShapeDtypeStruct((b, t_len, D_MODEL), F32),
        compiler_params=pltpu.CompilerParams(dimension_semantics=("arbitrary", "arbitrary"),
                                             vmem_limit_bytes=V7X_VMEM_LIMIT),
        name="merge",
    )(h32, ya, yb, *([p["w_in"]] * (2 * MERGE_BLOCKS)), p["bmg"], p["w_o"], p["b_o"],
      p["ln2_g"], p["ln2_b"])


def _rope_tables(t_total):
    inv = (ROPE_THETA ** (-np.arange(HALF, dtype=np.float32) / HALF)).astype(np.float32)
    ang = np.arange(t_total, dtype=np.float32)[:, None] * inv[None, :]
    cos, sin = np.cos(ang), np.sin(ang)
    cos_r = np.concatenate([cos, cos, cos, cos], axis=1)
    sin_r = np.concatenate([-sin, sin, -sin, sin], axis=1)
    return cos, sin, cos_r, sin_r


def kernel(x, meta_tokens, ln_emb_g, ln_emb_b, w_in, b_in, conv_w, conv_b, w_ra, b_ra, w_ri, b_ri,
           lru_lambda, sinks, w_rnn_out, w_attn_out, w_o, b_o, ln_g, ln_b):
    b, seq, _ = x.shape
    assert b == V7X_SUBLANES and w_in.shape[0] == DEPTH
    nb, rb = N_RNN_BLOCKS, RNN_BLOCK
    w = w_in[0].astype(BF16)
    bi = b_in[0]
    row = lambda v: v.reshape(1, -1)
    blocks = lambda v: v.reshape(nb, 1, rb)

    common = {"ln_g": row(ln_emb_g), "ln_b": row(ln_emb_b)}
    p_rnn = dict(common, w_in=w,
                 bx=blocks(bi[:OFF_GR]), bg=blocks(bi[OFF_GR:OFF_Q]),
                 cw=conv_w[0].reshape(CONV_WIDTH, nb, rb).transpose(1, 0, 2), cb=blocks(conv_b[0]),
                 wra=w_ra[0].astype(BF16), bra=blocks(b_ra[0]),
                 wri=w_ri[0].astype(BF16), bri=blocks(b_ri[0]),
                 lam=blocks(lru_lambda[0]), w_rnn_out=w_rnn_out[0].astype(BF16))
    p_attn = dict(common,
                  w_in=w, bq=row(bi[OFF_Q:OFF_K]), bga=row(bi[OFF_GA:OFF_G]),
                  wk=w[:, OFF_K:OFF_V], bk=row(bi[OFF_K:OFF_V]),
                  wvt=w[:, OFF_V:OFF_GA].T,
                  bvt=jnp.broadcast_to(bi[OFF_V:OFF_GA, None], (D_KV, V7X_LANES)),
                  wv=w[:, OFF_V:OFF_GA], bv=row(bi[OFF_V:OFF_GA]),
                  w_attn_out=w_attn_out[0].astype(BF16),
                  sink_rows=jnp.repeat(sinks[0].astype(F32), BLOCK).reshape(N_KV_HEADS, 1, GROUP * BLOCK))
    p_merge = dict(common, w_in=w, bmg=row(bi[OFF_G:]),
                   w_o=w_o[0].astype(BF16), b_o=row(b_o[0]), ln2_g=row(ln_g[0]), ln2_b=row(ln_b[0]))

    cos, sin, cos_r, sin_r = _rope_tables(N_META + seq)
    tabs = {"cos_t": np.ascontiguousarray(cos[N_META:].T), "sin_t": np.ascontiguousarray(sin[N_META:].T),
            "cos_r": cos_r[N_META:], "sin_r": sin_r[N_META:]}

    meta = meta_tokens.astype(x.dtype)
    meta_b = jnp.broadcast_to(meta[None], (V7X_SUBLANES, N_META, D_MODEL))
    zero_hist = jnp.zeros((nb, HIST_ROWS, rb), F32)
    zero_state = jnp.zeros((nb, V7X_SUBLANES, rb), F32)
    _, hist0, state0 = _rnn_branch(meta_b, p_rnn, zero_hist, zero_state, tt=N_META, seq_start=True)
    kmeta, vmeta = _meta_kv(meta, p_attn, cos_r[:N_META], sin_r[:N_META])

    yb, h, h32 = _attn_branch(x, p_attn, tabs, kmeta, vmeta.T, tq=ATTN_TILE)
    ya, _, _ = _rnn_branch(h32, p_rnn, hist0, state0, tt=RNN_TILE_STEPS, seq_start=False)
    return _merge(h32, ya, yb, p_merge, tm=MERGE_TILE)
```

```python
import functools

import jax
import jax.numpy as jnp
import numpy as np
from jax import lax
from jax.experimental import pallas as pl
from jax.experimental.pallas import tpu as pltpu

D_MODEL = 2048
N_META = 16
N_RNN_BLOCKS = 8
RNN_BLOCK = 256
CONV_WIDTH = 4
LRU_C = 8.0
HEAD_DIM = 64
HALF = HEAD_DIM // 2
N_Q_HEADS = 32
N_KV_HEADS = 4
GROUP = 8
D_KV = N_KV_HEADS * HEAD_DIM
BLOCK = 128
ROPE_THETA = 10000.0
NEG_INF = -1e30
LN_EPS = 1e-5
DEPTH = 1
DEEPNORM_ALPHA = (2.0 * DEPTH) ** 0.25
OFF_XR = 0
OFF_GR = D_MODEL
OFF_Q = 2 * D_MODEL
OFF_K = OFF_Q + D_MODEL
OFF_V = OFF_K + D_KV
OFF_GA = OFF_V + D_KV
OFF_G = OFF_GA + D_MODEL

V7X_SUBLANES = 8
V7X_LANES = 128
V7X_VMEM_LIMIT = 56 * 1024 * 1024

ATTN_TILE = 2 * BLOCK
RNN_TILE_STEPS = 64
MERGE_TILE = 256

HIST_ROWS = (CONV_WIDTH - 1) * V7X_SUBLANES
RNN_PERM_STEPS = 32
MERGE_COLS = 512
MERGE_BLOCKS = D_MODEL // MERGE_COLS

BF16 = jnp.bfloat16
F32 = jnp.float32


def _layer_norm(x, g, b):
    mu = jnp.mean(x, axis=-1, keepdims=True)
    xc = x - mu
    var = jnp.mean(xc * xc, axis=-1, keepdims=True)
    return xc * lax.rsqrt(var + LN_EPS) * g + b


def _sigmoid(x):
    return 1.0 / (1.0 + jnp.exp(-x))


def _dot(a, b):
    return jnp.dot(a, b, preferred_element_type=F32)


def _dot_nt(a, b):
    return lax.dot_general(a, b, (((1,), (1,)), ((), ())), preferred_element_type=F32)


def _const_spec(shape):
    nd = len(shape)
    return pl.BlockSpec(shape, lambda *_: (0,) * nd, pipeline_mode=pl.Buffered(1))


def _rnn_kernel(x_ref, lng_ref, lnb_ref, *refs, tt, seq_start, normalised_input, strided_input):
    wx_refs, wg_refs = refs[:N_RNN_BLOCKS], refs[N_RNN_BLOCKS:2 * N_RNN_BLOCKS]
    (bx_ref, bg_ref, cw_ref, cb_ref, wra_ref, bra_ref, wri_ref, bri_ref, lam_ref, wout_ref, perm_ref,
     permt_ref, hist0_ref, state0_ref, ya_ref, hist_out_ref, state_out_ref,
     y_s, hist_s, state_s) = refs[2 * N_RNN_BLOCKS:2 * N_RNN_BLOCKS + 20]
    rows = tt * V7X_SUBLANES
    nb = N_RNN_BLOCKS
    if strided_input:
        xbuf, xsem = refs[2 * N_RNN_BLOCKS + 20:]

        def tile_copy(tile, slot, bb):
            t0 = pl.multiple_of(tile * tt, tt)
            return pltpu.make_async_copy(x_ref.at[bb, pl.ds(t0, tt), :], xbuf.at[slot, :, bb, :], xsem.at[slot, bb])
    j = pl.program_id(0)

    @pl.when(j == 0)
    def _():
        y_s[...] = jnp.zeros_like(y_s)
        hist_s[...] = hist0_ref[...]
        state_s[...] = state0_ref[...]
        if strided_input:
            for bb in range(V7X_SUBLANES):
                tile_copy(0, 0, bb).start()

    if strided_input:
        @pl.when(j + 1 < pl.num_programs(0) - 1)
        def _():
            for bb in range(V7X_SUBLANES):
                tile_copy(j + 1, (j + 1) % 2, bb).start()

    ts = perm_ref.shape[0] // V7X_SUBLANES
    sub = ts * V7X_SUBLANES

    def project_out(y_prev, n):
        cs = slice(n * RNN_BLOCK, (n + 1) * RNN_BLOCK)
        ya = _dot(y_prev, wout_ref[:, cs])
        for k, s0 in enumerate(range(0, tt, ts)):
            ya_ref[:, s0:s0 + ts, cs] = (ya[k * sub:(k + 1) * sub]
                                         .reshape(V7X_SUBLANES, ts, RNN_BLOCK).astype(ya_ref.dtype))

    def scan_step():
        if strided_input:
            slot = j % 2
            for bb in range(V7X_SUBLANES):
                tile_copy(j, slot, bb).wait()
            h = xbuf[slot].reshape(rows, D_MODEL).astype(BF16)
        else:
            parts = []
            for s0 in range(0, tt, ts):
                hs_ = x_ref[:, s0:s0 + ts, :].reshape(ts * V7X_SUBLANES, D_MODEL)
                if not normalised_input:
                    hs_ = _layer_norm(hs_, lng_ref[...], lnb_ref[...]).astype(BF16)
                parts.append(_dot(perm_ref[...], hs_).astype(BF16))
            h = jnp.concatenate(parts, axis=0)

        def project(n):
            return _dot(h, wx_refs[n][...]) + bx_ref[n], _dot(h, wg_refs[n][...]) + bg_ref[n]

        y_prev = y_s[...].astype(BF16)
        ys = []
        nxt = project(0)
        for n in range(nb):
            xr, gr = nxt
            if n + 1 < nb:
                nxt = project(n + 1)
            project_out(y_prev, n)
            xe = jnp.concatenate([hist_s[n], xr], axis=0)
            cw = cw_ref[n]
            conv = cb_ref[n] + cw[0:1] * xr
            for k in range(1, CONV_WIDTH):
                off = HIST_ROWS - k * V7X_SUBLANES
                conv = conv + cw[k:k + 1] * xe[off:off + rows]
            hist_s[n] = xe[rows:rows + HIST_ROWS]
            cb16 = conv.astype(BF16)
            gate_r = _sigmoid(_dot(cb16, wra_ref[n]) + bra_ref[n])
            gate_i = _sigmoid(_dot(cb16, wri_ref[n]) + bri_ref[n])
            lam = lam_ref[n]
            log_sig = jnp.minimum(lam, 0.0) - jnp.log1p(jnp.exp(-jnp.abs(lam)))
            a = jnp.exp((LRU_C * gate_r) * log_sig)
            mult = jnp.sqrt(1.0 - a * a)
            if seq_start:
                first = lax.broadcasted_iota(jnp.int32, (rows, RNN_BLOCK), 0) < V7X_SUBLANES
                mult = jnp.where(first & (j == 0), 1.0, mult)
            u = mult * gate_i * conv
            hcur = state_s[n]
            hs = []
            for t in range(tt):
                sl = slice(t * V7X_SUBLANES, (t + 1) * V7X_SUBLANES)
                hcur = a[sl] * hcur + u[sl]
                hs.append(hcur)
            state_s[n] = hcur
            hr = jnp.concatenate(hs, axis=0)
            ys.append((hr * (gr * _sigmoid(gr))).astype(BF16))
        y = jnp.concatenate(ys, axis=1)
        for r0 in range(0, rows, sub):
            y_s[r0:r0 + sub, :] = _dot(permt_ref[...], y[r0:r0 + sub])

    def drain_step():
        y_prev = y_s[...].astype(BF16)
        for n in range(nb):
            project_out(y_prev, n)

    last = pl.num_programs(0) - 1
    pl.when(j < last)(scan_step)
    pl.when(j == last)(drain_step)

    @pl.when(j == pl.num_programs(0) - 2)
    def _():
        hist_out_ref[...] = hist_s[...]
        state_out_ref[...] = state_s[...]


def _rnn_branch(x, p, hist0, state0, *, tt, seq_start):
    t_len = x.shape[1]
    nb, rb = N_RNN_BLOCKS, RNN_BLOCK
    rows = tt * V7X_SUBLANES
    ts = min(tt, RNN_PERM_STEPS)
    prow = ts * V7X_SUBLANES
    r = np.arange(prow)
    perm = np.zeros((prow, prow), np.float32)
    perm[r, (r % V7X_SUBLANES) * ts + r // V7X_SUBLANES] = 1.0
    perm, perm_t = jnp.asarray(perm, BF16), jnp.asarray(perm.T, BF16)
    strided = not seq_start
    kern = functools.partial(_rnn_kernel, tt=tt, seq_start=seq_start, normalised_input=strided,
                             strided_input=strided)
    n_tiles = t_len // tt
    w_in_block = lambda c: pl.BlockSpec((D_MODEL, rb), lambda j: (0, c), pipeline_mode=pl.Buffered(1))
    x_spec = (pl.BlockSpec(memory_space=pl.ANY) if strided else
              pl.BlockSpec((V7X_SUBLANES, tt, D_MODEL), lambda j: (0, jnp.minimum(j, n_tiles - 1), 0)))
    extra_scratch = ([pltpu.VMEM((2, tt, V7X_SUBLANES, D_MODEL), F32), pltpu.SemaphoreType.DMA((2, V7X_SUBLANES))]
                     if strided else [])
    in_specs = [
        x_spec,
        _const_spec((1, D_MODEL)), _const_spec((1, D_MODEL)),
        *[w_in_block(OFF_XR // rb + n) for n in range(nb)],
        *[w_in_block(OFF_GR // rb + n) for n in range(nb)],
        _const_spec((nb, 1, rb)), _const_spec((nb, 1, rb)),
        _const_spec((nb, CONV_WIDTH, rb)), _const_spec((nb, 1, rb)),
        _const_spec((nb, rb, rb)), _const_spec((nb, 1, rb)),
        _const_spec((nb, rb, rb)), _const_spec((nb, 1, rb)),
        _const_spec((nb, 1, rb)),
        _const_spec((D_MODEL, D_MODEL)),
        _const_spec((prow, prow)), _const_spec((prow, prow)),
        _const_spec((nb, HIST_ROWS, rb)), _const_spec((nb, V7X_SUBLANES, rb)),
    ]
    out_shape = (
        jax.ShapeDtypeStruct((V7X_SUBLANES, t_len, D_MODEL), BF16),
        jax.ShapeDtypeStruct((nb, HIST_ROWS, rb), F32),
        jax.ShapeDtypeStruct((nb, V7X_SUBLANES, rb), F32),
    )
    out_specs = (
        pl.BlockSpec((V7X_SUBLANES, tt, D_MODEL), lambda j: (0, jnp.maximum(j - 1, 0), 0)),
        pl.BlockSpec((nb, HIST_ROWS, rb), lambda j: (0, 0, 0)),
        pl.BlockSpec((nb, V7X_SUBLANES, rb), lambda j: (0, 0, 0)),
    )
    return pl.pallas_call(
        kern,
        grid=(n_tiles + 1,),
        in_specs=in_specs,
        out_specs=out_specs,
        out_shape=out_shape,
        scratch_shapes=[pltpu.VMEM((rows, D_MODEL), F32),
                        pltpu.VMEM((nb, HIST_ROWS, rb), F32), pltpu.VMEM((nb, V7X_SUBLANES, rb), F32),
                        *extra_scratch],
        compiler_params=pltpu.CompilerParams(dimension_semantics=("arbitrary",),
                                             vmem_limit_bytes=V7X_VMEM_LIMIT),
        name="rnn_branch_start" if seq_start else "rnn_branch",
    )(x, p["ln_g"], p["ln_b"], *([p["w_in"]] * (2 * nb)), p["bx"], p["bg"], p["cw"], p["cb"],
      p["wra"], p["bra"], p["wri"], p["bri"], p["lam"], p["w_rnn_out"], perm, perm_t, hist0, state0)


def _rope_rows(k, cos, sin_signed):
    lane = lax.broadcasted_iota(jnp.int32, k.shape, 1)
    partner = jnp.where((lane % HEAD_DIM) < HALF,
                        pltpu.roll(k, V7X_LANES - HALF, axis=1),
                        pltpu.roll(k, HALF, axis=1))
    return k * cos + partner * sin_signed


def _meta_kv_kernel(m_ref, lng_ref, lnb_ref, wk_ref, bk_ref, wv_ref, bv_ref, cos_ref, sin_ref,
                    k_ref, v_ref):
    h = _layer_norm(m_ref[...], lng_ref[...], lnb_ref[...]).astype(BF16)
    k = _dot(h, wk_ref[...]) + bk_ref[...]
    v = _dot(h, wv_ref[...]) + bv_ref[...]
    for c in range(D_KV // V7X_LANES):
        sl = slice(c * V7X_LANES, (c + 1) * V7X_LANES)
        k_ref[:, sl] = _rope_rows(k[:, sl], cos_ref[...], sin_ref[...]).astype(BF16)
    v_ref[...] = v.astype(BF16)


def _meta_kv(meta, p, cos_rows, sin_rows):
    return pl.pallas_call(
        _meta_kv_kernel,
        out_shape=(jax.ShapeDtypeStruct((N_META, D_KV), BF16),
                   jax.ShapeDtypeStruct((N_META, D_KV), BF16)),
        name="meta_kv",
    )(meta, p["ln_g"], p["ln_b"], p["wk"], p["bk"], p["wv"], p["bv"], cos_rows, sin_rows)


def _attn_kernel(x_ref, lng_ref, lnb_ref, *refs, tq, n_t):
    wq_refs, wga_refs = refs[:N_KV_HEADS], refs[N_KV_HEADS:2 * N_KV_HEADS]
    (bq_ref, bga_ref, wk_ref, bk_ref, wvt_ref, bvt_ref, wout_ref, cost_ref, sint_ref, cosr_ref, sinr_ref,
     kmeta_ref, vmetat_ref, sink_ref, yb_ref, h32_ref,
     qt_s, k_s, vt_s, gat_s, kprev_s, vprev_s, ot_s, act_s) = refs[2 * N_KV_HEADS:]
    nblk = tq // BLOCK
    rows_g = GROUP * HEAD_DIM
    j = pl.program_id(0)

    @pl.when(j == 0)
    def _():
        for ref in (qt_s, k_s, vt_s, gat_s, kprev_s, vprev_s, act_s):
            ref[...] = jnp.zeros_like(ref)

    h32 = _layer_norm(x_ref[0], lng_ref[...], lnb_ref[...])
    h = h32.astype(BF16)
    h32_ref[0] = h32

    act_prev = act_s[...].astype(BF16)
    out_cols = D_MODEL // (N_KV_HEADS * nblk)

    def project_out(c):
        cs = slice(c * out_cols, (c + 1) * out_cols)
        yb_ref[0, :, cs] = _dot(act_prev, wout_ref[:, cs]).astype(yb_ref.dtype)

    scale = HEAD_DIM ** -0.5
    cos = cost_ref[...] * scale
    sin = sint_ref[...] * scale

    def project_q(g):
        rs = slice(g * rows_g, (g + 1) * rows_g)
        qt = (_dot(h, wq_refs[g][...]) + bq_ref[:, rs]).T
        out = []
        for hh in range(GROUP):
            r0 = hh * HEAD_DIM
            q1, q2 = qt[r0:r0 + HALF], qt[r0 + HALF:r0 + HEAD_DIM]
            out.append((q1 * cos - q2 * sin, q2 * cos + q1 * sin))
        return out

    def store_q(g, roped):
        for hh, (lo, hi) in enumerate(roped):
            r0 = g * rows_g + hh * HEAD_DIM
            qt_s[r0:r0 + HALF, :] = lo
            qt_s[r0 + HALF:r0 + HEAD_DIM, :] = hi

    def project_gate(g):
        rs = slice(g * rows_g, (g + 1) * rows_g)
        gat = _dot(h, wga_refs[g][...]) + bga_ref[:, rs]
        return gat * _sigmoid(gat)

    i = jnp.maximum(j - 1, 0) % n_t
    k = k_s[...].astype(BF16)
    vt = vt_s[...].astype(BF16)
    k_carry, vt_carry = kprev_s[...].astype(BF16), vprev_s[...].astype(BF16)
    kmeta = kmeta_ref[...]
    vmetat = vmetat_ref[...]

    key_row = lax.broadcasted_iota(jnp.int32, (BLOCK, GROUP * BLOCK), 0)
    qry_col = lax.broadcasted_iota(jnp.int32, (BLOCK, GROUP * BLOCK), 1) % BLOCK
    cur_ok = key_row <= qry_col
    first_prev_ok = (key_row > qry_col) & (i > 0)

    def scores(g, jb):
        tok = slice(jb * BLOCK, (jb + 1) * BLOCK)
        ks = slice(g * HEAD_DIM, (g + 1) * HEAD_DIM)
        k_prev = k_carry if jb == 0 else k[(jb - 1) * BLOCK:jb * BLOCK]
        q_g = jnp.concatenate(
            [qt_s[(g * GROUP + hh) * HEAD_DIM:(g * GROUP + hh + 1) * HEAD_DIM, tok].astype(BF16)
             for hh in range(GROUP)], axis=1)
        s_prev = _dot(k_prev[:, ks], q_g)
        if jb == 0:
            s_prev = jnp.where(first_prev_ok, s_prev, NEG_INF)
        s_cm = _dot(jnp.concatenate([k[tok, ks], kmeta[:, ks]], axis=0), q_g)
        s_band = jnp.where(cur_ok, s_cm[:BLOCK], s_prev)
        return s_band, s_cm[BLOCK:]

    def finish(g, jb, s):
        s_band, s_meta = s
        tok = slice(jb * BLOCK, (jb + 1) * BLOCK)
        vs = slice(g * HEAD_DIM, (g + 1) * HEAD_DIM)
        vt_prev = vt_carry if jb == 0 else vt[:, (jb - 1) * BLOCK:jb * BLOCK]
        sink = sink_ref[g]
        m = jnp.maximum(jnp.maximum(jnp.max(s_band, axis=0, keepdims=True),
                                    jnp.max(s_meta, axis=0, keepdims=True)), sink)
        p_band = jnp.exp(s_band - m)
        p_meta = jnp.exp(s_meta - m)
        denom = (jnp.sum(p_band, axis=0, keepdims=True) + jnp.sum(p_meta, axis=0, keepdims=True)
                 + jnp.exp(sink - m))
        p_cur = jnp.where(cur_ok, p_band, 0.0)
        p_prev = jnp.where(cur_ok, 0.0, p_band)
        p_cm = jnp.concatenate([p_cur, p_meta], axis=0).astype(BF16)
        v_cm = jnp.concatenate([vt[vs, tok], vmetat[vs]], axis=1)
        o = _dot(vt_prev[vs], p_prev.astype(BF16)) + _dot(v_cm, p_cm)
        o = o * (1.0 / denom)
        for hh in range(GROUP):
            r0 = (g * GROUP + hh) * HEAD_DIM
            ot_s[r0:r0 + HEAD_DIM, tok] = o[:, hh * BLOCK:(hh + 1) * BLOCK]

    gates = [None] * N_KV_HEADS
    held_q = None
    for g in range(N_KV_HEADS):
        pg = (g - 1) % N_KV_HEADS
        for jb in range(nblk):
            s = scores(g, jb)
            if jb == 0:
                roped = project_q(pg)
                if pg < g:
                    store_q(pg, roped)
                else:
                    held_q = roped
            elif jb == 1:
                gates[pg] = project_gate(pg)
            project_out(g * nblk + jb)
            finish(g, jb, s)
    store_q(N_KV_HEADS - 1, held_q)

    k_new = _dot(h, wk_ref[...]) + bk_ref[...]
    vt_new = _dot_nt(wvt_ref[...], h) + jnp.concatenate([bvt_ref[...]] * (tq // V7X_LANES), axis=1)

    act_s[...] = ot_s[...].T * gat_s[...]

    kprev_s[...] = k_s[tq - BLOCK:, :]
    vprev_s[...] = vt_s[:, tq - BLOCK:]
    for c in range(D_KV // V7X_LANES):
        sl = slice(c * V7X_LANES, (c + 1) * V7X_LANES)
        k_s[:, sl] = _rope_rows(k_new[:, sl], cosr_ref[...], sinr_ref[...])
    for g in range(N_KV_HEADS):
        gat_s[:, g * rows_g:(g + 1) * rows_g] = gates[g]
    vt_s[...] = vt_new


def _attn_branch(x, p, tabs, kmeta, vmetat, *, tq):
    b, t_len, _ = x.shape
    assert tq == 2 * BLOCK
    n_t = t_len // tq
    n_tiles = b * n_t
    kern = functools.partial(_attn_kernel, tq=tq, n_t=n_t)
    proj_tile = lambda j: jnp.minimum(j, n_tiles - 1)
    out_tile = lambda j: jnp.maximum(j - 2, 0)
    cols_g = GROUP * HEAD_DIM
    w_in_block = lambda c: pl.BlockSpec((D_MODEL, cols_g), lambda j: (0, c), pipeline_mode=pl.Buffered(1))
    in_specs = [
        pl.BlockSpec((1, tq, D_MODEL), lambda j: (proj_tile(j) // n_t, proj_tile(j) % n_t, 0)),
        _const_spec((1, D_MODEL)), _const_spec((1, D_MODEL)),
        *[w_in_block(OFF_Q // cols_g + g) for g in range(N_KV_HEADS)],
        *[w_in_block(OFF_GA // cols_g + g) for g in range(N_KV_HEADS)],
        _const_spec((1, D_MODEL)), _const_spec((1, D_MODEL)),
        pl.BlockSpec((D_MODEL, D_KV), lambda j: (0, OFF_K // D_KV), pipeline_mode=pl.Buffered(1)),
        _const_spec((1, D_KV)),
        _const_spec((D_KV, D_MODEL)), _const_spec((D_KV, V7X_LANES)),
        _const_spec((D_MODEL, D_MODEL)),
        pl.BlockSpec((HALF, tq), lambda j: (0, proj_tile(j) % n_t)),
        pl.BlockSpec((HALF, tq), lambda j: (0, proj_tile(j) % n_t)),
        pl.BlockSpec((tq, V7X_LANES), lambda j: (proj_tile(j) % n_t, 0)),
        pl.BlockSpec((tq, V7X_LANES), lambda j: (proj_tile(j) % n_t, 0)),
        _const_spec((N_META, D_KV)), _const_spec((D_KV, N_META)),
        _const_spec((N_KV_HEADS, 1, GROUP * BLOCK)),
    ]
    return pl.pallas_call(
        kern,
        grid=(n_tiles + 2,),
        in_specs=in_specs,
        out_specs=(pl.BlockSpec((1, tq, D_MODEL), lambda j: (out_tile(j) // n_t, out_tile(j) % n_t, 0)),
                   pl.BlockSpec((1, tq, D_MODEL), lambda j: (proj_tile(j) // n_t, proj_tile(j) % n_t, 0))),
        out_shape=(jax.ShapeDtypeStruct((b, t_len, D_MODEL), BF16),
                   jax.ShapeDtypeStruct((b, t_len, D_MODEL), F32)),
        scratch_shapes=[pltpu.VMEM((D_MODEL, tq), F32),
                        pltpu.VMEM((tq, D_KV), F32),
                        pltpu.VMEM((D_KV, tq), F32),
                        pltpu.VMEM((tq, D_MODEL), F32),
                        pltpu.VMEM((BLOCK, D_KV), F32),
                        pltpu.VMEM((D_KV, BLOCK), F32),
                        pltpu.VMEM((D_MODEL, tq), F32),
                        pltpu.VMEM((tq, D_MODEL), F32)],
        compiler_params=pltpu.CompilerParams(dimension_semantics=("arbitrary",),
                                             vmem_limit_bytes=V7X_VMEM_LIMIT),
        name="attn_branch",
    )(x, p["ln_g"], p["ln_b"], *([p["w_in"]] * (2 * N_KV_HEADS)), p["bq"], p["bga"],
      p["w_in"], p["bk"], p["wvt"], p["bvt"], p["w_attn_out"], tabs["cos_t"], tabs["sin_t"], tabs["cos_r"], tabs["sin_r"],
      kmeta, vmetat, p["sink_rows"])


def _merge_kernel(h32_ref, ya_ref, yb_ref, *refs):
    wga_refs, wgb_refs = refs[:MERGE_BLOCKS], refs[MERGE_BLOCKS:2 * MERGE_BLOCKS]
    bmg_ref, wo_ref, bo_ref, g2_ref, b2_ref, out_ref = refs[2 * MERGE_BLOCKS:]
    h = h32_ref[0].astype(BF16)
    mixed = []
    for c in range(MERGE_BLOCKS):
        cs = slice(c * MERGE_COLS, (c + 1) * MERGE_COLS)
        gs = slice(D_MODEL + c * MERGE_COLS, D_MODEL + (c + 1) * MERGE_COLS)
        ga = _sigmoid(_dot(h, wga_refs[c][...]) + bmg_ref[:, cs])
        gb = _sigmoid(_dot(h, wgb_refs[c][...]) + bmg_ref[:, gs])
        mixed.append(ga * ya_ref[0, :, cs].astype(F32) + gb * yb_ref[0, :, cs].astype(F32))
    mixed = jnp.concatenate(mixed, axis=1)
    out = _dot(mixed.astype(BF16), wo_ref[...]) + bo_ref[...]
    out_ref[0] = _layer_norm(DEEPNORM_ALPHA * h32_ref[0] + out, g2_ref[...], b2_ref[...])


def _merge(h32, ya, yb, p, *, tm):
    b, t_len, _ = h32.shape
    tile = pl.BlockSpec((1, tm, D_MODEL), lambda bi, i: (bi, i, 0))
    w_in_block = lambda c: pl.BlockSpec((D_MODEL, MERGE_COLS), lambda bi, i: (0, c), pipeline_mode=pl.Buffered(1))
    in_specs = [tile, tile, tile,
                *[w_in_block(OFF_G // MERGE_COLS + c) for c in range(MERGE_BLOCKS)],
                *[w_in_block((OFF_G + D_MODEL) // MERGE_COLS + c) for c in range(MERGE_BLOCKS)],
                _const_spec((1, 2 * D_MODEL)),
                _const_spec((D_MODEL, D_MODEL)), _const_spec((1, D_MODEL)),
                _const_spec((1, D_MODEL)), _const_spec((1, D_MODEL))]
    return pl.pallas_call(
        _merge_kernel,
        grid=(b, t_len // tm),
        in_specs=in_specs,
        out_specs=tile,
        out_shape=jax.ShapeDtypeStruct((b, t_len, D_MODEL), F32),
        compiler_params=pltpu.CompilerParams(dimension_semantics=("arbitrary", "arbitrary"),
                                             vmem_limit_bytes=V7X_VMEM_LIMIT),
        name="merge",
    )(h32, ya, yb, *([p["w_in"]] * (2 * MERGE_BLOCKS)), p["bmg"], p["w_o"], p["b_o"],
      p["ln2_g"], p["ln2_b"])


def _rope_tables(t_total):
    inv = (ROPE_THETA ** (-np.arange(HALF, dtype=np.float32) / HALF)).astype(np.float32)
    ang = np.arange(t_total, dtype=np.float32)[:, None] * inv[None, :]
    cos, sin = np.cos(ang), np.sin(ang)
    cos_r = np.concatenate([cos, cos, cos, cos], axis=1)
    sin_r = np.concatenate([-sin, sin, -sin, sin], axis=1)
    return cos, sin, cos_r, sin_r


def kernel(x, meta_tokens, ln_emb_g, ln_emb_b, w_in, b_in, conv_w, conv_b, w_ra, b_ra, w_ri, b_ri,
           lru_lambda, sinks, w_rnn_out, w_attn_out, w_o, b_o, ln_g, ln_b):
    b, seq, _ = x.shape
    assert b == V7X_SUBLANES and w_in.shape[0] == DEPTH
    nb, rb = N_RNN_BLOCKS, RNN_BLOCK
    w = w_in[0].astype(BF16)
    bi = b_in[0]
    row = lambda v: v.reshape(1, -1)
    blocks = lambda v: v.reshape(nb, 1, rb)

    common = {"ln_g": row(ln_emb_g), "ln_b": row(ln_emb_b)}
    p_rnn = dict(common, w_in=w,
                 bx=blocks(bi[:OFF_GR]), bg=blocks(bi[OFF_GR:OFF_Q]),
                 cw=conv_w[0].reshape(CONV_WIDTH, nb, rb).transpose(1, 0, 2), cb=blocks(conv_b[0]),
                 wra=w_ra[0].astype(BF16), bra=blocks(b_ra[0]),
                 wri=w_ri[0].astype(BF16), bri=blocks(b_ri[0]),
                 lam=blocks(lru_lambda[0]), w_rnn_out=w_rnn_out[0].astype(BF16))
    p_attn = dict(common,
                  w_in=w, bq=row(bi[OFF_Q:OFF_K]), bga=row(bi[OFF_GA:OFF_G]),
                  wk=w[:, OFF_K:OFF_V], bk=row(bi[OFF_K:OFF_V]),
                  wvt=w[:, OFF_V:OFF_GA].T,
                  bvt=jnp.broadcast_to(bi[OFF_V:OFF_GA, None], (D_KV, V7X_LANES)),
                  wv=w[:, OFF_V:OFF_GA], bv=row(bi[OFF_V:OFF_GA]),
                  w_attn_out=w_attn_out[0].astype(BF16),
                  sink_rows=jnp.repeat(sinks[0].astype(F32), BLOCK).reshape(N_KV_HEADS, 1, GROUP * BLOCK))
    p_merge = dict(common, w_in=w, bmg=row(bi[OFF_G:]),
                   w_o=w_o[0].astype(BF16), b_o=row(b_o[0]), ln2_g=row(ln_g[0]), ln2_b=row(ln_b[0]))

    cos, sin, cos_r, sin_r = _rope_tables(N_META + seq)
    tabs = {"cos_t": np.ascontiguousarray(cos[N_META:].T), "sin_t": np.ascontiguousarray(sin[N_META:].T),
            "cos_r": cos_r[N_META:], "sin_r": sin_r[N_META:]}

    meta = meta_tokens.astype(x.dtype)
    meta_b = jnp.broadcast_to(meta[None], (V7X_SUBLANES, N_META, D_MODEL))
    zero_hist = jnp.zeros((nb, HIST_ROWS, rb), F32)
    zero_state = jnp.zeros((nb, V7X_SUBLANES, rb), F32)
    _, hist0, state0 = _rnn_branch(meta_b, p_rnn, zero_hist, zero_state, tt=N_META, seq_start=True)
    kmeta, vmeta = _meta_kv(meta, p_attn, cos_r[:N_META], sin_r[:N_META])

    yb, h32 = _attn_branch(x, p_attn, tabs, kmeta, vmeta.T, tq=ATTN_TILE)
    ya, _, _ = _rnn_branch(h32, p_rnn, hist0, state0, tt=RNN_TILE_STEPS, seq_start=False)
    return _merge(h32, ya, yb, p_merge, tm=MERGE_TILE)
```
